```python
import jax, jax.numpy as jnp
from jax import lax
import numpy as np

D_MODEL = 1024
BATCH = 16
SEQ = 2048
DEPTH = 2

N_A_LAYERS = DEPTH // 2
N_B_LAYERS = DEPTH - N_A_LAYERS
N_META = 16
EPS = 1e-6
NEG_INF = -1e30

D_FF = 2816

GLA_HEADS = 4
GLA_DK = D_MODEL // 2 // GLA_HEADS
GLA_DV = D_MODEL // GLA_HEADS
GLA_QK = GLA_HEADS * GLA_DK
GLA_V = GLA_HEADS * GLA_DV
GLA_RANK = 16
GLA_TAU = 16.0
GLA_IN = 2 * GLA_QK + 2 * GLA_V + GLA_RANK
CHUNK = 64

N_Q_HEADS = 16
N_KV_HEADS = 4
HEAD_DIM = 64
GROUP = N_Q_HEADS // N_KV_HEADS
WINDOW = 128
ROT_DIM = HEAD_DIM // 4
ROPE_THETA = 500000.0

kernel_name = "yoco_gla_swa_sink_macaron"


def rmsnorm(x, g):
    xf = x.astype(jnp.float32)
    y = xf * lax.rsqrt(jnp.mean(xf * xf, axis=-1, keepdims=True) + EPS)
    return (y * g.astype(jnp.float32)).astype(x.dtype)


def swiglu(x, w_in, w_out):
    gate, up = jnp.split(x @ w_in, 2, axis=-1)
    return (jax.nn.silu(gate) * up) @ w_out


def rope_tables(length):
    inv_freq = ROPE_THETA ** (-jnp.arange(0, ROT_DIM, 2, dtype=jnp.float32) / ROT_DIM)
    ang = jnp.arange(length, dtype=jnp.float32)[:, None] * inv_freq[None, :]
    return jnp.cos(ang), jnp.sin(ang)


def apply_partial_rope(x, cos, sin):
    half = ROT_DIM // 2
    shape = (cos.shape[0],) + (1,) * (x.ndim - 3) + (half,)
    c, s = cos.reshape(shape), sin.reshape(shape)
    x1 = x[..., :half].astype(jnp.float32)
    x2 = x[..., half:ROT_DIM].astype(jnp.float32)
    return jnp.concatenate([(x1 * c - x2 * s).astype(x.dtype), (x2 * c + x1 * s).astype(x.dtype),
                            x[..., ROT_DIM:]], axis=-1)


def gla_mixer(hn, w_in, w_gate, b_gate, g_head, w_out):
    B, L, _ = hn.shape
    q, k, v, r, lr = jnp.split(hn @ w_in, [GLA_QK, 2 * GLA_QK, 2 * GLA_QK + GLA_V, 2 * GLA_QK + 2 * GLA_V], axis=-1)
    gk = jax.nn.log_sigmoid((lr @ w_gate + b_gate).astype(jnp.float32)) / GLA_TAU
    pad = CHUNK - N_META
    padf = lambda t: jnp.pad(t, ((0, 0), (pad, 0), (0, 0)))
    Lp = L + pad
    nC = Lp // CHUNK
    def chunks(t, d):
        return t.reshape(B, nC, CHUNK, GLA_HEADS, d).transpose(0, 3, 1, 2, 4)
    qf = chunks(padf(q).astype(jnp.float32), GLA_DK) * (GLA_DK ** -0.5)
    kf = chunks(padf(k).astype(jnp.float32), GLA_DK)
    vf = chunks(padf(v).astype(jnp.float32), GLA_DV)
    bcum = jnp.cumsum(chunks(padf(gk), GLA_DK), axis=3)
    q_dec = qf * jnp.exp(bcum)
    k_dec = kf * jnp.exp(-bcum)
    causal = jnp.tril(jnp.ones((CHUNK, CHUNK), dtype=bool))
    att = jnp.where(causal, jnp.einsum('bhncd,bhnsd->bhncs', q_dec, k_dec), 0.0)
    o_intra = jnp.einsum('bhncs,bhnsv->bhncv', att, vf)
    b_last = bcum[:, :, :, -1:, :]
    chunk_kv = jnp.einsum('bhnsd,bhnsv->bhndv', kf * jnp.exp(b_last - bcum), vf)
    decay = jnp.exp(b_last[:, :, :, 0, :])

    def step(S, inp):
        kv_c, dec_c = inp
        return dec_c[..., None] * S + kv_c, S
    S0 = jnp.zeros((B, GLA_HEADS, GLA_DK, GLA_DV), jnp.float32)
    _, states = lax.scan(step, S0, (jnp.moveaxis(chunk_kv, 2, 0), jnp.moveaxis(decay, 2, 0)))
    states = jnp.moveaxis(states, 0, 2)
    o = o_intra + jnp.einsum('bhncd,bhndv->bhncv', q_dec, states)
    o = o.reshape(B, GLA_HEADS, Lp, GLA_DV)[:, :, pad:].transpose(0, 2, 1, 3)
    o = rmsnorm(o, g_head) * jax.nn.silu(r.astype(jnp.float32)).reshape(B, L, GLA_HEADS, GLA_DV)
    return o.reshape(B, L, GLA_V).astype(hn.dtype) @ w_out


def shared_kv(h, g_kv, w_kv, cos, sin):
    B, L, _ = h.shape
    k, v = jnp.split(rmsnorm(h, g_kv) @ w_kv, 2, axis=-1)
    k = apply_partial_rope(k.reshape(B, L, N_KV_HEADS, HEAD_DIM), cos, sin)
    v = v.reshape(B, L, N_KV_HEADS, HEAD_DIM)
    return k.transpose(0, 2, 1, 3), v.transpose(0, 2, 1, 3)


def sink_softmax(s, mask, sink):
    s = jnp.where(mask, s.astype(jnp.float32), NEG_INF)
    m = jnp.maximum(jnp.max(s, axis=-1, keepdims=True), sink)
    p = jnp.exp(s - m)
    return p / (jnp.sum(p, axis=-1, keepdims=True) + jnp.exp(sink - m))


def swa_mixer(hn, w_q, sinks, w_out, k, v, cos, sin):
    B, L, _ = hn.shape
    S = L - N_META
    nB = S // WINDOW
    q = (hn @ w_q).reshape(B, L, N_KV_HEADS, GROUP, HEAD_DIM)
    q = (apply_partial_rope(q, cos, sin) * (HEAD_DIM ** -0.5)).transpose(0, 2, 3, 1, 4)
    sink_b = sinks.reshape(N_KV_HEADS, GROUP).astype(jnp.float32)
    k_meta, v_meta = k[:, :, :N_META], v[:, :, :N_META]
    s_meta = jnp.einsum('bkgqd,bkmd->bkgqm', q[:, :, :, :N_META], k_meta)
    p_meta = sink_softmax(s_meta, jnp.tril(jnp.ones((N_META, N_META), dtype=bool)), sink_b[None, :, :, None, None])
    o_meta = jnp.einsum('bkgqm,bkmd->bkgqd', p_meta.astype(v.dtype), v_meta)
    q_blk = q[:, :, :, N_META:].reshape(B, N_KV_HEADS, GROUP, nB, WINDOW, HEAD_DIM)
    def band(t):
        t_meta = t[:, :, :N_META]
        t_blk = t[:, :, N_META:].reshape(B, N_KV_HEADS, nB, WINDOW, HEAD_DIM)
        prev = jnp.concatenate([jnp.zeros_like(t_blk[:, :, :1]), t_blk[:, :, :-1]], axis=2)
        meta = jnp.broadcast_to(t_meta[:, :, None], (B, N_KV_HEADS, nB, N_META, HEAD_DIM))
        return jnp.concatenate([meta, prev, t_blk], axis=3)
    k_band, v_band = band(k), band(v)
    s = jnp.einsum('bkgnqd,bknsd->bkgnqs', q_blk, k_band)
    i = jnp.arange(WINDOW)[:, None]
    j = jnp.arange(WINDOW)[None, :]
    blk = jnp.arange(nB)[:, None, None]
    mask = jnp.concatenate([
        jnp.ones((nB, WINDOW, N_META), dtype=bool),
        (j > i)[None] & (blk > 0),
        jnp.broadcast_to((j <= i)[None], (nB, WINDOW, WINDOW)),
    ], axis=-1)
    p = sink_softmax(s, mask, sink_b[None, :, :, None, None, None])
    o_real = jnp.einsum('bkgnqs,bknsd->bkgnqd', p.astype(v.dtype), v_band).reshape(B, N_KV_HEADS, GROUP, S, HEAD_DIM)
    o = jnp.concatenate([o_meta, o_real], axis=3).transpose(0, 3, 1, 2, 4).reshape(B, L, N_Q_HEADS * HEAD_DIM)
    return o @ w_out


def setup_inputs(seed: int = 0) -> dict:
    key = jax.random.key(seed)
    ks = jax.random.split(key, 16)
    nrm = lambda k, shape, fan_in: jax.random.normal(k, shape, jnp.float32) * (fan_in ** -0.5)
    return {
        "x": jax.random.normal(ks[0], (BATCH, SEQ, D_MODEL), jnp.float32),
        "meta_tokens": jax.random.normal(ks[1], (N_META, D_MODEL), jnp.float32),
        "norm_gains": 1.0 + 0.02 * jax.random.normal(ks[2], (DEPTH, 6, D_MODEL), jnp.float32),
        "w_ffn_in": nrm(ks[3], (DEPTH, 2, D_MODEL, 2 * D_FF), D_MODEL),
        "w_ffn_out": nrm(ks[4], (DEPTH, 2, D_FF, D_MODEL), D_FF),
        "gla_w_in": nrm(ks[5], (N_A_LAYERS, D_MODEL, GLA_IN), D_MODEL),
        "gla_w_gate": nrm(ks[6], (N_A_LAYERS, GLA_RANK, GLA_QK), GLA_RANK),
        "gla_b_gate": 0.1 * jax.random.normal(ks[7], (N_A_LAYERS, GLA_QK), jnp.float32),
        "gla_norm": 1.0 + 0.02 * jax.random.normal(ks[8], (N_A_LAYERS, GLA_DV), jnp.float32),
        "gla_w_out": nrm(ks[9], (N_A_LAYERS, GLA_V, D_MODEL), GLA_V),
        "kv_norm": 1.0 + 0.02 * jax.random.normal(ks[10], (D_MODEL,), jnp.float32),
        "w_kv": nrm(ks[11], (D_MODEL, 2 * N_KV_HEADS * HEAD_DIM), D_MODEL),
        "swa_w_q": nrm(ks[12], (N_B_LAYERS, D_MODEL, N_Q_HEADS * HEAD_DIM), D_MODEL),
        "swa_sinks": 0.5 * jax.random.normal(ks[13], (N_B_LAYERS, N_Q_HEADS), jnp.float32),
        "swa_w_out": nrm(ks[14], (N_B_LAYERS, N_Q_HEADS * HEAD_DIM, D_MODEL), N_Q_HEADS * HEAD_DIM),
    }


def reference(x, meta_tokens, norm_gains, w_ffn_in, w_ffn_out, gla_w_in, gla_w_gate, gla_b_gate, gla_norm,
              gla_w_out, kv_norm, w_kv, swa_w_q, swa_sinks, swa_w_out):
    B = x.shape[0]
    meta = jnp.broadcast_to(meta_tokens.astype(x.dtype)[None], (B, N_META, D_MODEL))
    h = jnp.concatenate([meta, x], axis=1)
    cos, sin = rope_tables(h.shape[1])
    k_sh, v_sh = None, None
    for layer in range(DEPTH):
        g = norm_gains[layer]
        h = h + 0.5 * rmsnorm(swiglu(rmsnorm(h, g[0]), w_ffn_in[layer, 0], w_ffn_out[layer, 0]), g[1])
        hn = rmsnorm(h, g[2])
        if layer < N_A_LAYERS:
            a = layer
            mix = gla_mixer(hn, gla_w_in[a], gla_w_gate[a], gla_b_gate[a], gla_norm[a], gla_w_out[a])
        else:
            b = layer - N_A_LAYERS
            mix = swa_mixer(hn, swa_w_q[b], swa_sinks[b], swa_w_out[b], k_sh, v_sh, cos, sin)
        h = h + rmsnorm(mix, g[3])
        h = h + 0.5 * rmsnorm(swiglu(rmsnorm(h, g[4]), w_ffn_in[layer, 1], w_ffn_out[layer, 1]), g[5])
        if layer == N_A_LAYERS - 1:
            k_sh, v_sh = shared_kv(h, kv_norm, w_kv, cos, sin)
    return h[:, N_META:]
```

```python
import functools

import jax
import jax.numpy as jnp
from jax import lax
from jax.experimental import pallas as pl
from jax.experimental.pallas import tpu as pltpu

F32 = jnp.float32
BF16 = jnp.bfloat16

D_MODEL = 1024
D_FF = 2816
N_META = 16
EPS = 1e-6
NEG_INF = -1e30

GLA_HEADS = 4
GLA_DK = 128
GLA_DV = 256
GLA_QK = GLA_HEADS * GLA_DK
GLA_V = GLA_HEADS * GLA_DV
GLA_RANK = 16
GLA_TAU = 16.0
GLA_CHUNK = 64
GLA_IN_PAD = 2 * GLA_QK + 2 * GLA_V + 128

N_Q_HEADS = 16
N_KV_HEADS = 4
HEAD_DIM = 64
WINDOW = 128
ROT_DIM = HEAD_DIM // 4
ROT_HALF = ROT_DIM // 2
ROPE_THETA = 500000.0
KV_DIM = N_KV_HEADS * HEAD_DIM
V_PAIR = N_KV_HEADS * 2 * HEAD_DIM

META_ROWS = 64
LANES = 128

_VMEM_LIMIT = 56 * 1024 * 1024

_FF_CHUNKS = ((0, 768), (768, 1536), (1536, 2304), (2304, 2816))

_NT = (((1,), (1,)), ((), ()))
_TN = (((0,), (0,)), ((), ()))


def _rmsnorm(x, g):
    return x * lax.rsqrt(jnp.mean(x * x, axis=-1, keepdims=True) + EPS) * g


def _sigmoid(x):
    return 1.0 / (1.0 + jnp.exp(-x))


def _dot(a, b):
    return jnp.dot(a, b, preferred_element_type=F32)


def _const_spec(shape):
    zeros = (0,) * len(shape)
    return pl.BlockSpec(shape, lambda *_: zeros, pipeline_mode=pl.Buffered(1))


def _ffn_kernel(*refs, with_kv):
    if with_kv:
        (h_ref, gpre_ref, gpost_ref, win_ref, wout_ref, gkv_ref, wkt_ref, wv_ref, cos_ref, sin_ref,
         o_ref, kt_ref, v2_ref, act_ref) = refs
    else:
        h_ref, gpre_ref, gpost_ref, win_ref, wout_ref, o_ref, act_ref = refs
    x = h_ref[...]
    xn = _rmsnorm(x, gpre_ref[...]).astype(BF16)
    for lo, hi in _FF_CHUNKS:
        gate = _dot(xn, win_ref[:, lo:hi])
        up = _dot(xn, win_ref[:, D_FF + lo:D_FF + hi])
        act_ref[:, lo:hi] = (gate * _sigmoid(gate) * up).astype(BF16)
    y = _dot(act_ref[...], wout_ref[...])
    h_new = x + 0.5 * _rmsnorm(y, gpost_ref[...])
    o_ref[...] = h_new
    if with_kv:
        kn = _rmsnorm(h_new, gkv_ref[...]).astype(BF16)
        kt = lax.dot_general(wkt_ref[...], kn, _NT, preferred_element_type=F32)
        c = cos_ref[...]
        s = sin_ref[...]
        parts = []
        for g in range(N_KV_HEADS):
            base = g * HEAD_DIM
            x1 = kt[base:base + ROT_HALF]
            x2 = kt[base + ROT_HALF:base + ROT_DIM]
            parts += [x1 * c - x2 * s, x2 * c + x1 * s, kt[base + ROT_DIM:base + HEAD_DIM]]
        kt_ref[0] = jnp.concatenate(parts, axis=0).astype(BF16)
        v_lo = _dot(kn, wv_ref[...])
        v_hi = pltpu.roll(v_lo, HEAD_DIM, axis=1)
        v2_ref[...] = jnp.concatenate([v_lo, v_hi], axis=1).astype(BF16)


def _ffn(h, g_pre, g_post, w_in, w_out, *, tm, kv=None, tiles_per_seq=1):
    t = h.shape[0]
    n = t // tm
    row_spec = pl.BlockSpec((tm, D_MODEL), lambda i: (i, 0))
    in_specs = [row_spec, _const_spec((1, D_MODEL)), _const_spec((1, D_MODEL)),
                _const_spec((D_MODEL, 2 * D_FF)), _const_spec((D_FF, D_MODEL))]
    args = [h, g_pre, g_post, w_in, w_out]
    out_shape = [jax.ShapeDtypeStruct((t, D_MODEL), F32)]
    out_specs = [row_spec]
    if kv is not None:
        g_kv, wkt, wv, cos_t, sin_t = kv
        in_specs += [_const_spec((1, D_MODEL)), _const_spec((KV_DIM, D_MODEL)), _const_spec((D_MODEL, V_PAIR)),
                     pl.BlockSpec((ROT_HALF, tm), lambda i: (0, i % tiles_per_seq)),
                     pl.BlockSpec((ROT_HALF, tm), lambda i: (0, i % tiles_per_seq))]
        args += [g_kv, wkt, wv, cos_t, sin_t]
        n_seq = n // tiles_per_seq
        out_shape += [jax.ShapeDtypeStruct((n_seq, KV_DIM, tiles_per_seq * tm), BF16),
                      jax.ShapeDtypeStruct((t, 2 * V_PAIR), BF16)]
        out_specs += [pl.BlockSpec((1, KV_DIM, tm), lambda i: (i // tiles_per_seq, 0, i % tiles_per_seq)),
                      pl.BlockSpec((tm, 2 * V_PAIR), lambda i: (i, 0))]
    out = pl.pallas_call(
        functools.partial(_ffn_kernel, with_kv=kv is not None),
        grid=(n,),
        in_specs=in_specs,
        out_specs=out_specs,
        out_shape=out_shape,
        scratch_shapes=[pltpu.VMEM((tm, D_FF), BF16)],
        compiler_params=pltpu.CompilerParams(dimension_semantics=("parallel",), vmem_limit_bytes=_VMEM_LIMIT),
        name="ffn_kv" if kv is not None else "ffn",
    )(*args)
    return out if kv is not None else out[0]


def _gla_kernel(*refs, rows, valid_rows, emit_state):
    if emit_state:
        (h_ref, s0_ref, gpre_ref, gpost_ref, win_ref, wgate_ref, bgate_ref, ghead_ref, wout_ref,
         o_ref, sfin_ref, st_ref, oscr_ref) = refs
    else:
        (h_ref, s0_ref, gpre_ref, gpost_ref, win_ref, wgate_ref, bgate_ref, ghead_ref, wout_ref,
         o_ref, st_ref, oscr_ref) = refs
    step = pl.program_id(1)

    @pl.when(step == 0)
    def _():
        st_ref[...] = s0_ref[...]

    x = h_ref[0]
    hn = _rmsnorm(x, gpre_ref[...]).astype(BF16)
    q = _dot(hn, win_ref[:, 0:GLA_QK])
    k = _dot(hn, win_ref[:, GLA_QK:2 * GLA_QK])
    v = _dot(hn, win_ref[:, 2 * GLA_QK:2 * GLA_QK + GLA_V])
    r = _dot(hn, win_ref[:, 2 * GLA_QK + GLA_V:2 * GLA_QK + 2 * GLA_V])
    lr = _dot(hn, win_ref[:, 2 * GLA_QK + 2 * GLA_V:GLA_IN_PAD])
    gp = _dot(lr.astype(BF16), wgate_ref[...]) + bgate_ref[...]
    gk = (jnp.minimum(gp, 0.0) - jnp.log1p(jnp.exp(-jnp.abs(gp)))) * (1.0 / GLA_TAU)
    if valid_rows is not None:
        live = lax.broadcasted_iota(jnp.int32, (rows, 1), 0) < valid_rows
        gk = jnp.where(live, gk, 0.0)
        k = jnp.where(live, k, 0.0)

    ri = lax.broadcasted_iota(jnp.int32, (rows, rows), 0)
    ci = lax.broadcasted_iota(jnp.int32, (rows, rows), 1)
    same = (ri // GLA_CHUNK) == (ci // GLA_CHUNK)
    tri = jnp.where(same & (ci <= ri), 1.0, 0.0).astype(BF16)
    blk = jnp.where(same, 1.0, 0.0).astype(BF16)
    g_hi = gk.astype(BF16)
    rem = gk - g_hi.astype(F32)
    g_mid = rem.astype(BF16)
    g_lo = (rem - g_mid.astype(F32)).astype(BF16)
    bcum = _dot(tri, g_hi) + _dot(tri, g_mid) + _dot(tri, g_lo)
    btot = _dot(blk, g_hi) + _dot(blk, g_mid) + _dot(blk, g_lo)

    q_dec = (q * (GLA_DK ** -0.5) * jnp.exp(bcum)).astype(BF16)
    k_dec = (k * jnp.exp(-bcum)).astype(BF16)
    k_rem = (k * jnp.exp(btot - bcum)).astype(BF16)
    decay = jnp.exp(btot)
    vb = v.astype(BF16)
    causal = (lax.broadcasted_iota(jnp.int32, (GLA_CHUNK, GLA_CHUNK), 1)
              <= lax.broadcasted_iota(jnp.int32, (GLA_CHUNK, GLA_CHUNK), 0))

    for c in range(rows // GLA_CHUNK):
        r0 = c * GLA_CHUNK
        for hd in range(GLA_HEADS):
            k0 = hd * GLA_DK
            v0 = hd * GLA_DV
            qd = q_dec[r0:r0 + GLA_CHUNK, k0:k0 + GLA_DK]
            kd = k_dec[r0:r0 + GLA_CHUNK, k0:k0 + GLA_DK]
            kr = k_rem[r0:r0 + GLA_CHUNK, k0:k0 + GLA_DK]
            vc = vb[r0:r0 + GLA_CHUNK, v0:v0 + GLA_DV]
            att = lax.dot_general(qd, kd, _NT, preferred_element_type=F32)
            att = jnp.where(causal, att, 0.0).astype(BF16)
            st_t = st_ref[hd]
            o = _dot(att, vc) + lax.dot_general(qd, st_t.astype(BF16), _NT, preferred_element_type=F32)
            oscr_ref[r0:r0 + GLA_CHUNK, v0:v0 + GLA_DV] = o
            st_ref[hd] = (st_t * decay[r0:r0 + 1, k0:k0 + GLA_DK]
                          + lax.dot_general(vc, kr, _TN, preferred_element_type=F32))

    gated = []
    g_head = ghead_ref[...]
    for hd in range(GLA_HEADS):
        v0 = hd * GLA_DV
        rg = r[:, v0:v0 + GLA_DV]
        gated.append(_rmsnorm(oscr_ref[:, v0:v0 + GLA_DV], g_head) * (rg * _sigmoid(rg)))
    mix = _dot(jnp.concatenate(gated, axis=1).astype(BF16), wout_ref[...])
    o_ref[0] = x + _rmsnorm(mix, gpost_ref[...])

    if emit_state:
        @pl.when(step == pl.num_programs(1) - 1)
        def _():
            sfin_ref[0] = st_ref[...]


def _gla(h, s0, g_pre, g_post, w_in, w_gate, b_gate, g_head, w_out, *, rows, valid_rows=None, emit_state=False):
    b, length, _ = h.shape
    steps = length // rows
    seq_spec = pl.BlockSpec((1, rows, D_MODEL), lambda i, j: (i, j, 0))
    state_shape = (GLA_HEADS, GLA_DV, GLA_DK)
    in_specs = [seq_spec, _const_spec(state_shape), _const_spec((1, D_MODEL)), _const_spec((1, D_MODEL)),
                _const_spec((D_MODEL, GLA_IN_PAD)), _const_spec((LANES, GLA_QK)), _const_spec((1, GLA_QK)),
                _const_spec((1, GLA_DV)), _const_spec((GLA_V, D_MODEL))]
    out_shape = [jax.ShapeDtypeStruct(h.shape, F32)]
    out_specs = [seq_spec]
    if emit_state:
        out_shape.append(jax.ShapeDtypeStruct((b,) + state_shape, F32))
        out_specs.append(pl.BlockSpec((1,) + state_shape, lambda i, j: (i, 0, 0, 0)))
    return pl.pallas_call(
        functools.partial(_gla_kernel, rows=rows, valid_rows=valid_rows, emit_state=emit_state),
        grid=(b, steps),
        in_specs=in_specs,
        out_specs=out_specs,
        out_shape=out_shape,
        scratch_shapes=[pltpu.VMEM(state_shape, F32), pltpu.VMEM((rows, GLA_V), F32)],
        compiler_params=pltpu.CompilerParams(dimension_semantics=("parallel", "arbitrary"),
                                             vmem_limit_bytes=_VMEM_LIMIT),
        name="gla_meta" if emit_state else "gla",
    )(h, s0, g_pre, g_post, w_in, w_gate, b_gate, g_head, w_out)


def _swa_kernel(sinks_ref, h_ref, gpre_ref, gpost_ref, wq_ref, wo_ref, rc_ref, ra_ref, rb_ref,
                ktc_ref, ktp_ref, vc_ref, vp_ref, ktm_ref, vm_ref, o_ref, att_ref, *, rows):
    step = pl.program_id(1)
    x = h_ref[0]
    hn = _rmsnorm(x, gpre_ref[...]).astype(BF16)
    q = _dot(hn, wq_ref[...])
    rc = rc_ref[...]
    ra = ra_ref[...]
    rb = rb_ref[...]
    cols = []
    for g in range(D_MODEL // LANES):
        qg = q[:, g * LANES:(g + 1) * LANES]
        cols.append(qg * rc + pltpu.roll(qg, LANES - ROT_HALF, axis=1) * ra + pltpu.roll(qg, ROT_HALF, axis=1) * rb)
    qb = jnp.concatenate(cols, axis=1).astype(BF16)

    n_keys = 3 * WINDOW
    qi = lax.broadcasted_iota(jnp.int32, (WINDOW, n_keys), 0)
    kj = lax.broadcasted_iota(jnp.int32, (WINDOW, n_keys), 1)
    seg = kj // WINDOW
    off = kj - seg * WINDOW
    low_lanes = lax.broadcasted_iota(jnp.int32, (WINDOW, LANES), 1) < HEAD_DIM
    zeros_kt = jnp.zeros((HEAD_DIM, n_keys), BF16)

    for blk in range(rows // WINDOW):
        r0 = blk * WINDOW
        if blk == 0:
            kt_prev = ktp_ref[0]
            v_prev = vp_ref[0]
            prev_floor = qi + jnp.where(step > 0, 0, WINDOW)
        else:
            kt_prev = ktc_ref[0, :, r0 - WINDOW:r0]
            v_prev = vc_ref[0, r0 - WINDOW:r0, :]
            prev_floor = qi
        kt_cur = ktc_ref[0, :, r0:r0 + WINDOW]
        v_cur = vc_ref[0, r0:r0 + WINDOW, :]
        mask = ((seg == 0) & (off < N_META)) | ((seg == 1) & (off > prev_floor)) | ((seg == 2) & (off <= qi))
        for g in range(N_KV_HEADS):
            d0 = g * HEAD_DIM
            kt = jnp.concatenate([ktm_ref[d0:d0 + HEAD_DIM, :], kt_prev[d0:d0 + HEAD_DIM, :],
                                  kt_cur[d0:d0 + HEAD_DIM, :]], axis=1)
            kab = jnp.concatenate([jnp.concatenate([kt, zeros_kt], axis=0),
                                   jnp.concatenate([zeros_kt, kt], axis=0)], axis=1)
            l0 = g * LANES
            vv = jnp.concatenate([vm_ref[:, l0:l0 + LANES], v_prev[:, l0:l0 + LANES], v_cur[:, l0:l0 + LANES],
                                  vm_ref[:, V_PAIR + l0:V_PAIR + l0 + LANES],
                                  v_prev[:, V_PAIR + l0:V_PAIR + l0 + LANES],
                                  v_cur[:, V_PAIR + l0:V_PAIR + l0 + LANES]], axis=0)
            for pair in range(2):
                c0 = (2 * g + pair) * LANES
                head_a = 4 * g + 2 * pair
                s = _dot(qb[r0:r0 + WINDOW, c0:c0 + LANES], kab)
                probs = []
                inv = []
                for idx in range(2):
                    sink = sinks_ref[head_a + idx]
                    sh = jnp.where(mask, s[:, idx * n_keys:(idx + 1) * n_keys], NEG_INF)
                    m = jnp.maximum(jnp.max(sh, axis=-1, keepdims=True), sink)
                    p = jnp.exp(sh - m)
                    inv.append(1.0 / (jnp.sum(p, axis=-1, keepdims=True) + jnp.exp(sink - m)))
                    probs.append(p)
                o2 = _dot(jnp.concatenate(probs, axis=1).astype(BF16), vv)
                att_ref[r0:r0 + WINDOW, c0:c0 + LANES] = o2 * jnp.where(low_lanes, inv[0], inv[1])

    mix = _dot(att_ref[...].astype(BF16), wo_ref[...])
    o_ref[0] = x + _rmsnorm(mix, gpost_ref[...])


def _swa(h, sinks, g_pre, g_post, w_q, w_o, rope_c, rope_a, rope_b, kt, v2, kt_meta, v2_meta, *, rows):
    b, length, _ = h.shape
    steps = length // rows
    per = rows // WINDOW
    seq_spec = pl.BlockSpec((1, rows, D_MODEL), lambda i, j: (i, j, 0))
    rope_spec = pl.BlockSpec((rows, LANES), lambda i, j: (j, 0))
    prev_blk = lambda j: jnp.maximum(j * per - 1, 0)
    in_specs = [
        pl.BlockSpec(memory_space=pltpu.SMEM),
        seq_spec, _const_spec((1, D_MODEL)), _const_spec((1, D_MODEL)),
        _const_spec((D_MODEL, D_MODEL)), _const_spec((D_MODEL, D_MODEL)),
        rope_spec, rope_spec, rope_spec,
        pl.BlockSpec((1, KV_DIM, rows), lambda i, j: (i, 0, j)),
        pl.BlockSpec((1, KV_DIM, WINDOW), lambda i, j: (i, 0, prev_blk(j))),
        pl.BlockSpec((1, rows, 2 * V_PAIR), lambda i, j: (i, j, 0)),
        pl.BlockSpec((1, WINDOW, 2 * V_PAIR), lambda i, j: (i, prev_blk(j), 0)),
        _const_spec((KV_DIM, WINDOW)), _const_spec((WINDOW, 2 * V_PAIR)),
    ]
    return pl.pallas_call(
        functools.partial(_swa_kernel, rows=rows),
        grid=(b, steps),
        in_specs=in_specs,
        out_specs=seq_spec,
        out_shape=jax.ShapeDtypeStruct(h.shape, F32),
        scratch_shapes=[pltpu.VMEM((rows, D_MODEL), F32)],
        compiler_params=pltpu.CompilerParams(dimension_semantics=("parallel", "parallel"),
                                             vmem_limit_bytes=_VMEM_LIMIT),
        name="swa",
    )(sinks, h, g_pre, g_post, w_q, w_o, rope_c, rope_a, rope_b, kt, kt, v2, v2, kt_meta, v2_meta)


def _rope_angles(positions):
    inv_freq = ROPE_THETA ** (-jnp.arange(0, ROT_DIM, 2, dtype=F32) / ROT_DIM)
    ang = positions.astype(F32)[:, None] * inv_freq[None, :]
    return jnp.cos(ang), jnp.sin(ang)


def _q_rope_tables(cos, sin):
    n = cos.shape[0]
    scale = HEAD_DIM ** -0.5
    ones = jnp.ones((n, HEAD_DIM - ROT_DIM), F32)
    zeros_h = jnp.zeros((n, HEAD_DIM - ROT_DIM), F32)
    zeros_r = jnp.zeros((n, ROT_HALF), F32)
    rc = jnp.concatenate([cos, cos, ones], axis=1) * scale
    ra = jnp.concatenate([-sin, zeros_r, zeros_h], axis=1) * scale
    rb = jnp.concatenate([zeros_r, sin, zeros_h], axis=1) * scale
    return tuple(jnp.tile(t, (1, LANES // HEAD_DIM)) for t in (rc, ra, rb))


def kernel(x, meta_tokens, norm_gains, w_ffn_in, w_ffn_out, gla_w_in, gla_w_gate, gla_b_gate, gla_norm, gla_w_out,
           kv_norm, w_kv, swa_w_q, swa_sinks, swa_w_out):
    batch, seq, _ = x.shape
    depth = norm_gains.shape[0]
    n_a = gla_w_in.shape[0]
    tm = 512
    gla_rows = 256
    swa_rows = 256

    gains = norm_gains.reshape(depth, 6, 1, D_MODEL)
    w_in_b = w_ffn_in.astype(BF16)
    w_out_b = w_ffn_out.astype(BF16)

    h = x.reshape(batch * seq, D_MODEL)
    hm = jnp.pad(meta_tokens.astype(x.dtype), ((0, META_ROWS - N_META), (0, 0)))

    cos_m, sin_m = _rope_angles(jnp.arange(META_ROWS))
    cos_r, sin_r = _rope_angles(jnp.arange(N_META, N_META + seq))
    wkt = w_kv[:, :KV_DIM].T.astype(BF16)
    wv = jnp.pad(w_kv[:, KV_DIM:].reshape(D_MODEL, N_KV_HEADS, HEAD_DIM),
                 ((0, 0), (0, 0), (0, HEAD_DIM))).reshape(D_MODEL, V_PAIR).astype(BF16)
    kv_main = (kv_norm.reshape(1, D_MODEL), wkt, wv, cos_r.T, sin_r.T)
    kv_meta = (kv_norm.reshape(1, D_MODEL), wkt, wv, cos_m.T, sin_m.T)

    kt = v2 = kt_meta = v2_meta = None
    for layer in range(depth):
        g = gains[layer]
        last = layer == depth - 1
        h = _ffn(h, g[0], g[1], w_in_b[layer, 0], w_out_b[layer, 0], tm=tm)
        if layer < n_a:
            a = layer
            hm = _ffn(hm, g[0], g[1], w_in_b[layer, 0], w_out_b[layer, 0], tm=META_ROWS)
            w_in_a = jnp.pad(gla_w_in[a], ((0, 0), (0, GLA_IN_PAD - gla_w_in.shape[2]))).astype(BF16)
            w_gate_a = jnp.pad(gla_w_gate[a], ((0, LANES - GLA_RANK), (0, 0))).astype(BF16)
            gla_args = (g[2], g[3], w_in_a, w_gate_a, gla_b_gate[a].reshape(1, GLA_QK),
                        gla_norm[a].reshape(1, GLA_DV), gla_w_out[a].astype(BF16))
            s_zero = jnp.zeros((GLA_HEADS, GLA_DV, GLA_DK), F32)
            hm3, s_meta = _gla(hm[None], s_zero, *gla_args, rows=META_ROWS, valid_rows=N_META, emit_state=True)
            hm = hm3[0]
            h = _gla(h.reshape(batch, seq, D_MODEL), s_meta[0], *gla_args, rows=gla_rows)[0]
            h = h.reshape(batch * seq, D_MODEL)
        else:
            b = layer - n_a
            rope_c, rope_a, rope_b = _q_rope_tables(cos_r, sin_r)
            h = _swa(h.reshape(batch, seq, D_MODEL), swa_sinks[b], g[2], g[3], swa_w_q[b].astype(BF16),
                     swa_w_out[b].astype(BF16), rope_c, rope_a, rope_b, kt, v2.reshape(batch, seq, 2 * V_PAIR),
                     kt_meta, v2_meta, rows=swa_rows)
            h = h.reshape(batch * seq, D_MODEL)
        if layer == n_a - 1:
            h, kt, v2 = _ffn(h, g[4], g[5], w_in_b[layer, 1], w_out_b[layer, 1], tm=tm, kv=kv_main,
                             tiles_per_seq=seq // tm)
            _, kt_m, v2_m = _ffn(hm, g[4], g[5], w_in_b[layer, 1], w_out_b[layer, 1], tm=META_ROWS, kv=kv_meta)
            kt_meta = jnp.pad(kt_m[0], ((0, 0), (0, WINDOW - META_ROWS)))
            v2_meta = jnp.pad(v2_m, ((0, WINDOW - META_ROWS), (0, 0)))
        else:
            h = _ffn(h, g[4], g[5], w_in_b[layer, 1], w_out_b[layer, 1], tm=tm)
            if not last and layer < n_a:
                hm = _ffn(hm, g[4], g[5], w_in_b[layer, 1], w_out_b[layer, 1], tm=META_ROWS)
    return h.reshape(batch, seq, D_MODEL)
```

```python
import functools

import jax
import jax.numpy as jnp
from jax import lax
from jax.experimental import pallas as pl
from jax.experimental.pallas import tpu as pltpu

F32 = jnp.float32
BF16 = jnp.bfloat16

D_MODEL = 1024
D_FF = 2816
N_META = 16
EPS = 1e-6
NEG_INF = -1e30

GLA_HEADS = 4
GLA_DK = 128
GLA_DV = 256
GLA_QK = GLA_HEADS * GLA_DK
GLA_V = GLA_HEADS * GLA_DV
GLA_RANK = 16
GLA_TAU = 16.0
GLA_CHUNK = 64
GLA_IN_PAD = 2 * GLA_QK + 2 * GLA_V + 128

N_Q_HEADS = 16
N_KV_HEADS = 4
GROUP = N_Q_HEADS // N_KV_HEADS
HEAD_DIM = 64
WINDOW = 128
ROT_DIM = HEAD_DIM // 4
ROT_HALF = ROT_DIM // 2
ROPE_THETA = 500000.0
KV_DIM = N_KV_HEADS * HEAD_DIM
LANES = 128
K_PAD = N_KV_HEADS * LANES

META_ROWS = 64

_VMEM_LIMIT = 56 * 1024 * 1024

_FF_CHUNKS = ((0, 768), (768, 1536), (1536, 2304), (2304, 2816))

_NT = (((1,), (1,)), ((), ()))
_TN = (((0,), (0,)), ((), ()))


def _rmsnorm(x, g):
    return x * lax.rsqrt(jnp.mean(x * x, axis=-1, keepdims=True) + EPS) * g


def _sigmoid(x):
    return 1.0 / (1.0 + jnp.exp(-x))


def _mm(a, b, dims=None):
    precision = lax.Precision.HIGHEST if a.dtype == F32 else None
    if dims is None:
        return jnp.dot(a, b, preferred_element_type=F32, precision=precision)
    return lax.dot_general(a, b, dims, preferred_element_type=F32, precision=precision)


def _const_spec(shape):
    zeros = (0,) * len(shape)
    return pl.BlockSpec(shape, lambda *_: zeros, pipeline_mode=pl.Buffered(1))


def _rope_rows(xt, c, s, n_heads):
    parts = []
    for h in range(n_heads):
        base = h * HEAD_DIM
        x1 = xt[base:base + ROT_HALF]
        x2 = xt[base + ROT_HALF:base + ROT_DIM]
        parts += [x1 * c - x2 * s, x2 * c + x1 * s, xt[base + ROT_DIM:base + HEAD_DIM]]
    return jnp.concatenate(parts, axis=0)


def _ffn_kernel(*refs, with_kv):
    if with_kv:
        (h_ref, gpre_ref, gpost_ref, win_ref, wout_ref, gkv_ref, wk_ref, wvt_ref, rc_ref, ra_ref, rb_ref,
         o_ref, k_ref, vt_ref, act_ref) = refs
    else:
        h_ref, gpre_ref, gpost_ref, win_ref, wout_ref, o_ref, act_ref = refs
    mxu = win_ref.dtype
    x = h_ref[...]
    xn = _rmsnorm(x, gpre_ref[...]).astype(mxu)
    for lo, hi in _FF_CHUNKS:
        gate = _mm(xn, win_ref[:, lo:hi])
        up = _mm(xn, win_ref[:, D_FF + lo:D_FF + hi])
        act_ref[:, lo:hi] = (gate * _sigmoid(gate) * up).astype(mxu)
    y = _mm(act_ref[...], wout_ref[...])
    h_new = x + 0.5 * _rmsnorm(y, gpost_ref[...])
    o_ref[...] = h_new
    if with_kv:
        kn = _rmsnorm(h_new, gkv_ref[...]).astype(mxu)
        k = _mm(kn, wk_ref[...])
        rc = rc_ref[...]
        ra = ra_ref[...]
        rb = rb_ref[...]
        cols = []
        for g in range(N_KV_HEADS):
            kg = k[:, g * LANES:(g + 1) * LANES]
            cols.append(kg * rc + pltpu.roll(kg, LANES - ROT_HALF, axis=1) * ra
                        + pltpu.roll(kg, ROT_HALF, axis=1) * rb)
        k_ref[...] = jnp.concatenate(cols, axis=1).astype(k_ref.dtype)
        vt_ref[0] = _mm(wvt_ref[...], kn, _NT).astype(vt_ref.dtype)


def _ffn(h, g_pre, g_post, w_in, w_out, sel, *, tm, kv=None, tiles_per_seq=1):
    t = h.shape[0]
    n = t // tm
    row_spec = pl.BlockSpec((tm, D_MODEL), lambda i: (i, 0))
    w_idx = lambda i: (sel[0], sel[1], 0, 0)
    in_specs = [row_spec, _const_spec((1, D_MODEL)), _const_spec((1, D_MODEL)),
                pl.BlockSpec((None, None, D_MODEL, 2 * D_FF), w_idx, pipeline_mode=pl.Buffered(1)),
                pl.BlockSpec((None, None, D_FF, D_MODEL), w_idx, pipeline_mode=pl.Buffered(1))]
    args = [h, g_pre, g_post, w_in, w_out]
    out_shape = [jax.ShapeDtypeStruct((t, D_MODEL), F32)]
    out_specs = [row_spec]
    if kv is not None:
        g_kv, wk, wvt, rc, ra, rb = kv
        rope_spec = pl.BlockSpec((tm, LANES), lambda i: (i % tiles_per_seq, 0))
        in_specs += [_const_spec((1, D_MODEL)), _const_spec((D_MODEL, K_PAD)), _const_spec((KV_DIM, D_MODEL)),
                     rope_spec, rope_spec, rope_spec]
        args += [g_kv, wk, wvt, rc, ra, rb]
        n_seq = n // tiles_per_seq
        out_shape += [jax.ShapeDtypeStruct((t, K_PAD), BF16),
                      jax.ShapeDtypeStruct((n_seq, KV_DIM, tiles_per_seq * tm), BF16)]
        out_specs += [pl.BlockSpec((tm, K_PAD), lambda i: (i, 0)),
                      pl.BlockSpec((1, KV_DIM, tm), lambda i: (i // tiles_per_seq, 0, i % tiles_per_seq))]
    out = pl.pallas_call(
        functools.partial(_ffn_kernel, with_kv=kv is not None),
        grid=(n,),
        in_specs=in_specs,
        out_specs=out_specs,
        out_shape=out_shape,
        scratch_shapes=[pltpu.VMEM((tm, D_FF), w_in.dtype)],
        compiler_params=pltpu.CompilerParams(dimension_semantics=("parallel",), vmem_limit_bytes=_VMEM_LIMIT),
        name="ffn_kv" if kv is not None else "ffn",
    )(*args)
    return out if kv is not None else out[0]


def _gla_kernel(*refs, rows, valid_rows, emit_state):
    if emit_state:
        (h_ref, s0_ref, gpre_ref, gpost_ref, win_ref, wgate_ref, bgate_ref, ghead_ref, wout_ref,
         o_ref, sfin_ref, st_ref, oscr_ref) = refs
    else:
        (h_ref, s0_ref, gpre_ref, gpost_ref, win_ref, wgate_ref, bgate_ref, ghead_ref, wout_ref,
         o_ref, st_ref, oscr_ref) = refs
    step = pl.program_id(1)
    n_chunks = rows // GLA_CHUNK

    @pl.when(step == 0)
    def _():
        st_ref[...] = s0_ref[...]

    mxu = win_ref.dtype
    x = h_ref[0]
    hn = _rmsnorm(x, gpre_ref[...]).astype(mxu)
    lr = _mm(hn, win_ref[:, 2 * GLA_QK + 2 * GLA_V:GLA_IN_PAD])
    gp = _mm(lr.astype(mxu), wgate_ref[...]) + bgate_ref[...]
    gk = (jnp.minimum(gp, 0.0) - jnp.log1p(jnp.exp(-jnp.abs(gp)))) * (1.0 / GLA_TAU)
    q = _mm(hn, win_ref[:, 0:GLA_QK])
    k = _mm(hn, win_ref[:, GLA_QK:2 * GLA_QK])
    v = _mm(hn, win_ref[:, 2 * GLA_QK:2 * GLA_QK + GLA_V])
    if valid_rows is not None:
        live = lax.broadcasted_iota(jnp.int32, (rows, 1), 0) < valid_rows
        gk = jnp.where(live, gk, 0.0)
        k = jnp.where(live, k, 0.0)

    ri = lax.broadcasted_iota(jnp.int32, (rows, rows), 0)
    ci = lax.broadcasted_iota(jnp.int32, (rows, rows), 1)
    intra = ((ri // GLA_CHUNK) == (ci // GLA_CHUNK)) & (ci <= ri)
    tri = jnp.where(intra, 1.0, 0.0).astype(mxu)
    g_hi = gk.astype(mxu)
    rem = gk - g_hi.astype(F32)
    g_mid = rem.astype(mxu)
    g_lo = (rem - g_mid.astype(F32)).astype(mxu)
    parts = _mm(tri, jnp.concatenate([g_hi, g_mid, g_lo], axis=1))
    bcum = parts[:, 0:GLA_QK] + parts[:, GLA_QK:2 * GLA_QK] + parts[:, 2 * GLA_QK:3 * GLA_QK]
    btot = jnp.concatenate(
        [jnp.broadcast_to(bcum[(c + 1) * GLA_CHUNK - 1:(c + 1) * GLA_CHUNK], (GLA_CHUNK, GLA_QK))
         for c in range(n_chunks)], axis=0)

    q_dec = (q * (GLA_DK ** -0.5) * jnp.exp(bcum)).astype(mxu)
    k_dec = (k * jnp.exp(-bcum)).astype(mxu)
    k_rem = (k * jnp.exp(btot - bcum)).astype(mxu)
    decay = jnp.exp(btot)
    vb = v.astype(mxu)

    for hd in range(GLA_HEADS):
        k0 = hd * GLA_DK
        v0 = hd * GLA_DV
        qd = q_dec[:, k0:k0 + GLA_DK]
        att = _mm(qd, k_dec[:, k0:k0 + GLA_DK], _NT)
        o_intra = _mm(jnp.where(intra, att, 0.0).astype(mxu), vb[:, v0:v0 + GLA_DV])
        st_t = st_ref[hd]
        for c in range(n_chunks):
            r0 = c * GLA_CHUNK
            qc = qd[r0:r0 + GLA_CHUNK]
            oscr_ref[r0:r0 + GLA_CHUNK, v0:v0 + GLA_DV] = (
                o_intra[r0:r0 + GLA_CHUNK]
                + _mm(qc, st_t.astype(mxu), _NT))
            st_t = (st_t * decay[r0:r0 + 1, k0:k0 + GLA_DK]
                    + _mm(vb[r0:r0 + GLA_CHUNK, v0:v0 + GLA_DV], k_rem[r0:r0 + GLA_CHUNK, k0:k0 + GLA_DK], _TN))
        st_ref[hd] = st_t

    r = _mm(hn, win_ref[:, 2 * GLA_QK + GLA_V:2 * GLA_QK + 2 * GLA_V])
    gated = []
    g_head = ghead_ref[...]
    for hd in range(GLA_HEADS):
        v0 = hd * GLA_DV
        rg = r[:, v0:v0 + GLA_DV]
        gated.append(_rmsnorm(oscr_ref[:, v0:v0 + GLA_DV], g_head) * (rg * _sigmoid(rg)))
    mix = _mm(jnp.concatenate(gated, axis=1).astype(mxu), wout_ref[...])
    o_ref[0] = x + _rmsnorm(mix, gpost_ref[...])

    if emit_state:
        @pl.when(step == pl.num_programs(1) - 1)
        def _():
            sfin_ref[0] = st_ref[...]


def _gla(h, s0, g_pre, g_post, w_in, w_gate, b_gate, g_head, w_out, *, rows, valid_rows=None, emit_state=False):
    b, length, _ = h.shape
    steps = length // rows
    seq_spec = pl.BlockSpec((1, rows, D_MODEL), lambda i, j: (i, j, 0))
    state_shape = (GLA_HEADS, GLA_DV, GLA_DK)
    in_specs = [seq_spec, _const_spec(state_shape), _const_spec((1, D_MODEL)), _const_spec((1, D_MODEL)),
                _const_spec((D_MODEL, GLA_IN_PAD)), _const_spec((LANES, GLA_QK)), _const_spec((1, GLA_QK)),
                _const_spec((1, GLA_DV)), _const_spec((GLA_V, D_MODEL))]
    out_shape = [jax.ShapeDtypeStruct(h.shape, F32)]
    out_specs = [seq_spec]
    if emit_state:
        out_shape.append(jax.ShapeDtypeStruct((b,) + state_shape, F32))
        out_specs.append(pl.BlockSpec((1,) + state_shape, lambda i, j: (i, 0, 0, 0)))
    return pl.pallas_call(
        functools.partial(_gla_kernel, rows=rows, valid_rows=valid_rows, emit_state=emit_state),
        grid=(b, steps),
        in_specs=in_specs,
        out_specs=out_specs,
        out_shape=out_shape,
        scratch_shapes=[pltpu.VMEM(state_shape, F32), pltpu.VMEM((rows, GLA_V), F32)],
        compiler_params=pltpu.CompilerParams(dimension_semantics=("parallel", "arbitrary"),
                                             vmem_limit_bytes=_VMEM_LIMIT),
        name="gla_meta" if emit_state else "gla",
    )(h, s0, g_pre, g_post, w_in, w_gate, b_gate, g_head, w_out)


def _swa_kernel(sinks_ref, h_ref, gpre_ref, gpost_ref, wqt_ref, wo_ref, cos_ref, sin_ref,
                kc_ref, kp_ref, vtc_ref, vtp_ref, km_ref, vtm_ref, o_ref, ot_ref, st_ref, pt_ref, *, rows):
    step = pl.program_id(1)
    x = h_ref[0]
    hn = _rmsnorm(x, gpre_ref[...]).astype(BF16)
    qt = _mm(wqt_ref[...], hn, _NT)
    qtb = _rope_rows(qt, cos_ref[...], sin_ref[...], N_Q_HEADS).astype(BF16)

    key_j = lax.broadcasted_iota(jnp.int32, (WINDOW, WINDOW), 0)
    qry_i = lax.broadcasted_iota(jnp.int32, (WINDOW, WINDOW), 1)
    from_prev = key_j > qry_i
    first_prev_ok = key_j > qry_i + jnp.where(step > 0, 0, WINDOW)
    units = [(blk, g) for blk in range(rows // WINDOW) for g in range(N_KV_HEADS)]

    for u, (blk, g) in enumerate(units):
        c0 = blk * WINDOW
        l0 = g * LANES
        k_prev = kp_ref[0, :, l0:l0 + HEAD_DIM] if blk == 0 else kc_ref[0, c0 - WINDOW:c0, l0:l0 + HEAD_DIM]
        kcat = jnp.concatenate([km_ref[:, l0:l0 + HEAD_DIM], k_prev,
                                kc_ref[0, c0:c0 + WINDOW, l0:l0 + HEAD_DIM]], axis=0)
        qg = jnp.concatenate([qtb[(GROUP * g + i) * HEAD_DIM:(GROUP * g + i + 1) * HEAD_DIM, c0:c0 + WINDOW]
                              for i in range(GROUP)], axis=1)
        st_ref[u] = _mm(kcat, qg)

    for u, (blk, g) in enumerate(units):
        c0 = blk * WINDOW
        d0 = g * HEAD_DIM
        inv = []
        for i in range(GROUP):
            sink = sinks_ref[GROUP * g + i]
            q0 = i * WINDOW
            s_meta = st_ref[u, 0:N_META, q0:q0 + WINDOW]
            s_prev = st_ref[u, N_META:N_META + WINDOW, q0:q0 + WINDOW]
            s_cur = st_ref[u, N_META + WINDOW:N_META + 2 * WINDOW, q0:q0 + WINDOW]
            if blk == 0:
                s_prev = jnp.where(first_prev_ok, s_prev, NEG_INF)
            s_band = jnp.where(from_prev, s_prev, s_cur)
            m = jnp.maximum(jnp.max(s_band, axis=0, keepdims=True), jnp.max(s_meta, axis=0, keepdims=True))
            m = jnp.maximum(m, sink)
            p_band = jnp.exp(s_band - m)
            p_meta = jnp.exp(s_meta - m)
            den = (jnp.sum(p_band, axis=0, keepdims=True) + jnp.sum(p_meta, axis=0, keepdims=True)
                   + jnp.exp(sink - m))
            inv.append(1.0 / den)
            pt_ref[u, :, q0:q0 + WINDOW] = jnp.concatenate(
                [p_meta, jnp.where(from_prev, p_band, 0.0), jnp.where(from_prev, 0.0, p_band)],
                axis=0).astype(BF16)
        vt_prev = vtp_ref[0, d0:d0 + HEAD_DIM, :] if blk == 0 else vtc_ref[0, d0:d0 + HEAD_DIM, c0 - WINDOW:c0]
        vt_band = jnp.concatenate([vt_prev, vtc_ref[0, d0:d0 + HEAD_DIM, c0:c0 + WINDOW]], axis=1)
        og = (_mm(vt_band, pt_ref[u, N_META:, :])
              + _mm(vtm_ref[d0:d0 + HEAD_DIM, :], pt_ref[u, 0:N_META, :]))
        for i in range(GROUP):
            h0 = (GROUP * g + i) * HEAD_DIM
            ot_ref[h0:h0 + HEAD_DIM, c0:c0 + WINDOW] = og[:, i * WINDOW:(i + 1) * WINDOW] * inv[i]

    mix = _mm(ot_ref[...].astype(BF16), wo_ref[...], _TN)
    o_ref[0] = x + _rmsnorm(mix, gpost_ref[...])


def _swa(h, sinks, g_pre, g_post, wqt, w_o, cos_t, sin_t, k_pad, vt, k_meta, vt_meta, *, rows):
    b, length, _ = h.shape
    steps = length // rows
    per = rows // WINDOW
    seq_spec = pl.BlockSpec((1, rows, D_MODEL), lambda i, j: (i, j, 0))
    rope_spec = pl.BlockSpec((ROT_HALF, rows), lambda i, j: (0, j))
    prev_blk = lambda j: jnp.maximum(j * per - 1, 0)
    in_specs = [
        pl.BlockSpec(memory_space=pltpu.SMEM),
        seq_spec, _const_spec((1, D_MODEL)), _const_spec((1, D_MODEL)),
        _const_spec((D_MODEL, D_MODEL)), _const_spec((D_MODEL, D_MODEL)),
        rope_spec, rope_spec,
        pl.BlockSpec((1, rows, K_PAD), lambda i, j: (i, j, 0)),
        pl.BlockSpec((1, WINDOW, K_PAD), lambda i, j: (i, prev_blk(j), 0)),
        pl.BlockSpec((1, KV_DIM, rows), lambda i, j: (i, 0, j)),
        pl.BlockSpec((1, KV_DIM, WINDOW), lambda i, j: (i, 0, prev_blk(j))),
        _const_spec((N_META, K_PAD)), _const_spec((KV_DIM, N_META)),
    ]
    return pl.pallas_call(
        functools.partial(_swa_kernel, rows=rows),
        grid=(b, steps),
        in_specs=in_specs,
        out_specs=seq_spec,
        out_shape=jax.ShapeDtypeStruct(h.shape, F32),
        scratch_shapes=[pltpu.VMEM((D_MODEL, rows), F32),
                        pltpu.VMEM((per * N_KV_HEADS, N_META + 2 * WINDOW, GROUP * WINDOW), F32),
                        pltpu.VMEM((per * N_KV_HEADS, N_META + 2 * WINDOW, GROUP * WINDOW), BF16)],
        compiler_params=pltpu.CompilerParams(dimension_semantics=("parallel", "parallel"),
                                             vmem_limit_bytes=_VMEM_LIMIT),
        name="swa",
    )(sinks, h, g_pre, g_post, wqt, w_o, cos_t, sin_t, k_pad, k_pad, vt, vt, k_meta, vt_meta)


def _rope_angles(positions):
    inv_freq = ROPE_THETA ** (-jnp.arange(0, ROT_DIM, 2, dtype=F32) / ROT_DIM)
    ang = positions.astype(F32)[:, None] * inv_freq[None, :]
    return jnp.cos(ang), jnp.sin(ang)


def _k_rope_tables(cos, sin):
    n = cos.shape[0]
    rest = LANES - ROT_DIM
    rc = jnp.concatenate([cos, cos, jnp.ones((n, rest), F32)], axis=1)
    ra = jnp.concatenate([-sin, jnp.zeros((n, ROT_HALF + rest), F32)], axis=1)
    rb = jnp.concatenate([jnp.zeros((n, ROT_HALF), F32), sin, jnp.zeros((n, rest), F32)], axis=1)
    return rc, ra, rb


def kernel(x, meta_tokens, norm_gains, w_ffn_in, w_ffn_out, gla_w_in, gla_w_gate, gla_b_gate, gla_norm, gla_w_out,
           kv_norm, w_kv, swa_w_q, swa_sinks, swa_w_out):
    batch, seq, _ = x.shape
    depth = norm_gains.shape[0]
    n_a = gla_w_in.shape[0]
    tm = 1024
    gla_rows = 256
    swa_rows = 512

    gains = norm_gains.reshape(depth, 6, 1, D_MODEL)
    w_in_b = w_ffn_in.astype(BF16)
    w_out_b = w_ffn_out.astype(BF16)

    h = x.reshape(batch * seq, D_MODEL)
    hm = jnp.pad(meta_tokens.astype(x.dtype), ((0, META_ROWS - N_META), (0, 0)))

    cos_m, sin_m = _rope_angles(jnp.arange(META_ROWS))
    cos_r, sin_r = _rope_angles(jnp.arange(N_META, N_META + seq))
    wk = jnp.pad(w_kv[:, :KV_DIM].reshape(D_MODEL, N_KV_HEADS, HEAD_DIM),
                 ((0, 0), (0, 0), (0, LANES - HEAD_DIM))).reshape(D_MODEL, K_PAD)
    wvt = w_kv[:, KV_DIM:].T
    g_kv = kv_norm.reshape(1, D_MODEL)
    kv_main = (g_kv, wk.astype(BF16), wvt.astype(BF16)) + _k_rope_tables(cos_r, sin_r)
    kv_meta = (g_kv, wk, wvt) + _k_rope_tables(cos_m, sin_m)

    k_pad = vt = k_meta = vt_meta = None
    for layer in range(depth):
        g = gains[layer]
        last = layer == depth - 1
        h = _ffn(h, g[0], g[1], w_in_b, w_out_b, (layer, 0), tm=tm)
        if layer < n_a:
            a = layer
            hm = _ffn(hm, g[0], g[1], w_ffn_in, w_ffn_out, (layer, 0), tm=META_ROWS)
            w_in_a = jnp.pad(gla_w_in[a], ((0, 0), (0, GLA_IN_PAD - gla_w_in.shape[2])))
            w_gate_a = jnp.pad(gla_w_gate[a], ((0, LANES - GLA_RANK), (0, 0)))
            gla_vecs = (gla_b_gate[a].reshape(1, GLA_QK), gla_norm[a].reshape(1, GLA_DV))
            s_zero = jnp.zeros((GLA_HEADS, GLA_DV, GLA_DK), F32)
            hm3, s_meta = _gla(hm[None], s_zero, g[2], g[3], w_in_a, w_gate_a, *gla_vecs, gla_w_out[a],
                               rows=META_ROWS, valid_rows=N_META, emit_state=True)
            hm = hm3[0]
            h = _gla(h.reshape(batch, seq, D_MODEL), s_meta[0], g[2], g[3], w_in_a.astype(BF16),
                     w_gate_a.astype(BF16), *gla_vecs, gla_w_out[a].astype(BF16), rows=gla_rows)[0]
            h = h.reshape(batch * seq, D_MODEL)
        else:
            b = layer - n_a
            wqt = (swa_w_q[b] * (HEAD_DIM ** -0.5)).T.astype(BF16)
            h = _swa(h.reshape(batch, seq, D_MODEL), swa_sinks[b], g[2], g[3], wqt, swa_w_out[b].astype(BF16),
                     cos_r.T, sin_r.T, k_pad.reshape(batch, seq, K_PAD), vt, k_meta, vt_meta, rows=swa_rows)
            h = h.reshape(batch * seq, D_MODEL)
        if layer == n_a - 1:
            h, k_pad, vt = _ffn(h, g[4], g[5], w_in_b, w_out_b, (layer, 1), tm=tm, kv=kv_main,
                                tiles_per_seq=seq // tm)
            _, k_m, vt_m = _ffn(hm, g[4], g[5], w_ffn_in, w_ffn_out, (layer, 1), tm=META_ROWS, kv=kv_meta)
            k_meta = k_m[:N_META]
            vt_meta = vt_m[0, :, :N_META]
        else:
            h = _ffn(h, g[4], g[5], w_in_b, w_out_b, (layer, 1), tm=tm)
            if not last and layer < n_a:
                hm = _ffn(hm, g[4], g[5], w_ffn_in, w_ffn_out, (layer, 1), tm=META_ROWS)
    return h.reshape(batch, seq, D_MODEL)
```

```python
import functools

import jax
import jax.numpy as jnp
from jax import lax
from jax.experimental import pallas as pl
from jax.experimental.pallas import tpu as pltpu

F32 = jnp.float32
BF16 = jnp.bfloat16

D_MODEL = 1024
D_FF = 2816
N_META = 16
EPS = 1e-6
NEG_INF = -1e30

GLA_HEADS = 4
GLA_DK = 128
GLA_DV = 256
GLA_QK = GLA_HEADS * GLA_DK
GLA_V = GLA_HEADS * GLA_DV
GLA_RANK = 16
GLA_TAU = 16.0
GLA_CHUNK = 64
GLA_IN_PAD = 2 * GLA_QK + 2 * GLA_V + 128

N_Q_HEADS = 16
N_KV_HEADS = 4
GROUP = N_Q_HEADS // N_KV_HEADS
HEAD_DIM = 64
WINDOW = 128
ROT_DIM = HEAD_DIM // 4
ROT_HALF = ROT_DIM // 2
ROPE_THETA = 500000.0
KV_DIM = N_KV_HEADS * HEAD_DIM
LANES = 128
K_PAD = N_KV_HEADS * LANES

META_ROWS = 64

_VMEM_LIMIT = 56 * 1024 * 1024

_FF_CHUNKS = ((0, 768), (768, 1536), (1536, 2304), (2304, 2816))
_FFN_SUB_ROWS = 512

_NT = (((1,), (1,)), ((), ()))
_TN = (((0,), (0,)), ((), ()))


def _rmsnorm(x, g):
    return x * lax.rsqrt(jnp.mean(x * x, axis=-1, keepdims=True) + EPS) * g


def _sigmoid(x):
    return 1.0 / (1.0 + jnp.exp(-x))


def _mm(a, b, dims=None):
    precision = lax.Precision.HIGHEST if a.dtype == F32 else None
    if dims is None:
        return jnp.dot(a, b, preferred_element_type=F32, precision=precision)
    return lax.dot_general(a, b, dims, preferred_element_type=F32, precision=precision)


def _const_spec(shape):
    zeros = (0,) * len(shape)
    return pl.BlockSpec(shape, lambda *_: zeros, pipeline_mode=pl.Buffered(1))


def _rope_rows(xt, c, s, n_heads):
    parts = []
    for h in range(n_heads):
        base = h * HEAD_DIM
        x1 = xt[base:base + ROT_HALF]
        x2 = xt[base + ROT_HALF:base + ROT_DIM]
        parts += [x1 * c - x2 * s, x2 * c + x1 * s, xt[base + ROT_DIM:base + HEAD_DIM]]
    return jnp.concatenate(parts, axis=0)


def _ffn_kernel(*refs, with_kv):
    if with_kv:
        (h_ref, gpre_ref, gpost_ref, win_ref, wout_ref, gkv_ref, wk_ref, wvt_ref, rc_ref, ra_ref, rb_ref,
         o_ref, k_ref, vt_ref, act_ref) = refs
    else:
        h_ref, gpre_ref, gpost_ref, win_ref, wout_ref, o_ref, act_ref = refs
    mxu = win_ref.dtype
    tm = h_ref.shape[0]
    n_sub = 2 if tm >= 2 * _FFN_SUB_ROWS else 1
    sub = tm // n_sub
    for blk in range(n_sub):
        rows = slice(blk * sub, (blk + 1) * sub)
        x = h_ref[rows, :]
        xn = _rmsnorm(x, gpre_ref[...]).astype(mxu)
        for lo, hi in _FF_CHUNKS:
            gate = _mm(xn, win_ref[:, lo:hi])
            up = _mm(xn, win_ref[:, D_FF + lo:D_FF + hi])
            act_ref[rows, lo:hi] = (gate * _sigmoid(gate) * up).astype(mxu)
        y = _mm(act_ref[rows, :], wout_ref[...])
        o_ref[rows, :] = x + 0.5 * _rmsnorm(y, gpost_ref[...])
    if with_kv:
        h_new = o_ref[...]
        kn = _rmsnorm(h_new, gkv_ref[...]).astype(mxu)
        k = _mm(kn, wk_ref[...])
        rc = rc_ref[...]
        ra = ra_ref[...]
        rb = rb_ref[...]
        cols = []
        for g in range(N_KV_HEADS):
            kg = k[:, g * LANES:(g + 1) * LANES]
            cols.append(kg * rc + pltpu.roll(kg, LANES - ROT_HALF, axis=1) * ra
                        + pltpu.roll(kg, ROT_HALF, axis=1) * rb)
        k_ref[...] = jnp.concatenate(cols, axis=1).astype(k_ref.dtype)
        vt_ref[0] = _mm(wvt_ref[...], kn, _NT).astype(vt_ref.dtype)


def _ffn(h, g_pre, g_post, w_in, w_out, sel, *, tm, kv=None, tiles_per_seq=1):
    t = h.shape[0]
    n = t // tm
    row_spec = pl.BlockSpec((tm, D_MODEL), lambda i: (i, 0))
    w_idx = lambda i: (sel[0], sel[1], 0, 0)
    in_specs = [row_spec, _const_spec((1, D_MODEL)), _const_spec((1, D_MODEL)),
                pl.BlockSpec((None, None, D_MODEL, 2 * D_FF), w_idx, pipeline_mode=pl.Buffered(1)),
                pl.BlockSpec((None, None, D_FF, D_MODEL), w_idx, pipeline_mode=pl.Buffered(1))]
    args = [h, g_pre, g_post, w_in, w_out]
    out_shape = [jax.ShapeDtypeStruct((t, D_MODEL), F32)]
    out_specs = [row_spec]
    if kv is not None:
        g_kv, wk, wvt, rc, ra, rb = kv
        rope_spec = pl.BlockSpec((tm, LANES), lambda i: (i % tiles_per_seq, 0))
        in_specs += [_const_spec((1, D_MODEL)), _const_spec((D_MODEL, K_PAD)), _const_spec((KV_DIM, D_MODEL)),
                     rope_spec, rope_spec, rope_spec]
        args += [g_kv, wk, wvt, rc, ra, rb]
        n_seq = n // tiles_per_seq
        out_shape += [jax.ShapeDtypeStruct((t, K_PAD), BF16),
                      jax.ShapeDtypeStruct((n_seq, KV_DIM, tiles_per_seq * tm), BF16)]
        out_specs += [pl.BlockSpec((tm, K_PAD), lambda i: (i, 0)),
                      pl.BlockSpec((1, KV_DIM, tm), lambda i: (i // tiles_per_seq, 0, i % tiles_per_seq))]
    out = pl.pallas_call(
        functools.partial(_ffn_kernel, with_kv=kv is not None),
        grid=(n,),
        in_specs=in_specs,
        out_specs=out_specs,
        out_shape=out_shape,
        scratch_shapes=[pltpu.VMEM((tm, D_FF), w_in.dtype)],
        compiler_params=pltpu.CompilerParams(dimension_semantics=("parallel",), vmem_limit_bytes=_VMEM_LIMIT),
        name="ffn_kv" if kv is not None else "ffn",
    )(*args)
    return out if kv is not None else out[0]


def _gla_kernel(*refs, rows, valid_rows, emit_state):
    if emit_state:
        (h_ref, s0_ref, gpre_ref, gpost_ref, win_ref, wgate_ref, bgate_ref, ghead_ref, wout_ref,
         o_ref, sfin_ref, st_ref) = refs
    else:
        (h_ref, s0_ref, gpre_ref, gpost_ref, win_ref, wgate_ref, bgate_ref, ghead_ref, wout_ref,
         o_ref, st_ref) = refs
    step = pl.program_id(1)
    n_chunks = rows // GLA_CHUNK

    @pl.when(step == 0)
    def _():
        st_ref[...] = s0_ref[...]

    mxu = win_ref.dtype
    x = h_ref[0]
    hn = _rmsnorm(x, gpre_ref[...]).astype(mxu)
    lr = _mm(hn, win_ref[:, 2 * GLA_QK + 2 * GLA_V:GLA_IN_PAD])
    gp = _mm(lr.astype(mxu), wgate_ref[...]) + bgate_ref[...]
    gk = (jnp.minimum(gp, 0.0) - jnp.log1p(jnp.exp(-jnp.abs(gp)))) * (1.0 / GLA_TAU)
    q = _mm(hn, win_ref[:, 0:GLA_QK])
    k = _mm(hn, win_ref[:, GLA_QK:2 * GLA_QK])
    if valid_rows is not None:
        live = lax.broadcasted_iota(jnp.int32, (rows, 1), 0) < valid_rows
        gk = jnp.where(live, gk, 0.0)
        k = jnp.where(live, k, 0.0)

    ri = lax.broadcasted_iota(jnp.int32, (rows, rows), 0)
    ci = lax.broadcasted_iota(jnp.int32, (rows, rows), 1)
    intra = ((ri // GLA_CHUNK) == (ci // GLA_CHUNK)) & (ci <= ri)
    tri = jnp.where(intra, 1.0, 0.0).astype(mxu)
    g_hi = gk.astype(mxu)
    rem = gk - g_hi.astype(F32)
    g_mid = rem.astype(mxu)
    g_lo = (rem - g_mid.astype(F32)).astype(mxu)
    parts = _mm(tri, jnp.concatenate([g_hi, g_mid, g_lo], axis=1))
    bcum = parts[:, 0:GLA_QK] + parts[:, GLA_QK:2 * GLA_QK] + parts[:, 2 * GLA_QK:3 * GLA_QK]
    btot = jnp.concatenate(
        [jnp.broadcast_to(bcum[(c + 1) * GLA_CHUNK - 1:(c + 1) * GLA_CHUNK], (GLA_CHUNK, GLA_QK))
         for c in range(n_chunks)], axis=0)

    v = _mm(hn, win_ref[:, 2 * GLA_QK:2 * GLA_QK + GLA_V])
    r = _mm(hn, win_ref[:, 2 * GLA_QK + GLA_V:2 * GLA_QK + 2 * GLA_V])
    q_dec = (q * (GLA_DK ** -0.5) * jnp.exp(bcum)).astype(mxu)
    k_dec = (k * jnp.exp(-bcum)).astype(mxu)
    k_rem = (k * jnp.exp(btot - bcum)).astype(mxu)
    decay = jnp.exp(btot)
    vb = v.astype(mxu)

    heads = range(GLA_HEADS)
    ksl = [slice(hd * GLA_DK, (hd + 1) * GLA_DK) for hd in heads]
    vsl = [slice(hd * GLA_DV, (hd + 1) * GLA_DV) for hd in heads]
    csl = [slice(c * GLA_CHUNK, (c + 1) * GLA_CHUNK) for c in range(n_chunks)]
    att = [_mm(q_dec[:, ksl[hd]], k_dec[:, ksl[hd]], _NT) for hd in heads]
    upd = [[_mm(vb[csl[c], vsl[hd]], k_rem[csl[c], ksl[hd]], _TN) for c in range(n_chunks)] for hd in heads]
    o_intra = [_mm(jnp.where(intra, att[hd], 0.0).astype(mxu), vb[:, vsl[hd]]) for hd in heads]

    outs = []
    for hd in heads:
        st_t = st_ref[hd]
        o_head = []
        for c in range(n_chunks):
            o_head.append(o_intra[hd][csl[c]] + _mm(q_dec[csl[c], ksl[hd]], st_t.astype(mxu), _NT))
            st_t = st_t * decay[c * GLA_CHUNK:c * GLA_CHUNK + 1, ksl[hd]] + upd[hd][c]
        st_ref[hd] = st_t
        outs.append(jnp.concatenate(o_head, axis=0))

    g_head = ghead_ref[...]
    mix = None
    for hd in heads:
        rg = r[:, vsl[hd]]
        gated = (_rmsnorm(outs[hd], g_head) * (rg * _sigmoid(rg))).astype(mxu)
        part = _mm(gated, wout_ref[vsl[hd], :])
        mix = part if mix is None else mix + part
    o_ref[0] = x + _rmsnorm(mix, gpost_ref[...])

    if emit_state:
        @pl.when(step == pl.num_programs(1) - 1)
        def _():
            sfin_ref[0] = st_ref[...]


def _gla(h, s0, g_pre, g_post, w_in, w_gate, b_gate, g_head, w_out, *, rows, valid_rows=None, emit_state=False):
    b, length, _ = h.shape
    steps = length // rows
    seq_spec = pl.BlockSpec((1, rows, D_MODEL), lambda i, j: (i, j, 0))
    state_shape = (GLA_HEADS, GLA_DV, GLA_DK)
    in_specs = [seq_spec, _const_spec(state_shape), _const_spec((1, D_MODEL)), _const_spec((1, D_MODEL)),
                _const_spec((D_MODEL, GLA_IN_PAD)), _const_spec((LANES, GLA_QK)), _const_spec((1, GLA_QK)),
                _const_spec((1, GLA_DV)), _const_spec((GLA_V, D_MODEL))]
    out_shape = [jax.ShapeDtypeStruct(h.shape, F32)]
    out_specs = [seq_spec]
    if emit_state:
        out_shape.append(jax.ShapeDtypeStruct((b,) + state_shape, F32))
        out_specs.append(pl.BlockSpec((1,) + state_shape, lambda i, j: (i, 0, 0, 0)))
    return pl.pallas_call(
        functools.partial(_gla_kernel, rows=rows, valid_rows=valid_rows, emit_state=emit_state),
        grid=(b, steps),
        in_specs=in_specs,
        out_specs=out_specs,
        out_shape=out_shape,
        scratch_shapes=[pltpu.VMEM(state_shape, F32)],
        compiler_params=pltpu.CompilerParams(dimension_semantics=("parallel", "arbitrary"),
                                             vmem_limit_bytes=_VMEM_LIMIT),
        name="gla_meta" if emit_state else "gla",
    )(h, s0, g_pre, g_post, w_in, w_gate, b_gate, g_head, w_out)


def _swa_kernel(sinks_ref, h_ref, gpre_ref, gpost_ref, wqt_ref, wo_ref, cos_ref, sin_ref,
                kc_ref, kp_ref, vtc_ref, vtp_ref, km_ref, vtm_ref, o_ref, ot_ref, st_ref, pt_ref, *, rows):
    step = pl.program_id(1)
    x = h_ref[0]
    hn = _rmsnorm(x, gpre_ref[...]).astype(BF16)
    qt = _mm(wqt_ref[...], hn, _NT)
    qtb = _rope_rows(qt, cos_ref[...], sin_ref[...], N_Q_HEADS).astype(BF16)

    key_j = lax.broadcasted_iota(jnp.int32, (WINDOW, WINDOW), 0)
    qry_i = lax.broadcasted_iota(jnp.int32, (WINDOW, WINDOW), 1)
    from_prev = key_j > qry_i
    first_prev_ok = key_j > qry_i + jnp.where(step > 0, 0, WINDOW)
    units = [(blk, g) for blk in range(rows // WINDOW) for g in range(N_KV_HEADS)]

    for u, (blk, g) in enumerate(units):
        c0 = blk * WINDOW
        l0 = g * LANES
        k_prev = kp_ref[0, :, l0:l0 + HEAD_DIM] if blk == 0 else kc_ref[0, c0 - WINDOW:c0, l0:l0 + HEAD_DIM]
        kcat = jnp.concatenate([km_ref[:, l0:l0 + HEAD_DIM], k_prev,
                                kc_ref[0, c0:c0 + WINDOW, l0:l0 + HEAD_DIM]], axis=0)
        qg = jnp.concatenate([qtb[(GROUP * g + i) * HEAD_DIM:(GROUP * g + i + 1) * HEAD_DIM, c0:c0 + WINDOW]
                              for i in range(GROUP)], axis=1)
        st_ref[u] = _mm(kcat, qg)

    for u, (blk, g) in enumerate(units):
        c0 = blk * WINDOW
        d0 = g * HEAD_DIM
        inv = []
        for i in range(GROUP):
            sink = sinks_ref[GROUP * g + i]
            q0 = i * WINDOW
            s_meta = st_ref[u, 0:N_META, q0:q0 + WINDOW]
            s_prev = st_ref[u, N_META:N_META + WINDOW, q0:q0 + WINDOW]
            s_cur = st_ref[u, N_META + WINDOW:N_META + 2 * WINDOW, q0:q0 + WINDOW]
            if blk == 0:
                s_prev = jnp.where(first_prev_ok, s_prev, NEG_INF)
            s_band = jnp.where(from_prev, s_prev, s_cur)
            m = jnp.maximum(jnp.max(s_band, axis=0, keepdims=True), jnp.max(s_meta, axis=0, keepdims=True))
            m = jnp.maximum(m, sink)
            p_band = jnp.exp(s_band - m)
            p_meta = jnp.exp(s_meta - m)
            den = (jnp.sum(p_band, axis=0, keepdims=True) + jnp.sum(p_meta, axis=0, keepdims=True)
                   + jnp.exp(sink - m))
            inv.append(1.0 / den)
            pt_ref[u, :, q0:q0 + WINDOW] = jnp.concatenate(
                [p_meta, jnp.where(from_prev, p_band, 0.0), jnp.where(from_prev, 0.0, p_band)],
                axis=0).astype(BF16)
        vt_prev = vtp_ref[0, d0:d0 + HEAD_DIM, :] if blk == 0 else vtc_ref[0, d0:d0 + HEAD_DIM, c0 - WINDOW:c0]
        vt_band = jnp.concatenate([vt_prev, vtc_ref[0, d0:d0 + HEAD_DIM, c0:c0 + WINDOW]], axis=1)
        og = (_mm(vt_band, pt_ref[u, N_META:, :])
              + _mm(vtm_ref[d0:d0 + HEAD_DIM, :], pt_ref[u, 0:N_META, :]))
        for i in range(GROUP):
            h0 = (GROUP * g + i) * HEAD_DIM
            ot_ref[h0:h0 + HEAD_DIM, c0:c0 + WINDOW] = og[:, i * WINDOW:(i + 1) * WINDOW] * inv[i]

    mix = _mm(ot_ref[...].astype(BF16), wo_ref[...], _TN)
    o_ref[0] = x + _rmsnorm(mix, gpost_ref[...])


def _swa(h, sinks, g_pre, g_post, wqt, w_o, cos_t, sin_t, k_pad, vt, k_meta, vt_meta, *, rows):
    b, length, _ = h.shape
    steps = length // rows
    per = rows // WINDOW
    seq_spec = pl.BlockSpec((1, rows, D_MODEL), lambda i, j: (i, j, 0))
    rope_spec = pl.BlockSpec((ROT_HALF, rows), lambda i, j: (0, j))
    prev_blk = lambda j: jnp.maximum(j * per - 1, 0)
    in_specs = [
        pl.BlockSpec(memory_space=pltpu.SMEM),
        seq_spec, _const_spec((1, D_MODEL)), _const_spec((1, D_MODEL)),
        _const_spec((D_MODEL, D_MODEL)), _const_spec((D_MODEL, D_MODEL)),
        rope_spec, rope_spec,
        pl.BlockSpec((1, rows, K_PAD), lambda i, j: (i, j, 0)),
        pl.BlockSpec((1, WINDOW, K_PAD), lambda i, j: (i, prev_blk(j), 0)),
        pl.BlockSpec((1, KV_DIM, rows), lambda i, j: (i, 0, j)),
        pl.BlockSpec((1, KV_DIM, WINDOW), lambda i, j: (i, 0, prev_blk(j))),
        _const_spec((N_META, K_PAD)), _const_spec((KV_DIM, N_META)),
    ]
    return pl.pallas_call(
        functools.partial(_swa_kernel, rows=rows),
        grid=(b, steps),
        in_specs=in_specs,
        out_specs=seq_spec,
        out_shape=jax.ShapeDtypeStruct(h.shape, F32),
        scratch_shapes=[pltpu.VMEM((D_MODEL, rows), F32),
                        pltpu.VMEM((per * N_KV_HEADS, N_META + 2 * WINDOW, GROUP * WINDOW), F32),
                        pltpu.VMEM((per * N_KV_HEADS, N_META + 2 * WINDOW, GROUP * WINDOW), BF16)],
        compiler_params=pltpu.CompilerParams(dimension_semantics=("parallel", "parallel"),
                                             vmem_limit_bytes=_VMEM_LIMIT),
        name="swa",
    )(sinks, h, g_pre, g_post, wqt, w_o, cos_t, sin_t, k_pad, k_pad, vt, vt, k_meta, vt_meta)


def _rope_angles(positions):
    inv_freq = ROPE_THETA ** (-jnp.arange(0, ROT_DIM, 2, dtype=F32) / ROT_DIM)
    ang = positions.astype(F32)[:, None] * inv_freq[None, :]
    return jnp.cos(ang), jnp.sin(ang)


def _k_rope_tables(cos, sin):
    n = cos.shape[0]
    rest = LANES - ROT_DIM
    rc = jnp.concatenate([cos, cos, jnp.ones((n, rest), F32)], axis=1)
    ra = jnp.concatenate([-sin, jnp.zeros((n, ROT_HALF + rest), F32)], axis=1)
    rb = jnp.concatenate([jnp.zeros((n, ROT_HALF), F32), sin, jnp.zeros((n, rest), F32)], axis=1)
    return rc, ra, rb


def kernel(x, meta_tokens, norm_gains, w_ffn_in, w_ffn_out, gla_w_in, gla_w_gate, gla_b_gate, gla_norm, gla_w_out,
           kv_norm, w_kv, swa_w_q, swa_sinks, swa_w_out):
    batch, seq, _ = x.shape
    depth = norm_gains.shape[0]
    n_a = gla_w_in.shape[0]
    tm = 1024
    gla_rows = 256
    swa_rows = 512

    gains = norm_gains.reshape(depth, 6, 1, D_MODEL)
    w_in_b = w_ffn_in.astype(BF16)
    w_out_b = w_ffn_out.astype(BF16)

    h = x.reshape(batch * seq, D_MODEL)
    hm = jnp.pad(meta_tokens.astype(x.dtype), ((0, META_ROWS - N_META), (0, 0)))

    cos_m, sin_m = _rope_angles(jnp.arange(META_ROWS))
    cos_r, sin_r = _rope_angles(jnp.arange(N_META, N_META + seq))
    wk = jnp.pad(w_kv[:, :KV_DIM].reshape(D_MODEL, N_KV_HEADS, HEAD_DIM),
                 ((0, 0), (0, 0), (0, LANES - HEAD_DIM))).reshape(D_MODEL, K_PAD)
    wvt = w_kv[:, KV_DIM:].T
    g_kv = kv_norm.reshape(1, D_MODEL)
    kv_main = (g_kv, wk.astype(BF16), wvt.astype(BF16)) + _k_rope_tables(cos_r, sin_r)
    kv_meta = (g_kv, wk, wvt) + _k_rope_tables(cos_m, sin_m)

    k_pad = vt = k_meta = vt_meta = None
    for layer in range(depth):
        g = gains[layer]
        last = layer == depth - 1
        h = _ffn(h, g[0], g[1], w_in_b, w_out_b, (layer, 0), tm=tm)
        if layer < n_a:
            a = layer
            hm = _ffn(hm, g[0], g[1], w_ffn_in, w_ffn_out, (layer, 0), tm=META_ROWS)
            w_in_a = jnp.pad(gla_w_in[a], ((0, 0), (0, GLA_IN_PAD - gla_w_in.shape[2])))
            w_gate_a = jnp.pad(gla_w_gate[a], ((0, LANES - GLA_RANK), (0, 0)))
            gla_vecs = (gla_b_gate[a].reshape(1, GLA_QK), gla_norm[a].reshape(1, GLA_DV))
            s_zero = jnp.zeros((GLA_HEADS, GLA_DV, GLA_DK), F32)
            hm3, s_meta = _gla(hm[None], s_zero, g[2], g[3], w_in_a, w_gate_a, *gla_vecs, gla_w_out[a],
                               rows=META_ROWS, valid_rows=N_META, emit_state=True)
            hm = hm3[0]
            h = _gla(h.reshape(batch, seq, D_MODEL), s_meta[0], g[2], g[3], w_in_a.astype(BF16),
                     w_gate_a.astype(BF16), *gla_vecs, gla_w_out[a].astype(BF16), rows=gla_rows)[0]
            h = h.reshape(batch * seq, D_MODEL)
        else:
            b = layer - n_a
            wqt = (swa_w_q[b] * (HEAD_DIM ** -0.5)).T.astype(BF16)
            h = _swa(h.reshape(batch, seq, D_MODEL), swa_sinks[b], g[2], g[3], wqt, swa_w_out[b].astype(BF16),
                     cos_r.T, sin_r.T, k_pad.reshape(batch, seq, K_PAD), vt, k_meta, vt_meta, rows=swa_rows)
            h = h.reshape(batch * seq, D_MODEL)
        if layer == n_a - 1:
            h, k_pad, vt = _ffn(h, g[4], g[5], w_in_b, w_out_b, (layer, 1), tm=tm, kv=kv_main,
                                tiles_per_seq=seq // tm)
            _, k_m, vt_m = _ffn(hm, g[4], g[5], w_ffn_in, w_ffn_out, (layer, 1), tm=META_ROWS, kv=kv_meta)
            k_meta = k_m[:N_META]
            vt_meta = vt_m[0, :, :N_META]
        else:
            h = _ffn(h, g[4], g[5], w_in_b, w_out_b, (layer, 1), tm=tm)
            if not last and layer < n_a:
                hm = _ffn(hm, g[4], g[5], w_ffn_in, w_ffn_out, (layer, 1), tm=META_ROWS)
    return h.reshape(batch, seq, D_MODEL)
```

```python
import functools

import jax
import jax.numpy as jnp
from jax import lax
from jax.experimental import pallas as pl
from jax.experimental.pallas import tpu as pltpu

F32 = jnp.float32
BF16 = jnp.bfloat16

D_MODEL = 1024
D_FF = 2816
N_META = 16
EPS = 1e-6
NEG_INF = -1e30

GLA_HEADS = 4
GLA_DK = 128
GLA_DV = 256
GLA_QK = GLA_HEADS * GLA_DK
GLA_V = GLA_HEADS * GLA_DV
GLA_RANK = 16
GLA_TAU = 16.0
GLA_CHUNK = 64
GLA_IN_MAIN = 2 * GLA_QK + 2 * GLA_V
GLA_ATT_ROWS = 256

N_Q_HEADS = 16
N_KV_HEADS = 4
GROUP = N_Q_HEADS // N_KV_HEADS
HEAD_DIM = 64
WINDOW = 128
ROT_DIM = HEAD_DIM // 4
ROT_HALF = ROT_DIM // 2
ROPE_THETA = 500000.0
KV_DIM = N_KV_HEADS * HEAD_DIM
LANES = 128
K_PAD = N_KV_HEADS * LANES

META_ROWS = 64

_VMEM_LIMIT = 56 * 1024 * 1024

_FF_CHUNKS = ((0, 768), (768, 1536), (1536, 2304), (2304, 2816))
_FFN_SUB_ROWS = 512
_SWA_PASS_ROWS = 512

_NT = (((1,), (1,)), ((), ()))
_TN = (((0,), (0,)), ((), ()))


def _rmsnorm(x, g):
    return x * lax.rsqrt(jnp.mean(x * x, axis=-1, keepdims=True) + EPS) * g


def _sigmoid(x):
    return 1.0 / (1.0 + jnp.exp(-x))


def _mm(a, b, dims=None):
    precision = lax.Precision.HIGHEST if a.dtype == F32 else None
    if dims is None:
        return jnp.dot(a, b, preferred_element_type=F32, precision=precision)
    return lax.dot_general(a, b, dims, preferred_element_type=F32, precision=precision)


def _const_spec(shape):
    zeros = (0,) * len(shape)
    return pl.BlockSpec(shape, lambda *_: zeros, pipeline_mode=pl.Buffered(1))


def _rope_rows(xt, c, s, n_heads):
    parts = []
    for h in range(n_heads):
        base = h * HEAD_DIM
        x1 = xt[base:base + ROT_HALF]
        x2 = xt[base + ROT_HALF:base + ROT_DIM]
        parts += [x1 * c - x2 * s, x2 * c + x1 * s, xt[base + ROT_DIM:base + HEAD_DIM]]
    return jnp.concatenate(parts, axis=0)


def _ffn_kernel(*refs, with_kv):
    if with_kv:
        (h_ref, gpre_ref, gpost_ref, win_ref, wout_ref, gkv_ref, wk_ref, wvt_ref, rc_ref, ra_ref, rb_ref,
         o_ref, k_ref, vt_ref, act_ref) = refs
    else:
        h_ref, gpre_ref, gpost_ref, win_ref, wout_ref, o_ref, act_ref = refs
    mxu = win_ref.dtype
    tm = h_ref.shape[0]
    n_sub = 2 if tm >= 2 * _FFN_SUB_ROWS else 1
    sub = tm // n_sub
    for blk in range(n_sub):
        rows = slice(blk * sub, (blk + 1) * sub)
        x = h_ref[rows, :]
        xn = _rmsnorm(x, gpre_ref[...]).astype(mxu)
        for lo, hi in _FF_CHUNKS:
            gate = _mm(xn, win_ref[:, lo:hi])
            up = _mm(xn, win_ref[:, D_FF + lo:D_FF + hi])
            act_ref[rows, lo:hi] = (gate * _sigmoid(gate) * up).astype(mxu)
        y = _mm(act_ref[rows, :], wout_ref[...])
        o_ref[rows, :] = x + 0.5 * _rmsnorm(y, gpost_ref[...])
    if with_kv:
        h_new = o_ref[...]
        kn = _rmsnorm(h_new, gkv_ref[...]).astype(mxu)
        k = _mm(kn, wk_ref[...])
        rc = rc_ref[...]
        ra = ra_ref[...]
        rb = rb_ref[...]
        cols = []
        for g in range(N_KV_HEADS):
            kg = k[:, g * LANES:(g + 1) * LANES]
            cols.append(kg * rc + pltpu.roll(kg, LANES - ROT_HALF, axis=1) * ra
                        + pltpu.roll(kg, ROT_HALF, axis=1) * rb)
        k_ref[...] = jnp.concatenate(cols, axis=1).astype(k_ref.dtype)
        vt_ref[0] = _mm(wvt_ref[...], kn, _NT).astype(vt_ref.dtype)


def _ffn(h, g_pre, g_post, w_in, w_out, sel, *, tm, kv=None, tiles_per_seq=1):
    t = h.shape[0]
    n = t // tm
    row_spec = pl.BlockSpec((tm, D_MODEL), lambda i: (i, 0))
    w_idx = lambda i: (sel[0], sel[1], 0, 0)
    in_specs = [row_spec, _const_spec((1, D_MODEL)), _const_spec((1, D_MODEL)),
                pl.BlockSpec((None, None, D_MODEL, 2 * D_FF), w_idx, pipeline_mode=pl.Buffered(1)),
                pl.BlockSpec((None, None, D_FF, D_MODEL), w_idx, pipeline_mode=pl.Buffered(1))]
    args = [h, g_pre, g_post, w_in, w_out]
    out_shape = [jax.ShapeDtypeStruct((t, D_MODEL), F32)]
    out_specs = [row_spec]
    if kv is not None:
        g_kv, wk, wvt, rc, ra, rb = kv
        rope_spec = pl.BlockSpec((tm, LANES), lambda i: (i % tiles_per_seq, 0))
        in_specs += [_const_spec((1, D_MODEL)), _const_spec((D_MODEL, K_PAD)), _const_spec((KV_DIM, D_MODEL)),
                     rope_spec, rope_spec, rope_spec]
        args += [g_kv, wk, wvt, rc, ra, rb]
        n_seq = n // tiles_per_seq
        out_shape += [jax.ShapeDtypeStruct((t, K_PAD), BF16),
                      jax.ShapeDtypeStruct((n_seq, KV_DIM, tiles_per_seq * tm), BF16)]
        out_specs += [pl.BlockSpec((tm, K_PAD), lambda i: (i, 0)),
                      pl.BlockSpec((1, KV_DIM, tm), lambda i: (i // tiles_per_seq, 0, i % tiles_per_seq))]
    out = pl.pallas_call(
        functools.partial(_ffn_kernel, with_kv=kv is not None),
        grid=(n,),
        in_specs=in_specs,
        out_specs=out_specs,
        out_shape=out_shape,
        scratch_shapes=[pltpu.VMEM((tm, D_FF), w_in.dtype)],
        compiler_params=pltpu.CompilerParams(dimension_semantics=("parallel",), vmem_limit_bytes=_VMEM_LIMIT),
        name="ffn_kv" if kv is not None else "ffn",
    )(*args)
    return out if kv is not None else out[0]


def _gla_kernel(*refs, rows, valid_rows, emit_state):
    if emit_state:
        (h_ref, s0_ref, gpre_ref, gpost_ref, win_ref, wlr_ref, wgate_ref, bgate_ref, ghead_ref, wout_ref,
         o_ref, sfin_ref, st_ref) = refs
    else:
        (h_ref, s0_ref, gpre_ref, gpost_ref, win_ref, wlr_ref, wgate_ref, bgate_ref, ghead_ref, wout_ref,
         o_ref, st_ref) = refs
    step = pl.program_id(1)
    n_chunks = rows // GLA_CHUNK
    blk_rows = min(rows, GLA_ATT_ROWS)
    bsl = [slice(b * blk_rows, (b + 1) * blk_rows) for b in range(rows // blk_rows)]

    @pl.when(step == 0)
    def _():
        st_ref[...] = s0_ref[...]

    mxu = win_ref.dtype
    x = h_ref[0]
    hn = _rmsnorm(x, gpre_ref[...]).astype(mxu)
    lr = _mm(hn, wlr_ref[...])
    gp = _mm(lr.astype(mxu), wgate_ref[...]) + bgate_ref[...]
    gk = (jnp.minimum(gp, 0.0) - jnp.log1p(jnp.exp(-jnp.abs(gp)))) * (1.0 / GLA_TAU)
    q = _mm(hn, win_ref[:, 0:GLA_QK])
    k = _mm(hn, win_ref[:, GLA_QK:2 * GLA_QK])
    if valid_rows is not None:
        live = lax.broadcasted_iota(jnp.int32, (rows, 1), 0) < valid_rows
        gk = jnp.where(live, gk, 0.0)
        k = jnp.where(live, k, 0.0)

    ri = lax.broadcasted_iota(jnp.int32, (blk_rows, blk_rows), 0)
    ci = lax.broadcasted_iota(jnp.int32, (blk_rows, blk_rows), 1)
    intra = ((ri // GLA_CHUNK) == (ci // GLA_CHUNK)) & (ci <= ri)
    tri = jnp.where(intra, 1.0, 0.0).astype(mxu)
    g_hi = gk.astype(mxu)
    rem = gk - g_hi.astype(F32)
    g_mid = rem.astype(mxu)
    g_lo = (rem - g_mid.astype(F32)).astype(mxu)
    g_split = jnp.concatenate([g_hi, g_mid, g_lo], axis=1)
    parts = jnp.concatenate([_mm(tri, g_split[rs]) for rs in bsl], axis=0)
    bcum = parts[:, 0:GLA_QK] + parts[:, GLA_QK:2 * GLA_QK] + parts[:, 2 * GLA_QK:3 * GLA_QK]
    btot = jnp.concatenate(
        [jnp.broadcast_to(bcum[(c + 1) * GLA_CHUNK - 1:(c + 1) * GLA_CHUNK], (GLA_CHUNK, GLA_QK))
         for c in range(n_chunks)], axis=0)

    v = _mm(hn, win_ref[:, 2 * GLA_QK:2 * GLA_QK + GLA_V])
    r = _mm(hn, win_ref[:, 2 * GLA_QK + GLA_V:2 * GLA_QK + 2 * GLA_V])
    q_dec = (q * (GLA_DK ** -0.5) * jnp.exp(bcum)).astype(mxu)
    k_dec = (k * jnp.exp(-bcum)).astype(mxu)
    k_rem = (k * jnp.exp(btot - bcum)).astype(mxu)
    decay = jnp.exp(btot)
    vb = v.astype(mxu)

    heads = range(GLA_HEADS)
    ksl = [slice(hd * GLA_DK, (hd + 1) * GLA_DK) for hd in heads]
    vsl = [slice(hd * GLA_DV, (hd + 1) * GLA_DV) for hd in heads]
    csl = [slice(c * GLA_CHUNK, (c + 1) * GLA_CHUNK) for c in range(n_chunks)]
    att = [[_mm(q_dec[rs, ksl[hd]], k_dec[rs, ksl[hd]], _NT) for rs in bsl] for hd in heads]
    upd = [[_mm(vb[csl[c], vsl[hd]], k_rem[csl[c], ksl[hd]], _TN) for c in range(n_chunks)] for hd in heads]
    o_intra = [jnp.concatenate([_mm(jnp.where(intra, att[hd][b], 0.0).astype(mxu), vb[rs, vsl[hd]])
                                for b, rs in enumerate(bsl)], axis=0) for hd in heads]

    outs = []
    for hd in heads:
        st_t = st_ref[hd]
        o_head = []
        for c in range(n_chunks):
            o_head.append(o_intra[hd][csl[c]] + _mm(q_dec[csl[c], ksl[hd]], st_t.astype(mxu), _NT))
            st_t = st_t * decay[c * GLA_CHUNK:c * GLA_CHUNK + 1, ksl[hd]] + upd[hd][c]
        st_ref[hd] = st_t
        outs.append(jnp.concatenate(o_head, axis=0))

    g_head = ghead_ref[...]
    mix = None
    for hd in heads:
        rg = r[:, vsl[hd]]
        gated = (_rmsnorm(outs[hd], g_head) * (rg * _sigmoid(rg))).astype(mxu)
        part = _mm(gated, wout_ref[vsl[hd], :])
        mix = part if mix is None else mix + part
    o_ref[0] = x + _rmsnorm(mix, gpost_ref[...])

    if emit_state:
        @pl.when(step == pl.num_programs(1) - 1)
        def _():
            sfin_ref[0] = st_ref[...]


def _gla(h, s0, g_pre, g_post, w_in, layer, w_lr, w_gate, b_gate, g_head, w_out, *, rows, valid_rows=None,
         emit_state=False):
    b, length, _ = h.shape
    steps = length // rows
    seq_spec = pl.BlockSpec((1, rows, D_MODEL), lambda i, j: (i, j, 0))
    state_shape = (GLA_HEADS, GLA_DV, GLA_DK)
    in_specs = [seq_spec, _const_spec(state_shape), _const_spec((1, D_MODEL)), _const_spec((1, D_MODEL)),
                pl.BlockSpec((None, D_MODEL, GLA_IN_MAIN), lambda i, j: (layer, 0, 0), pipeline_mode=pl.Buffered(1)),
                _const_spec((D_MODEL, LANES)), _const_spec((LANES, GLA_QK)), _const_spec((1, GLA_QK)),
                _const_spec((1, GLA_DV)), _const_spec((GLA_V, D_MODEL))]
    out_shape = [jax.ShapeDtypeStruct(h.shape, F32)]
    out_specs = [seq_spec]
    if emit_state:
        out_shape.append(jax.ShapeDtypeStruct((b,) + state_shape, F32))
        out_specs.append(pl.BlockSpec((1,) + state_shape, lambda i, j: (i, 0, 0, 0)))
    return pl.pallas_call(
        functools.partial(_gla_kernel, rows=rows, valid_rows=valid_rows, emit_state=emit_state),
        grid=(b, steps),
        in_specs=in_specs,
        out_specs=out_specs,
        out_shape=out_shape,
        scratch_shapes=[pltpu.VMEM(state_shape, F32)],
        compiler_params=pltpu.CompilerParams(dimension_semantics=("parallel", "arbitrary"),
                                             vmem_limit_bytes=_VMEM_LIMIT),
        name="gla_meta" if emit_state else "gla",
    )(h, s0, g_pre, g_post, w_in, w_lr, w_gate, b_gate, g_head, w_out)


def _swa_kernel(sinks_ref, h_ref, gpre_ref, gpost_ref, wqt_ref, wo_ref, cos_ref, sin_ref,
                kc_ref, kp_ref, vtc_ref, vtp_ref, km_ref, vtm_ref, o_ref, ot_ref, st_ref, pt_ref, *, rows):
    step = pl.program_id(1)
    x = h_ref[0]
    hn = _rmsnorm(x, gpre_ref[...]).astype(BF16)
    qt = _mm(wqt_ref[...], hn, _NT)
    qtb = _rope_rows(qt, cos_ref[...], sin_ref[...], N_Q_HEADS).astype(BF16)

    key_j = lax.broadcasted_iota(jnp.int32, (WINDOW, WINDOW), 0)
    qry_i = lax.broadcasted_iota(jnp.int32, (WINDOW, WINDOW), 1)
    from_prev = key_j > qry_i
    first_prev_ok = key_j > qry_i + jnp.where(step > 0, 0, WINDOW)
    per_pass = min(rows, _SWA_PASS_ROWS) // WINDOW
    for first_blk in range(0, rows // WINDOW, per_pass):
        _swa_pass(first_blk, per_pass, sinks_ref, qtb, from_prev, first_prev_ok, kc_ref, kp_ref, vtc_ref, vtp_ref,
                  km_ref, vtm_ref, ot_ref, st_ref, pt_ref)

    mix = _mm(ot_ref[...].astype(BF16), wo_ref[...], _TN)
    o_ref[0] = x + _rmsnorm(mix, gpost_ref[...])


def _swa_pass(first_blk, n_blk, sinks_ref, qtb, from_prev, first_prev_ok, kc_ref, kp_ref, vtc_ref, vtp_ref,
              km_ref, vtm_ref, ot_ref, st_ref, pt_ref):
    units = [(blk, g) for blk in range(first_blk, first_blk + n_blk) for g in range(N_KV_HEADS)]

    for u, (blk, g) in enumerate(units):
        c0 = blk * WINDOW
        l0 = g * LANES
        k_prev = kp_ref[0, :, l0:l0 + HEAD_DIM] if blk == 0 else kc_ref[0, c0 - WINDOW:c0, l0:l0 + HEAD_DIM]
        kcat = jnp.concatenate([km_ref[:, l0:l0 + HEAD_DIM], k_prev,
                                kc_ref[0, c0:c0 + WINDOW, l0:l0 + HEAD_DIM]], axis=0)
        qg = jnp.concatenate([qtb[(GROUP * g + i) * HEAD_DIM:(GROUP * g + i + 1) * HEAD_DIM, c0:c0 + WINDOW]
                              for i in range(GROUP)], axis=1)
        st_ref[u] = _mm(kcat, qg)

    for u, (blk, g) in enumerate(units):
        c0 = blk * WINDOW
        d0 = g * HEAD_DIM
        inv = []
        for i in range(GROUP):
            sink = sinks_ref[GROUP * g + i]
            q0 = i * WINDOW
            s_meta = st_ref[u, 0:N_META, q0:q0 + WINDOW]
            s_prev = st_ref[u, N_META:N_META + WINDOW, q0:q0 + WINDOW]
            s_cur = st_ref[u, N_META + WINDOW:N_META + 2 * WINDOW, q0:q0 + WINDOW]
            if blk == 0:
                s_prev = jnp.where(first_prev_ok, s_prev, NEG_INF)
            s_band = jnp.where(from_prev, s_prev, s_cur)
            m = jnp.maximum(jnp.max(s_band, axis=0, keepdims=True), jnp.max(s_meta, axis=0, keepdims=True))
            m = jnp.maximum(m, sink)
            p_band = jnp.exp(s_band - m)
            p_meta = jnp.exp(s_meta - m)
            den = (jnp.sum(p_band, axis=0, keepdims=True) + jnp.sum(p_meta, axis=0, keepdims=True)
                   + jnp.exp(sink - m))
            inv.append(1.0 / den)
            pt_ref[u, :, q0:q0 + WINDOW] = jnp.concatenate(
                [p_meta, jnp.where(from_prev, p_band, 0.0), jnp.where(from_prev, 0.0, p_band)],
                axis=0).astype(BF16)
        vt_prev = vtp_ref[0, d0:d0 + HEAD_DIM, :] if blk == 0 else vtc_ref[0, d0:d0 + HEAD_DIM, c0 - WINDOW:c0]
        vt_band = jnp.concatenate([vt_prev, vtc_ref[0, d0:d0 + HEAD_DIM, c0:c0 + WINDOW]], axis=1)
        og = (_mm(vt_band, pt_ref[u, N_META:, :])
              + _mm(vtm_ref[d0:d0 + HEAD_DIM, :], pt_ref[u, 0:N_META, :]))
        for i in range(GROUP):
            h0 = (GROUP * g + i) * HEAD_DIM
            ot_ref[h0:h0 + HEAD_DIM, c0:c0 + WINDOW] = og[:, i * WINDOW:(i + 1) * WINDOW] * inv[i]


def _swa(h, sinks, g_pre, g_post, wqt, w_o, cos_t, sin_t, k_pad, vt, k_meta, vt_meta, *, rows):
    b, length, _ = h.shape
    steps = length // rows
    per = rows // WINDOW
    n_units = min(rows, _SWA_PASS_ROWS) // WINDOW * N_KV_HEADS
    seq_spec = pl.BlockSpec((1, rows, D_MODEL), lambda i, j: (i, j, 0))
    rope_spec = pl.BlockSpec((ROT_HALF, rows), lambda i, j: (0, j))
    prev_blk = lambda j: jnp.maximum(j * per - 1, 0)
    in_specs = [
        pl.BlockSpec(memory_space=pltpu.SMEM),
        seq_spec, _const_spec((1, D_MODEL)), _const_spec((1, D_MODEL)),
        _const_spec((D_MODEL, D_MODEL)), _const_spec((D_MODEL, D_MODEL)),
        rope_spec, rope_spec,
        pl.BlockSpec((1, rows, K_PAD), lambda i, j: (i, j, 0)),
        pl.BlockSpec((1, WINDOW, K_PAD), lambda i, j: (i, prev_blk(j), 0)),
        pl.BlockSpec((1, KV_DIM, rows), lambda i, j: (i, 0, j)),
        pl.BlockSpec((1, KV_DIM, WINDOW), lambda i, j: (i, 0, prev_blk(j))),
        _const_spec((N_META, K_PAD)), _const_spec((KV_DIM, N_META)),
    ]
    return pl.pallas_call(
        functools.partial(_swa_kernel, rows=rows),
        grid=(b, steps),
        in_specs=in_specs,
        out_specs=seq_spec,
        out_shape=jax.ShapeDtypeStruct(h.shape, F32),
        scratch_shapes=[pltpu.VMEM((D_MODEL, rows), F32),
                        pltpu.VMEM((n_units, N_META + 2 * WINDOW, GROUP * WINDOW), F32),
                        pltpu.VMEM((n_units, N_META + 2 * WINDOW, GROUP * WINDOW), BF16)],
        compiler_params=pltpu.CompilerParams(dimension_semantics=("parallel", "parallel"),
                                             vmem_limit_bytes=_VMEM_LIMIT),
        name="swa",
    )(sinks, h, g_pre, g_post, wqt, w_o, cos_t, sin_t, k_pad, k_pad, vt, vt, k_meta, vt_meta)


def _rope_angles(positions):
    inv_freq = ROPE_THETA ** (-jnp.arange(0, ROT_DIM, 2, dtype=F32) / ROT_DIM)
    ang = positions.astype(F32)[:, None] * inv_freq[None, :]
    return jnp.cos(ang), jnp.sin(ang)


def _k_rope_tables(cos, sin):
    n = cos.shape[0]
    rest = LANES - ROT_DIM
    rc = jnp.concatenate([cos, cos, jnp.ones((n, rest), F32)], axis=1)
    ra = jnp.concatenate([-sin, jnp.zeros((n, ROT_HALF + rest), F32)], axis=1)
    rb = jnp.concatenate([jnp.zeros((n, ROT_HALF), F32), sin, jnp.zeros((n, rest), F32)], axis=1)
    return rc, ra, rb


def kernel(x, meta_tokens, norm_gains, w_ffn_in, w_ffn_out, gla_w_in, gla_w_gate, gla_b_gate, gla_norm, gla_w_out,
           kv_norm, w_kv, swa_w_q, swa_sinks, swa_w_out):
    batch, seq, _ = x.shape
    depth = norm_gains.shape[0]
    n_a = gla_w_in.shape[0]
    tm = 1024
    gla_rows = 512
    swa_rows = 1024

    gains = norm_gains.reshape(depth, 6, 1, D_MODEL)
    w_in_b = w_ffn_in.astype(BF16)
    w_out_b = w_ffn_out.astype(BF16)
    gla_w_in_b = gla_w_in.astype(BF16)

    h = x.reshape(batch * seq, D_MODEL)
    hm = jnp.pad(meta_tokens.astype(x.dtype), ((0, META_ROWS - N_META), (0, 0)))

    cos_m, sin_m = _rope_angles(jnp.arange(META_ROWS))
    cos_r, sin_r = _rope_angles(jnp.arange(N_META, N_META + seq))
    w_kv_b = w_kv.astype(BF16)
    wk = jnp.pad(w_kv_b[:, :KV_DIM].reshape(D_MODEL, N_KV_HEADS, HEAD_DIM),
                 ((0, 0), (0, 0), (0, LANES - HEAD_DIM))).reshape(D_MODEL, K_PAD)
    kv_w = (kv_norm.reshape(1, D_MODEL), wk, w_kv_b[:, KV_DIM:].T)
    kv_main = kv_w + _k_rope_tables(cos_r, sin_r)
    kv_meta = kv_w + _k_rope_tables(cos_m, sin_m)

    k_pad = vt = k_meta = vt_meta = None
    for layer in range(depth):
        g = gains[layer]
        last = layer == depth - 1
        h = _ffn(h, g[0], g[1], w_in_b, w_out_b, (layer, 0), tm=tm)
        if layer < n_a:
            a = layer
            hm = _ffn(hm, g[0], g[1], w_ffn_in, w_ffn_out, (layer, 0), tm=META_ROWS)
            w_lr = jnp.pad(gla_w_in[a, :, GLA_IN_MAIN:], ((0, 0), (0, LANES - GLA_RANK)))
            w_gate_a = jnp.pad(gla_w_gate[a], ((0, LANES - GLA_RANK), (0, 0)))
            gla_vecs = (gla_b_gate[a].reshape(1, GLA_QK), gla_norm[a].reshape(1, GLA_DV))
            s_zero = jnp.zeros((GLA_HEADS, GLA_DV, GLA_DK), F32)
            hm3, s_meta = _gla(hm[None], s_zero, g[2], g[3], gla_w_in, a, w_lr, w_gate_a, *gla_vecs, gla_w_out[a],
                               rows=META_ROWS, valid_rows=N_META, emit_state=True)
            hm = hm3[0]
            h = _gla(h.reshape(batch, seq, D_MODEL), s_meta[0], g[2], g[3], gla_w_in_b, a, w_lr.astype(BF16),
                     w_gate_a.astype(BF16), *gla_vecs, gla_w_out[a].astype(BF16), rows=gla_rows)[0]
            h = h.reshape(batch * seq, D_MODEL)
        else:
            b = layer - n_a
            wqt = (swa_w_q[b].astype(BF16) * (HEAD_DIM ** -0.5)).T
            h = _swa(h.reshape(batch, seq, D_MODEL), swa_sinks[b], g[2], g[3], wqt, swa_w_out[b].astype(BF16),
                     cos_r.T, sin_r.T, k_pad.reshape(batch, seq, K_PAD), vt, k_meta, vt_meta, rows=swa_rows)
            h = h.reshape(batch * seq, D_MODEL)
        if layer == n_a - 1:
            h, k_pad, vt = _ffn(h, g[4], g[5], w_in_b, w_out_b, (layer, 1), tm=tm, kv=kv_main,
                                tiles_per_seq=seq // tm)
            _, k_m, vt_m = _ffn(hm, g[4], g[5], w_in_b, w_out_b, (layer, 1), tm=META_ROWS, kv=kv_meta)
            k_meta = k_m[:N_META]
            vt_meta = vt_m[0, :, :N_META]
        else:
            h = _ffn(h, g[4], g[5], w_in_b, w_out_b, (layer, 1), tm=tm)
            if not last and layer < n_a:
                hm = _ffn(hm, g[4], g[5], w_ffn_in, w_ffn_out, (layer, 1), tm=META_ROWS)
    return h.reshape(batch, seq, D_MODEL)
```

```python
import functools

import jax
import jax.numpy as jnp
from jax import lax
from jax.experimental import pallas as pl
from jax.experimental.pallas import tpu as pltpu

F32 = jnp.float32
BF16 = jnp.bfloat16

D_MODEL = 1024
D_FF = 2816
N_META = 16
EPS = 1e-6
NEG_INF = -1e30

GLA_HEADS = 4
GLA_DK = 128
GLA_DV = 256
GLA_QK = GLA_HEADS * GLA_DK
GLA_V = GLA_HEADS * GLA_DV
GLA_RANK = 16
GLA_TAU = 16.0
GLA_CHUNK = 64
GLA_IN_MAIN = 2 * GLA_QK + 2 * GLA_V
GLA_ATT_ROWS = 256

N_Q_HEADS = 16
N_KV_HEADS = 4
GROUP = N_Q_HEADS // N_KV_HEADS
HEAD_DIM = 64
WINDOW = 128
ROT_DIM = HEAD_DIM // 4
ROT_HALF = ROT_DIM // 2
ROPE_THETA = 500000.0
KV_DIM = N_KV_HEADS * HEAD_DIM
LANES = 128
K_PAD = N_KV_HEADS * LANES

META_ROWS = 64

_VMEM_LIMIT = 56 * 1024 * 1024

_FF_CHUNKS = ((0, 768), (768, 1536), (1536, 2304), (2304, 2816))
_FFN_SUB_ROWS = 512
_SWA_PASS_BLOCKS = 2

_NT = (((1,), (1,)), ((), ()))
_TN = (((0,), (0,)), ((), ()))


def _rmsnorm(x, g):
    return x * lax.rsqrt(jnp.mean(x * x, axis=-1, keepdims=True) + EPS) * g


def _sigmoid(x):
    return 1.0 / (1.0 + jnp.exp(-x))


def _mm(a, b, dims=None):
    precision = lax.Precision.HIGHEST if a.dtype == F32 else None
    if dims is None:
        return jnp.dot(a, b, preferred_element_type=F32, precision=precision)
    return lax.dot_general(a, b, dims, preferred_element_type=F32, precision=precision)


def _const_spec(shape):
    zeros = (0,) * len(shape)
    return pl.BlockSpec(shape, lambda *_: zeros, pipeline_mode=pl.Buffered(1))


def _rope_rows(xt, c, s, n_heads):
    parts = []
    for h in range(n_heads):
        base = h * HEAD_DIM
        x1 = xt[base:base + ROT_HALF]
        x2 = xt[base + ROT_HALF:base + ROT_DIM]
        parts += [x1 * c - x2 * s, x2 * c + x1 * s, xt[base + ROT_DIM:base + HEAD_DIM]]
    return jnp.concatenate(parts, axis=0)


def _ffn_kernel(*refs, with_kv):
    if with_kv:
        (h_ref, gpre_ref, gpost_ref, win_ref, wout_ref, gkv_ref, wk_ref, wvt_ref, rc_ref, ra_ref, rb_ref,
         o_ref, k_ref, vt_ref, act_ref) = refs
    else:
        h_ref, gpre_ref, gpost_ref, win_ref, wout_ref, o_ref, act_ref = refs
    mxu = win_ref.dtype
    tm = h_ref.shape[0]
    n_sub = 2 if tm >= 2 * _FFN_SUB_ROWS else 1
    sub = tm // n_sub
    for blk in range(n_sub):
        rows = slice(blk * sub, (blk + 1) * sub)
        x = h_ref[rows, :]
        xn = _rmsnorm(x, gpre_ref[...]).astype(mxu)
        for lo, hi in _FF_CHUNKS:
            gate = _mm(xn, win_ref[:, lo:hi])
            up = _mm(xn, win_ref[:, D_FF + lo:D_FF + hi])
            act_ref[rows, lo:hi] = (gate * _sigmoid(gate) * up).astype(mxu)
        y = _mm(act_ref[rows, :], wout_ref[...])
        o_ref[rows, :] = x + 0.5 * _rmsnorm(y, gpost_ref[...])
    if with_kv:
        h_new = o_ref[...]
        kn = _rmsnorm(h_new, gkv_ref[...]).astype(mxu)
        k = _mm(kn, wk_ref[...])
        rc = rc_ref[...]
        ra = ra_ref[...]
        rb = rb_ref[...]
        cols = []
        for g in range(N_KV_HEADS):
            kg = k[:, g * LANES:(g + 1) * LANES]
            cols.append(kg * rc + pltpu.roll(kg, LANES - ROT_HALF, axis=1) * ra
                        + pltpu.roll(kg, ROT_HALF, axis=1) * rb)
        k_ref[...] = jnp.concatenate(cols, axis=1).astype(k_ref.dtype)
        vt_ref[0] = _mm(wvt_ref[...], kn, _NT).astype(vt_ref.dtype)


def _ffn(h, g_pre, g_post, w_in, w_out, sel, *, tm, kv=None, tiles_per_seq=1):
    t = h.shape[0]
    n = t // tm
    row_spec = pl.BlockSpec((tm, D_MODEL), lambda i: (i, 0))
    w_idx = lambda i: (sel[0], sel[1], 0, 0)
    in_specs = [row_spec, _const_spec((1, D_MODEL)), _const_spec((1, D_MODEL)),
                pl.BlockSpec((None, None, D_MODEL, 2 * D_FF), w_idx, pipeline_mode=pl.Buffered(1)),
                pl.BlockSpec((None, None, D_FF, D_MODEL), w_idx, pipeline_mode=pl.Buffered(1))]
    args = [h, g_pre, g_post, w_in, w_out]
    out_shape = [jax.ShapeDtypeStruct((t, D_MODEL), F32)]
    out_specs = [row_spec]
    if kv is not None:
        g_kv, wk, wvt, rc, ra, rb = kv
        rope_spec = pl.BlockSpec((tm, LANES), lambda i: (i % tiles_per_seq, 0))
        in_specs += [_const_spec((1, D_MODEL)), _const_spec((D_MODEL, K_PAD)), _const_spec((KV_DIM, D_MODEL)),
                     rope_spec, rope_spec, rope_spec]
        args += [g_kv, wk, wvt, rc, ra, rb]
        n_seq = n // tiles_per_seq
        out_shape += [jax.ShapeDtypeStruct((t, K_PAD), BF16),
                      jax.ShapeDtypeStruct((n_seq, KV_DIM, tiles_per_seq * tm), BF16)]
        out_specs += [pl.BlockSpec((tm, K_PAD), lambda i: (i, 0)),
                      pl.BlockSpec((1, KV_DIM, tm), lambda i: (i // tiles_per_seq, 0, i % tiles_per_seq))]
    out = pl.pallas_call(
        functools.partial(_ffn_kernel, with_kv=kv is not None),
        grid=(n,),
        in_specs=in_specs,
        out_specs=out_specs,
        out_shape=out_shape,
        scratch_shapes=[pltpu.VMEM((tm, D_FF), w_in.dtype)],
        compiler_params=pltpu.CompilerParams(dimension_semantics=("parallel",), vmem_limit_bytes=_VMEM_LIMIT),
        name="ffn_kv" if kv is not None else "ffn",
    )(*args)
    return out if kv is not None else out[0]


def _gla_kernel(*refs, rows, valid_rows, emit_state):
    if emit_state:
        (h_ref, s0_ref, gpre_ref, gpost_ref, win_ref, wlr_ref, wgate_ref, bgate_ref, ghead_ref, wout_ref,
         o_ref, sfin_ref, st_ref) = refs
    else:
        (h_ref, s0_ref, gpre_ref, gpost_ref, win_ref, wlr_ref, wgate_ref, bgate_ref, ghead_ref, wout_ref,
         o_ref, st_ref) = refs
    step = pl.program_id(1)
    n_chunks = rows // GLA_CHUNK
    blk_rows = min(rows, GLA_ATT_ROWS)
    bsl = [slice(b * blk_rows, (b + 1) * blk_rows) for b in range(rows // blk_rows)]

    @pl.when(step == 0)
    def _():
        st_ref[...] = s0_ref[...]

    mxu = win_ref.dtype
    x = h_ref[0]
    hn = _rmsnorm(x, gpre_ref[...]).astype(mxu)
    lr = _mm(hn, wlr_ref[...])
    gp = _mm(lr.astype(mxu), wgate_ref[...]) + bgate_ref[...]
    gk = (jnp.minimum(gp, 0.0) - jnp.log1p(jnp.exp(-jnp.abs(gp)))) * (1.0 / GLA_TAU)
    q = _mm(hn, win_ref[:, 0:GLA_QK])
    k = _mm(hn, win_ref[:, GLA_QK:2 * GLA_QK])
    if valid_rows is not None:
        live = lax.broadcasted_iota(jnp.int32, (rows, 1), 0) < valid_rows
        gk = jnp.where(live, gk, 0.0)
        k = jnp.where(live, k, 0.0)

    ri = lax.broadcasted_iota(jnp.int32, (blk_rows, blk_rows), 0)
    ci = lax.broadcasted_iota(jnp.int32, (blk_rows, blk_rows), 1)
    intra = ((ri // GLA_CHUNK) == (ci // GLA_CHUNK)) & (ci <= ri)
    tri = jnp.where(intra, 1.0, 0.0).astype(mxu)
    g_hi = gk.astype(mxu)
    rem = gk - g_hi.astype(F32)
    g_mid = rem.astype(mxu)
    g_lo = (rem - g_mid.astype(F32)).astype(mxu)
    g_split = jnp.concatenate([g_hi, g_mid, g_lo], axis=1)
    parts = jnp.concatenate([_mm(tri, g_split[rs]) for rs in bsl], axis=0)
    bcum = parts[:, 0:GLA_QK] + parts[:, GLA_QK:2 * GLA_QK] + parts[:, 2 * GLA_QK:3 * GLA_QK]
    btot = jnp.concatenate(
        [jnp.broadcast_to(bcum[(c + 1) * GLA_CHUNK - 1:(c + 1) * GLA_CHUNK], (GLA_CHUNK, GLA_QK))
         for c in range(n_chunks)], axis=0)

    v = _mm(hn, win_ref[:, 2 * GLA_QK:2 * GLA_QK + GLA_V])
    r = _mm(hn, win_ref[:, 2 * GLA_QK + GLA_V:2 * GLA_QK + 2 * GLA_V])
    q_dec = (q * (GLA_DK ** -0.5) * jnp.exp(bcum)).astype(mxu)
    k_dec = (k * jnp.exp(-bcum)).astype(mxu)
    k_rem = (k * jnp.exp(btot - bcum)).astype(mxu)
    decay = jnp.exp(btot)
    vb = v.astype(mxu)

    heads = range(GLA_HEADS)
    ksl = [slice(hd * GLA_DK, (hd + 1) * GLA_DK) for hd in heads]
    vsl = [slice(hd * GLA_DV, (hd + 1) * GLA_DV) for hd in heads]
    csl = [slice(c * GLA_CHUNK, (c + 1) * GLA_CHUNK) for c in range(n_chunks)]
    att = [[_mm(q_dec[rs, ksl[hd]], k_dec[rs, ksl[hd]], _NT) for rs in bsl] for hd in heads]
    upd = [[_mm(vb[csl[c], vsl[hd]], k_rem[csl[c], ksl[hd]], _TN) for c in range(n_chunks)] for hd in heads]
    o_intra = [jnp.concatenate([_mm(jnp.where(intra, att[hd][b], 0.0).astype(mxu), vb[rs, vsl[hd]])
                                for b, rs in enumerate(bsl)], axis=0) for hd in heads]

    outs = []
    for hd in heads:
        st_t = st_ref[hd]
        o_head = []
        for c in range(n_chunks):
            o_head.append(o_intra[hd][csl[c]] + _mm(q_dec[csl[c], ksl[hd]], st_t.astype(mxu), _NT))
            st_t = st_t * decay[c * GLA_CHUNK:c * GLA_CHUNK + 1, ksl[hd]] + upd[hd][c]
        st_ref[hd] = st_t
        outs.append(jnp.concatenate(o_head, axis=0))

    g_head = ghead_ref[...]
    gated = []
    for hd in heads:
        rg = r[:, vsl[hd]]
        gated.append((_rmsnorm(outs[hd], g_head) * (rg * _sigmoid(rg))).astype(mxu))
    for rs in bsl:
        mix = _mm(gated[0][rs], wout_ref[vsl[0], :])
        for hd in heads[1:]:
            mix = mix + _mm(gated[hd][rs], wout_ref[vsl[hd], :])
        o_ref[0, rs, :] = h_ref[0, rs, :] + _rmsnorm(mix, gpost_ref[...])

    if emit_state:
        @pl.when(step == pl.num_programs(1) - 1)
        def _():
            sfin_ref[0] = st_ref[...]


def _gla(h, s0, g_pre, g_post, w_in, layer, w_lr, w_gate, b_gate, g_head, w_out, *, rows, valid_rows=None,
         emit_state=False):
    b, length, _ = h.shape
    steps = length // rows
    seq_spec = pl.BlockSpec((1, rows, D_MODEL), lambda i, j: (i, j, 0))
    state_shape = (GLA_HEADS, GLA_DV, GLA_DK)
    in_specs = [seq_spec, _const_spec(state_shape), _const_spec((1, D_MODEL)), _const_spec((1, D_MODEL)),
                pl.BlockSpec((None, D_MODEL, GLA_IN_MAIN), lambda i, j: (layer, 0, 0), pipeline_mode=pl.Buffered(1)),
                _const_spec((D_MODEL, LANES)), _const_spec((LANES, GLA_QK)), _const_spec((1, GLA_QK)),
                _const_spec((1, GLA_DV)), _const_spec((GLA_V, D_MODEL))]
    out_shape = [jax.ShapeDtypeStruct(h.shape, F32)]
    out_specs = [seq_spec]
    if emit_state:
        out_shape.append(jax.ShapeDtypeStruct((b,) + state_shape, F32))
        out_specs.append(pl.BlockSpec((1,) + state_shape, lambda i, j: (i, 0, 0, 0)))
    return pl.pallas_call(
        functools.partial(_gla_kernel, rows=rows, valid_rows=valid_rows, emit_state=emit_state),
        grid=(b, steps),
        in_specs=in_specs,
        out_specs=out_specs,
        out_shape=out_shape,
        scratch_shapes=[pltpu.VMEM(state_shape, F32)],
        compiler_params=pltpu.CompilerParams(dimension_semantics=("parallel", "arbitrary"),
                                             vmem_limit_bytes=_VMEM_LIMIT),
        name="gla_meta" if emit_state else "gla",
    )(h, s0, g_pre, g_post, w_in, w_lr, w_gate, b_gate, g_head, w_out)


def _swa_kernel(sinks_ref, h_ref, gpre_ref, gpost_ref, wqt_ref, wo_ref, cos_ref, sin_ref,
                kc_ref, kp_ref, vtc_ref, vtp_ref, km_ref, vtm_ref, o_ref, ot_ref, st_ref, pt_ref, *, rows):
    step = pl.program_id(1)
    x = h_ref[0]
    hn = _rmsnorm(x, gpre_ref[...]).astype(BF16)
    qt = _mm(wqt_ref[...], hn, _NT)
    qtb = _rope_rows(qt, cos_ref[...], sin_ref[...], N_Q_HEADS).astype(BF16)

    key_j = lax.broadcasted_iota(jnp.int32, (WINDOW, WINDOW), 0)
    qry_i = lax.broadcasted_iota(jnp.int32, (WINDOW, WINDOW), 1)
    from_prev = key_j > qry_i
    first_prev_ok = key_j > qry_i + jnp.where(step > 0, 0, WINDOW)
    n_pass = rows // (WINDOW * _SWA_PASS_BLOCKS)

    def pass_units(p):
        return [(blk, g) for blk in range(p * _SWA_PASS_BLOCKS, (p + 1) * _SWA_PASS_BLOCKS)
                for g in range(N_KV_HEADS)]

    def scores(p):
        for u, (blk, g) in enumerate(pass_units(p)):
            c0 = blk * WINDOW
            l0 = g * LANES
            k_prev = kp_ref[0, :, l0:l0 + HEAD_DIM] if blk == 0 else kc_ref[0, c0 - WINDOW:c0, l0:l0 + HEAD_DIM]
            kcat = jnp.concatenate([km_ref[:, l0:l0 + HEAD_DIM], k_prev,
                                    kc_ref[0, c0:c0 + WINDOW, l0:l0 + HEAD_DIM]], axis=0)
            qg = jnp.concatenate([qtb[(GROUP * g + i) * HEAD_DIM:(GROUP * g + i + 1) * HEAD_DIM, c0:c0 + WINDOW]
                                  for i in range(GROUP)], axis=1)
            st_ref[p % 2, u] = _mm(kcat, qg)

    def softmax(p):
        inv = []
        for u, (blk, g) in enumerate(pass_units(p)):
            for i in range(GROUP):
                sink = sinks_ref[GROUP * g + i]
                q0 = i * WINDOW
                s_meta = st_ref[p % 2, u, 0:N_META, q0:q0 + WINDOW]
                s_prev = st_ref[p % 2, u, N_META:N_META + WINDOW, q0:q0 + WINDOW]
                s_cur = st_ref[p % 2, u, N_META + WINDOW:N_META + 2 * WINDOW, q0:q0 + WINDOW]
                if blk == 0:
                    s_prev = jnp.where(first_prev_ok, s_prev, NEG_INF)
                s_band = jnp.where(from_prev, s_prev, s_cur)
                m = jnp.maximum(jnp.max(s_band, axis=0, keepdims=True), jnp.max(s_meta, axis=0, keepdims=True))
                m = jnp.maximum(m, sink)
                p_band = jnp.exp(s_band - m)
                p_meta = jnp.exp(s_meta - m)
                den = (jnp.sum(p_band, axis=0, keepdims=True) + jnp.sum(p_meta, axis=0, keepdims=True)
                       + jnp.exp(sink - m))
                inv.append(1.0 / den)
                pt_ref[p % 2, u, :, q0:q0 + WINDOW] = jnp.concatenate(
                    [p_meta, jnp.where(from_prev, p_band, 0.0), jnp.where(from_prev, 0.0, p_band)],
                    axis=0).astype(BF16)
        return inv

    def values(p, inv):
        for u, (blk, g) in enumerate(pass_units(p)):
            c0 = blk * WINDOW
            d0 = g * HEAD_DIM
            vt_prev = vtp_ref[0, d0:d0 + HEAD_DIM, :] if blk == 0 else vtc_ref[0, d0:d0 + HEAD_DIM, c0 - WINDOW:c0]
            vt_band = jnp.concatenate([vt_prev, vtc_ref[0, d0:d0 + HEAD_DIM, c0:c0 + WINDOW]], axis=1)
            og = (_mm(vt_band, pt_ref[p % 2, u, N_META:, :])
                  + _mm(vtm_ref[d0:d0 + HEAD_DIM, :], pt_ref[p % 2, u, 0:N_META, :]))
            for i in range(GROUP):
                h0 = (GROUP * g + i) * HEAD_DIM
                ot_ref[h0:h0 + HEAD_DIM, c0:c0 + WINDOW] = og[:, i * WINDOW:(i + 1) * WINDOW] * inv[GROUP * u + i]

    pass_rows = WINDOW * _SWA_PASS_BLOCKS
    scores(0)
    for p in range(n_pass):
        if p + 1 < n_pass:
            scores(p + 1)
        values(p, softmax(p))
        rs = slice(p * pass_rows, (p + 1) * pass_rows)
        mix = _mm(ot_ref[:, rs].astype(BF16), wo_ref[...], _TN)
        o_ref[0, rs, :] = h_ref[0, rs, :] + _rmsnorm(mix, gpost_ref[...])


def _swa(h, sinks, g_pre, g_post, wqt, w_o, cos_t, sin_t, k_pad, vt, k_meta, vt_meta, *, rows):
    b, length, _ = h.shape
    steps = length // rows
    per = rows // WINDOW
    n_units = _SWA_PASS_BLOCKS * N_KV_HEADS
    seq_spec = pl.BlockSpec((1, rows, D_MODEL), lambda i, j: (i, j, 0))
    rope_spec = pl.BlockSpec((ROT_HALF, rows), lambda i, j: (0, j))
    prev_blk = lambda j: jnp.maximum(j * per - 1, 0)
    in_specs = [
        pl.BlockSpec(memory_space=pltpu.SMEM),
        seq_spec, _const_spec((1, D_MODEL)), _const_spec((1, D_MODEL)),
        _const_spec((D_MODEL, D_MODEL)), _const_spec((D_MODEL, D_MODEL)),
        rope_spec, rope_spec,
        pl.BlockSpec((1, rows, K_PAD), lambda i, j: (i, j, 0)),
        pl.BlockSpec((1, WINDOW, K_PAD), lambda i, j: (i, prev_blk(j), 0)),
        pl.BlockSpec((1, KV_DIM, rows), lambda i, j: (i, 0, j)),
        pl.BlockSpec((1, KV_DIM, WINDOW), lambda i, j: (i, 0, prev_blk(j))),
        _const_spec((N_META, K_PAD)), _const_spec((KV_DIM, N_META)),
    ]
    return pl.pallas_call(
        functools.partial(_swa_kernel, rows=rows),
        grid=(b, steps),
        in_specs=in_specs,
        out_specs=seq_spec,
        out_shape=jax.ShapeDtypeStruct(h.shape, F32),
        scratch_shapes=[pltpu.VMEM((D_MODEL, rows), F32),
                        pltpu.VMEM((2, n_units, N_META + 2 * WINDOW, GROUP * WINDOW), F32),
                        pltpu.VMEM((2, n_units, N_META + 2 * WINDOW, GROUP * WINDOW), BF16)],
        compiler_params=pltpu.CompilerParams(dimension_semantics=("parallel", "parallel"),
                                             vmem_limit_bytes=_VMEM_LIMIT),
        name="swa",
    )(sinks, h, g_pre, g_post, wqt, w_o, cos_t, sin_t, k_pad, k_pad, vt, vt, k_meta, vt_meta)


def _rope_angles(positions):
    inv_freq = ROPE_THETA ** (-jnp.arange(0, ROT_DIM, 2, dtype=F32) / ROT_DIM)
    ang = positions.astype(F32)[:, None] * inv_freq[None, :]
    return jnp.cos(ang), jnp.sin(ang)


def _k_rope_tables(cos, sin):
    n = cos.shape[0]
    rest = LANES - ROT_DIM
    rc = jnp.concatenate([cos, cos, jnp.ones((n, rest), F32)], axis=1)
    ra = jnp.concatenate([-sin, jnp.zeros((n, ROT_HALF + rest), F32)], axis=1)
    rb = jnp.concatenate([jnp.zeros((n, ROT_HALF), F32), sin, jnp.zeros((n, rest), F32)], axis=1)
    return rc, ra, rb


def kernel(x, meta_tokens, norm_gains, w_ffn_in, w_ffn_out, gla_w_in, gla_w_gate, gla_b_gate, gla_norm, gla_w_out,
           kv_norm, w_kv, swa_w_q, swa_sinks, swa_w_out):
    batch, seq, _ = x.shape
    depth = norm_gains.shape[0]
    n_a = gla_w_in.shape[0]
    tm = 1024
    gla_rows = 1024
    swa_rows = 1024

    gains = norm_gains.reshape(depth, 6, 1, D_MODEL)
    w_in_b = w_ffn_in.astype(BF16)
    w_out_b = w_ffn_out.astype(BF16)
    gla_w_in_b = gla_w_in.astype(BF16)

    h = x.reshape(batch * seq, D_MODEL)
    hm = jnp.pad(meta_tokens.astype(x.dtype), ((0, META_ROWS - N_META), (0, 0)))

    cos_m, sin_m = _rope_angles(jnp.arange(META_ROWS))
    cos_r, sin_r = _rope_angles(jnp.arange(N_META, N_META + seq))
    w_kv_b = w_kv.astype(BF16)
    wk = jnp.pad(w_kv_b[:, :KV_DIM].reshape(D_MODEL, N_KV_HEADS, HEAD_DIM),
                 ((0, 0), (0, 0), (0, LANES - HEAD_DIM))).reshape(D_MODEL, K_PAD)
    kv_w = (kv_norm.reshape(1, D_MODEL), wk, w_kv_b[:, KV_DIM:].T)
    kv_main = kv_w + _k_rope_tables(cos_r, sin_r)
    kv_meta = kv_w + _k_rope_tables(cos_m, sin_m)

    k_pad = vt = k_meta = vt_meta = None
    for layer in range(depth):
        g = gains[layer]
        last = layer == depth - 1
        h = _ffn(h, g[0], g[1], w_in_b, w_out_b, (layer, 0), tm=tm)
        if layer < n_a:
            a = layer
            hm = _ffn(hm, g[0], g[1], w_ffn_in, w_ffn_out, (layer, 0), tm=META_ROWS)
            w_lr = jnp.pad(gla_w_in[a, :, GLA_IN_MAIN:], ((0, 0), (0, LANES - GLA_RANK)))
            w_gate_a = jnp.pad(gla_w_gate[a], ((0, LANES - GLA_RANK), (0, 0)))
            gla_vecs = (gla_b_gate[a].reshape(1, GLA_QK), gla_norm[a].reshape(1, GLA_DV))
            s_zero = jnp.zeros((GLA_HEADS, GLA_DV, GLA_DK), F32)
            hm3, s_meta = _gla(hm[None], s_zero, g[2], g[3], gla_w_in, a, w_lr, w_gate_a, *gla_vecs, gla_w_out[a],
                               rows=META_ROWS, valid_rows=N_META, emit_state=True)
            hm = hm3[0]
            h = _gla(h.reshape(batch, seq, D_MODEL), s_meta[0], g[2], g[3], gla_w_in_b, a, w_lr.astype(BF16),
                     w_gate_a.astype(BF16), *gla_vecs, gla_w_out[a].astype(BF16), rows=gla_rows)[0]
            h = h.reshape(batch * seq, D_MODEL)
        else:
            b = layer - n_a
            wqt = (swa_w_q[b].astype(BF16) * (HEAD_DIM ** -0.5)).T
            h = _swa(h.reshape(batch, seq, D_MODEL), swa_sinks[b], g[2], g[3], wqt, swa_w_out[b].astype(BF16),
                     cos_r.T, sin_r.T, k_pad.reshape(batch, seq, K_PAD), vt, k_meta, vt_meta, rows=swa_rows)
            h = h.reshape(batch * seq, D_MODEL)
        if layer == n_a - 1:
            h, k_pad, vt = _ffn(h, g[4], g[5], w_in_b, w_out_b, (layer, 1), tm=tm, kv=kv_main,
                                tiles_per_seq=seq // tm)
            _, k_m, vt_m = _ffn(hm, g[4], g[5], w_in_b, w_out_b, (layer, 1), tm=META_ROWS, kv=kv_meta)
            k_meta = k_m[:N_META]
            vt_meta = vt_m[0, :, :N_META]
        else:
            h = _ffn(h, g[4], g[5], w_in_b, w_out_b, (layer, 1), tm=tm)
            if not last and layer < n_a:
                hm = _ffn(hm, g[4], g[5], w_ffn_in, w_ffn_out, (layer, 1), tm=META_ROWS)
    return h.reshape(batch, seq, D_MODEL)
```

```python
import functools

import jax
import jax.numpy as jnp
from jax import lax
from jax.experimental import pallas as pl
from jax.experimental.pallas import tpu as pltpu

F32 = jnp.float32
BF16 = jnp.bfloat16

D_MODEL = 1024
D_FF = 2816
N_META = 16
EPS = 1e-6
NEG_INF = -1e30

GLA_HEADS = 4
GLA_DK = 128
GLA_DV = 256
GLA_QK = GLA_HEADS * GLA_DK
GLA_V = GLA_HEADS * GLA_DV
GLA_RANK = 16
GLA_TAU = 16.0
GLA_CHUNK = 64
GLA_IN_MAIN = 2 * GLA_QK + 2 * GLA_V
GLA_ATT_ROWS = 256

N_Q_HEADS = 16
N_KV_HEADS = 4
GROUP = N_Q_HEADS // N_KV_HEADS
HEAD_DIM = 64
WINDOW = 128
ROT_DIM = HEAD_DIM // 4
ROT_HALF = ROT_DIM // 2
ROPE_THETA = 500000.0
KV_DIM = N_KV_HEADS * HEAD_DIM
LANES = 128
K_PAD = N_KV_HEADS * LANES

META_ROWS = 64

_VMEM_LIMIT = 56 * 1024 * 1024

_FF_CHUNKS = ((0, 768), (768, 1536), (1536, 2304), (2304, 2816))
_FFN_SUB_ROWS = 512
_WOUT_VIEW = (2 * D_FF, D_MODEL // 2)
_SWA_PASS_ROWS = 512

_NT = (((1,), (1,)), ((), ()))
_TN = (((0,), (0,)), ((), ()))


def _rmsnorm(x, g):
    return x * lax.rsqrt(jnp.mean(x * x, axis=-1, keepdims=True) + EPS) * g


def _sigmoid(x):
    return 1.0 / (1.0 + jnp.exp(-x))


def _mm(a, b, dims=None):
    precision = lax.Precision.HIGHEST if a.dtype == F32 else None
    if dims is None:
        return jnp.dot(a, b, preferred_element_type=F32, precision=precision)
    return lax.dot_general(a, b, dims, preferred_element_type=F32, precision=precision)


def _const_spec(shape):
    zeros = (0,) * len(shape)
    return pl.BlockSpec(shape, lambda *_: zeros, pipeline_mode=pl.Buffered(1))


def _rope_rows(xt, c, s, n_heads):
    parts = []
    for h in range(n_heads):
        base = h * HEAD_DIM
        x1 = xt[base:base + ROT_HALF]
        x2 = xt[base + ROT_HALF:base + ROT_DIM]
        parts += [x1 * c - x2 * s, x2 * c + x1 * s, xt[base + ROT_DIM:base + HEAD_DIM]]
    return jnp.concatenate(parts, axis=0)


def _ffn_kernel(*refs, with_kv, with_cast):
    refs = list(refs)
    act_ref = refs.pop()
    if with_cast:
        cast_dst = [refs.pop(), refs.pop()][::-1]
    if with_kv:
        k_ref, vt_ref = refs[-2:]
        del refs[-2:]
    o_ref = refs.pop()
    if with_cast:
        cast_src = [refs.pop(), refs.pop()][::-1]
        for src, dst in zip(cast_src, cast_dst):
            dst[...] = src[...].astype(dst.dtype)
    if with_kv:
        gkv_ref, wk_ref, wvt_ref, rc_ref, ra_ref, rb_ref = refs[-6:]
        del refs[-6:]
    h_ref, gpre_ref, gpost_ref, win_ref, wout_ref = refs
    mxu = win_ref.dtype
    tm = h_ref.shape[0]
    n_sub = 2 if tm >= 2 * _FFN_SUB_ROWS else 1
    sub = tm // n_sub
    for blk in range(n_sub):
        rows = slice(blk * sub, (blk + 1) * sub)
        x = h_ref[rows, :]
        xn = _rmsnorm(x, gpre_ref[...]).astype(mxu)
        for lo, hi in _FF_CHUNKS:
            gate = _mm(xn, win_ref[:, lo:hi])
            up = _mm(xn, win_ref[:, D_FF + lo:D_FF + hi])
            act_ref[rows, lo:hi] = (gate * _sigmoid(gate) * up).astype(mxu)
        y = _mm(act_ref[rows, :], wout_ref[...])
        o_ref[rows, :] = x + 0.5 * _rmsnorm(y, gpost_ref[...])
    if with_kv:
        h_new = o_ref[...]
        kn = _rmsnorm(h_new, gkv_ref[...]).astype(mxu)
        k = _mm(kn, wk_ref[...])
        rc = rc_ref[...]
        ra = ra_ref[...]
        rb = rb_ref[...]
        cols = []
        for g in range(N_KV_HEADS):
            kg = k[:, g * LANES:(g + 1) * LANES]
            cols.append(kg * rc + pltpu.roll(kg, LANES - ROT_HALF, axis=1) * ra
                        + pltpu.roll(kg, ROT_HALF, axis=1) * rb)
        k_ref[...] = jnp.concatenate(cols, axis=1).astype(k_ref.dtype)
        vt_ref[0] = _mm(wvt_ref[...], kn, _NT).astype(vt_ref.dtype)


def _weight_arg(w, shape):
    if isinstance(w, tuple):
        stack, sel = w
        return stack, pl.BlockSpec((None, None) + shape, lambda *_: (sel[0], sel[1], 0, 0),
                                   pipeline_mode=pl.Buffered(1))
    return w, _const_spec(shape)


def _ffn(h, g_pre, g_post, w_in, w_out, *, tm, kv=None, tiles_per_seq=1, cast_next=None):
    t = h.shape[0]
    n = t // tm
    row_spec = pl.BlockSpec((tm, D_MODEL), lambda i: (i, 0))
    w_in, w_in_spec = _weight_arg(w_in, (D_MODEL, 2 * D_FF))
    w_out, w_out_spec = _weight_arg(w_out, (D_FF, D_MODEL))
    in_specs = [row_spec, _const_spec((1, D_MODEL)), _const_spec((1, D_MODEL)), w_in_spec, w_out_spec]
    args = [h, g_pre, g_post, w_in, w_out]
    out_shape = [jax.ShapeDtypeStruct((t, D_MODEL), F32)]
    out_specs = [row_spec]
    if kv is not None:
        g_kv, wk, wvt, rc, ra, rb = kv
        rope_spec = pl.BlockSpec((tm, LANES), lambda i: (i % tiles_per_seq, 0))
        in_specs += [_const_spec((1, D_MODEL)), _const_spec((D_MODEL, K_PAD)), _const_spec((KV_DIM, D_MODEL)),
                     rope_spec, rope_spec, rope_spec]
        args += [g_kv, wk, wvt, rc, ra, rb]
        n_seq = n // tiles_per_seq
        out_shape += [jax.ShapeDtypeStruct((t, K_PAD), BF16),
                      jax.ShapeDtypeStruct((n_seq, KV_DIM, tiles_per_seq * tm), BF16)]
        out_specs += [pl.BlockSpec((tm, K_PAD), lambda i: (i, 0)),
                      pl.BlockSpec((1, KV_DIM, tm), lambda i: (i // tiles_per_seq, 0, i % tiles_per_seq))]
    if cast_next is not None:
        nxt_in, nxt_out, (lyr, which) = cast_next
        for src, shape in ((nxt_in, (D_MODEL, 2 * D_FF)), (nxt_out, _WOUT_VIEW)):
            blk = (shape[0] // n, shape[1])
            in_specs.append(pl.BlockSpec((None, None) + blk, lambda i: (lyr, which, i, 0)))
            args.append(src)
            out_shape.append(jax.ShapeDtypeStruct(shape, BF16))
            out_specs.append(pl.BlockSpec(blk, lambda i: (i, 0)))
    out = pl.pallas_call(
        functools.partial(_ffn_kernel, with_kv=kv is not None, with_cast=cast_next is not None),
        grid=(n,),
        in_specs=in_specs,
        out_specs=out_specs,
        out_shape=out_shape,
        scratch_shapes=[pltpu.VMEM((tm, D_FF), w_in.dtype)],
        compiler_params=pltpu.CompilerParams(dimension_semantics=("parallel",), vmem_limit_bytes=_VMEM_LIMIT),
        name="ffn_kv" if kv is not None else "ffn",
    )(*args)
    return out if len(out) > 1 else out[0]


def _gla_kernel(*refs, rows, valid_rows, emit_state):
    if emit_state:
        (h_ref, s0_ref, gpre_ref, gpost_ref, win_ref, wlr_ref, wgate_ref, bgate_ref, ghead_ref, wout_ref,
         o_ref, sfin_ref, st_ref) = refs
    else:
        (h_ref, s0_ref, gpre_ref, gpost_ref, win_ref, wlr_ref, wgate_ref, bgate_ref, ghead_ref, wout_ref,
         o_ref, st_ref) = refs
    step = pl.program_id(1)
    n_chunks = rows // GLA_CHUNK
    blk_rows = min(rows, GLA_ATT_ROWS)
    bsl = [slice(b * blk_rows, (b + 1) * blk_rows) for b in range(rows // blk_rows)]

    @pl.when(step == 0)
    def _():
        st_ref[...] = s0_ref[...]

    mxu = win_ref.dtype
    x = h_ref[0]
    hn = _rmsnorm(x, gpre_ref[...]).astype(mxu)
    lr = _mm(hn, wlr_ref[...])
    gp = _mm(lr.astype(mxu), wgate_ref[...]) + bgate_ref[...]
    gk = (jnp.minimum(gp, 0.0) - jnp.log1p(jnp.exp(-jnp.abs(gp)))) * (1.0 / GLA_TAU)
    q = _mm(hn, win_ref[:, 0:GLA_QK])
    k = _mm(hn, win_ref[:, GLA_QK:2 * GLA_QK])
    if valid_rows is not None:
        live = lax.broadcasted_iota(jnp.int32, (rows, 1), 0) < valid_rows
        gk = jnp.where(live, gk, 0.0)
        k = jnp.where(live, k, 0.0)

    ri = lax.broadcasted_iota(jnp.int32, (blk_rows, blk_rows), 0)
    ci = lax.broadcasted_iota(jnp.int32, (blk_rows, blk_rows), 1)
    intra = ((ri // GLA_CHUNK) == (ci // GLA_CHUNK)) & (ci <= ri)
    tri = jnp.where(intra, 1.0, 0.0).astype(mxu)
    g_hi = gk.astype(mxu)
    rem = gk - g_hi.astype(F32)
    g_mid = rem.astype(mxu)
    g_lo = (rem - g_mid.astype(F32)).astype(mxu)
    g_split = jnp.concatenate([g_hi, g_mid, g_lo], axis=1)
    parts = jnp.concatenate([_mm(tri, g_split[rs]) for rs in bsl], axis=0)
    bcum = parts[:, 0:GLA_QK] + parts[:, GLA_QK:2 * GLA_QK] + parts[:, 2 * GLA_QK:3 * GLA_QK]
    btot = jnp.concatenate(
        [jnp.broadcast_to(bcum[(c + 1) * GLA_CHUNK - 1:(c + 1) * GLA_CHUNK], (GLA_CHUNK, GLA_QK))
         for c in range(n_chunks)], axis=0)

    v = _mm(hn, win_ref[:, 2 * GLA_QK:2 * GLA_QK + GLA_V])
    r = _mm(hn, win_ref[:, 2 * GLA_QK + GLA_V:2 * GLA_QK + 2 * GLA_V])
    q_dec = (q * (GLA_DK ** -0.5) * jnp.exp(bcum)).astype(mxu)
    k_dec = (k * jnp.exp(-bcum)).astype(mxu)
    k_rem = (k * jnp.exp(btot - bcum)).astype(mxu)
    decay = jnp.exp(btot)
    vb = v.astype(mxu)

    heads = range(GLA_HEADS)
    ksl = [slice(hd * GLA_DK, (hd + 1) * GLA_DK) for hd in heads]
    vsl = [slice(hd * GLA_DV, (hd + 1) * GLA_DV) for hd in heads]
    csl = [slice(c * GLA_CHUNK, (c + 1) * GLA_CHUNK) for c in range(n_chunks)]
    att = [[_mm(q_dec[rs, ksl[hd]], k_dec[rs, ksl[hd]], _NT) for rs in bsl] for hd in heads]
    upd = [[_mm(vb[csl[c], vsl[hd]], k_rem[csl[c], ksl[hd]], _TN) for c in range(n_chunks)] for hd in heads]
    o_intra = [jnp.concatenate([_mm(jnp.where(intra, att[hd][b], 0.0).astype(mxu), vb[rs, vsl[hd]])
                                for b, rs in enumerate(bsl)], axis=0) for hd in heads]

    outs = []
    for hd in heads:
        st_t = st_ref[hd]
        o_head = []
        for c in range(n_chunks):
            o_head.append(o_intra[hd][csl[c]] + _mm(q_dec[csl[c], ksl[hd]], st_t.astype(mxu), _NT))
            st_t = st_t * decay[c * GLA_CHUNK:c * GLA_CHUNK + 1, ksl[hd]] + upd[hd][c]
        st_ref[hd] = st_t
        outs.append(jnp.concatenate(o_head, axis=0))

    g_head = ghead_ref[...]
    gated = []
    for hd in heads:
        rg = r[:, vsl[hd]]
        gated.append((_rmsnorm(outs[hd], g_head) * (rg * _sigmoid(rg))).astype(mxu))
    for rs in bsl:
        mix = _mm(gated[0][rs], wout_ref[vsl[0], :])
        for hd in heads[1:]:
            mix = mix + _mm(gated[hd][rs], wout_ref[vsl[hd], :])
        o_ref[0, rs, :] = h_ref[0, rs, :] + _rmsnorm(mix, gpost_ref[...])

    if emit_state:
        @pl.when(step == pl.num_programs(1) - 1)
        def _():
            sfin_ref[0] = st_ref[...]


def _gla(h, s0, g_pre, g_post, w_in, layer, w_lr, w_gate, b_gate, g_head, w_out, *, rows, valid_rows=None,
         emit_state=False):
    b, length, _ = h.shape
    steps = length // rows
    seq_spec = pl.BlockSpec((1, rows, D_MODEL), lambda i, j: (i, j, 0))
    state_shape = (GLA_HEADS, GLA_DV, GLA_DK)
    in_specs = [seq_spec, _const_spec(state_shape), _const_spec((1, D_MODEL)), _const_spec((1, D_MODEL)),
                pl.BlockSpec((None, D_MODEL, GLA_IN_MAIN), lambda i, j: (layer, 0, 0), pipeline_mode=pl.Buffered(1)),
                _const_spec((D_MODEL, LANES)), _const_spec((LANES, GLA_QK)), _const_spec((1, GLA_QK)),
                _const_spec((1, GLA_DV)), _const_spec((GLA_V, D_MODEL))]
    out_shape = [jax.ShapeDtypeStruct(h.shape, F32)]
    out_specs = [seq_spec]
    if emit_state:
        out_shape.append(jax.ShapeDtypeStruct((b,) + state_shape, F32))
        out_specs.append(pl.BlockSpec((1,) + state_shape, lambda i, j: (i, 0, 0, 0)))
    return pl.pallas_call(
        functools.partial(_gla_kernel, rows=rows, valid_rows=valid_rows, emit_state=emit_state),
        grid=(b, steps),
        in_specs=in_specs,
        out_specs=out_specs,
        out_shape=out_shape,
        scratch_shapes=[pltpu.VMEM(state_shape, F32)],
        compiler_params=pltpu.CompilerParams(dimension_semantics=("parallel", "arbitrary"),
                                             vmem_limit_bytes=_VMEM_LIMIT),
        name="gla_meta" if emit_state else "gla",
    )(h, s0, g_pre, g_post, w_in, w_lr, w_gate, b_gate, g_head, w_out)


def _swa_kernel(sinks_ref, h_ref, gpre_ref, gpost_ref, wqt_ref, wo_ref, cos_ref, sin_ref,
                kc_ref, kp_ref, vtc_ref, vtp_ref, km_ref, vtm_ref, o_ref, ot_ref, st_ref, pt_ref, *, rows):
    step = pl.program_id(1)
    x = h_ref[0]
    hn = _rmsnorm(x, gpre_ref[...]).astype(BF16)
    qt = _mm(wqt_ref[...], hn, _NT)
    qtb = _rope_rows(qt, cos_ref[...], sin_ref[...], N_Q_HEADS).astype(BF16)

    key_j = lax.broadcasted_iota(jnp.int32, (WINDOW, WINDOW), 0)
    qry_i = lax.broadcasted_iota(jnp.int32, (WINDOW, WINDOW), 1)
    from_prev = key_j > qry_i
    first_prev_ok = key_j > qry_i + jnp.where(step > 0, 0, WINDOW)
    per_pass = min(rows, _SWA_PASS_ROWS) // WINDOW
    for first_blk in range(0, rows // WINDOW, per_pass):
        _swa_pass(first_blk, per_pass, sinks_ref, qtb, from_prev, first_prev_ok, kc_ref, kp_ref, vtc_ref, vtp_ref,
                  km_ref, vtm_ref, ot_ref, st_ref, pt_ref)

    mix = _mm(ot_ref[...].astype(BF16), wo_ref[...], _TN)
    o_ref[0] = x + _rmsnorm(mix, gpost_ref[...])


def _swa_pass(first_blk, n_blk, sinks_ref, qtb, from_prev, first_prev_ok, kc_ref, kp_ref, vtc_ref, vtp_ref,
              km_ref, vtm_ref, ot_ref, st_ref, pt_ref):
    units = [(blk, g) for blk in range(first_blk, first_blk + n_blk) for g in range(N_KV_HEADS)]

    for u, (blk, g) in enumerate(units):
        c0 = blk * WINDOW
        l0 = g * LANES
        k_prev = kp_ref[0, :, l0:l0 + HEAD_DIM] if blk == 0 else kc_ref[0, c0 - WINDOW:c0, l0:l0 + HEAD_DIM]
        kcat = jnp.concatenate([km_ref[:, l0:l0 + HEAD_DIM], k_prev,
                                kc_ref[0, c0:c0 + WINDOW, l0:l0 + HEAD_DIM]], axis=0)
        qg = jnp.concatenate([qtb[(GROUP * g + i) * HEAD_DIM:(GROUP * g + i + 1) * HEAD_DIM, c0:c0 + WINDOW]
                              for i in range(GROUP)], axis=1)
        st_ref[u] = _mm(kcat, qg)

    for u, (blk, g) in enumerate(units):
        c0 = blk * WINDOW
        d0 = g * HEAD_DIM
        inv = []
        for i in range(GROUP):
            sink = sinks_ref[GROUP * g + i]
            q0 = i * WINDOW
            s_meta = st_ref[u, 0:N_META, q0:q0 + WINDOW]
            s_prev = st_ref[u, N_META:N_META + WINDOW, q0:q0 + WINDOW]
            s_cur = st_ref[u, N_META + WINDOW:N_META + 2 * WINDOW, q0:q0 + WINDOW]
            if blk == 0:
                s_prev = jnp.where(first_prev_ok, s_prev, NEG_INF)
            s_band = jnp.where(from_prev, s_prev, s_cur)
            m = jnp.maximum(jnp.max(s_band, axis=0, keepdims=True), jnp.max(s_meta, axis=0, keepdims=True))
            m = jnp.maximum(m, sink)
            p_band = jnp.exp(s_band - m)
            p_meta = jnp.exp(s_meta - m)
            den = (jnp.sum(p_band, axis=0, keepdims=True) + jnp.sum(p_meta, axis=0, keepdims=True)
                   + jnp.exp(sink - m))
            inv.append(1.0 / den)
            pt_ref[u, :, q0:q0 + WINDOW] = jnp.concatenate(
                [p_meta, jnp.where(from_prev, p_band, 0.0), jnp.where(from_prev, 0.0, p_band)],
                axis=0).astype(BF16)
        vt_prev = vtp_ref[0, d0:d0 + HEAD_DIM, :] if blk == 0 else vtc_ref[0, d0:d0 + HEAD_DIM, c0 - WINDOW:c0]
        vt_band = jnp.concatenate([vt_prev, vtc_ref[0, d0:d0 + HEAD_DIM, c0:c0 + WINDOW]], axis=1)
        og = (_mm(vt_band, pt_ref[u, N_META:, :])
              + _mm(vtm_ref[d0:d0 + HEAD_DIM, :], pt_ref[u, 0:N_META, :]))
        for i in range(GROUP):
            h0 = (GROUP * g + i) * HEAD_DIM
            ot_ref[h0:h0 + HEAD_DIM, c0:c0 + WINDOW] = og[:, i * WINDOW:(i + 1) * WINDOW] * inv[i]


def _swa(h, sinks, g_pre, g_post, wqt, w_o, cos_t, sin_t, k_pad, vt, k_meta, vt_meta, *, rows):
    b, length, _ = h.shape
    steps = length // rows
    per = rows // WINDOW
    n_units = min(rows, _SWA_PASS_ROWS) // WINDOW * N_KV_HEADS
    seq_spec = pl.BlockSpec((1, rows, D_MODEL), lambda i, j: (i, j, 0))
    rope_spec = pl.BlockSpec((ROT_HALF, rows), lambda i, j: (0, j))
    prev_blk = lambda j: jnp.maximum(j * per - 1, 0)
    in_specs = [
        pl.BlockSpec(memory_space=pltpu.SMEM),
        seq_spec, _const_spec((1, D_MODEL)), _const_spec((1, D_MODEL)),
        _const_spec((D_MODEL, D_MODEL)), _const_spec((D_MODEL, D_MODEL)),
        rope_spec, rope_spec,
        pl.BlockSpec((1, rows, K_PAD), lambda i, j: (i, j, 0)),
        pl.BlockSpec((1, WINDOW, K_PAD), lambda i, j: (i, prev_blk(j), 0)),
        pl.BlockSpec((1, KV_DIM, rows), lambda i, j: (i, 0, j)),
        pl.BlockSpec((1, KV_DIM, WINDOW), lambda i, j: (i, 0, prev_blk(j))),
        _const_spec((N_META, K_PAD)), _const_spec((KV_DIM, N_META)),
    ]
    return pl.pallas_call(
        functools.partial(_swa_kernel, rows=rows),
        grid=(b, steps),
        in_specs=in_specs,
        out_specs=seq_spec,
        out_shape=jax.ShapeDtypeStruct(h.shape, F32),
        scratch_shapes=[pltpu.VMEM((D_MODEL, rows), F32),
                        pltpu.VMEM((n_units, N_META + 2 * WINDOW, GROUP * WINDOW), F32),
                        pltpu.VMEM((n_units, N_META + 2 * WINDOW, GROUP * WINDOW), BF16)],
        compiler_params=pltpu.CompilerParams(dimension_semantics=("parallel", "parallel"),
                                             vmem_limit_bytes=_VMEM_LIMIT),
        name="swa",
    )(sinks, h, g_pre, g_post, wqt, w_o, cos_t, sin_t, k_pad, k_pad, vt, vt, k_meta, vt_meta)


def _rope_angles(positions):
    inv_freq = ROPE_THETA ** (-jnp.arange(0, ROT_DIM, 2, dtype=F32) / ROT_DIM)
    ang = positions.astype(F32)[:, None] * inv_freq[None, :]
    return jnp.cos(ang), jnp.sin(ang)


def _k_rope_tables(cos, sin):
    n = cos.shape[0]
    rest = LANES - ROT_DIM
    rc = jnp.concatenate([cos, cos, jnp.ones((n, rest), F32)], axis=1)
    ra = jnp.concatenate([-sin, jnp.zeros((n, ROT_HALF + rest), F32)], axis=1)
    rb = jnp.concatenate([jnp.zeros((n, ROT_HALF), F32), sin, jnp.zeros((n, rest), F32)], axis=1)
    return rc, ra, rb


def kernel(x, meta_tokens, norm_gains, w_ffn_in, w_ffn_out, gla_w_in, gla_w_gate, gla_b_gate, gla_norm, gla_w_out,
           kv_norm, w_kv, swa_w_q, swa_sinks, swa_w_out):
    batch, seq, _ = x.shape
    depth = norm_gains.shape[0]
    n_a = gla_w_in.shape[0]
    tm = 1024
    gla_rows = 1024
    swa_rows = 1024

    gains = norm_gains.reshape(depth, 6, 1, D_MODEL)
    gla_w_in_b = gla_w_in.astype(BF16)
    w_out_view = w_ffn_out.reshape(depth, 2, *_WOUT_VIEW)
    ffn_order = [(layer, which) for layer in range(depth) for which in range(2)]
    ffn_w = {ffn_order[0]: (w_ffn_in[0, 0].astype(BF16), w_ffn_out[0, 0].astype(BF16))}

    def ffn_main(h, layer, which, g_pre, g_post, **kw):
        pos = ffn_order.index((layer, which))
        nxt = ffn_order[pos + 1] if pos + 1 < len(ffn_order) else None
        cast = (w_ffn_in, w_out_view, nxt) if nxt is not None else None
        out = _ffn(h, g_pre, g_post, *ffn_w[(layer, which)], tm=tm, cast_next=cast, **kw)
        if nxt is None:
            return out
        *out, nxt_in, nxt_out = out
        ffn_w[nxt] = (nxt_in, nxt_out.reshape(D_FF, D_MODEL))
        return out if len(out) > 1 else out[0]

    h = x.reshape(batch * seq, D_MODEL)
    hm = jnp.pad(meta_tokens.astype(x.dtype), ((0, META_ROWS - N_META), (0, 0)))

    cos_m, sin_m = _rope_angles(jnp.arange(META_ROWS))
    cos_r, sin_r = _rope_angles(jnp.arange(N_META, N_META + seq))
    w_kv_b = w_kv.astype(BF16)
    wk = jnp.pad(w_kv_b[:, :KV_DIM].reshape(D_MODEL, N_KV_HEADS, HEAD_DIM),
                 ((0, 0), (0, 0), (0, LANES - HEAD_DIM))).reshape(D_MODEL, K_PAD)
    kv_w = (kv_norm.reshape(1, D_MODEL), wk, w_kv_b[:, KV_DIM:].T)
    kv_main = kv_w + _k_rope_tables(cos_r, sin_r)
    kv_meta = kv_w + _k_rope_tables(cos_m, sin_m)

    k_pad = vt = k_meta = vt_meta = None
    for layer in range(depth):
        g = gains[layer]
        last = layer == depth - 1
        h = ffn_main(h, layer, 0, g[0], g[1])
        if layer < n_a:
            a = layer
            hm = _ffn(hm, g[0], g[1], (w_ffn_in, (layer, 0)), (w_ffn_out, (layer, 0)), tm=META_ROWS)
            w_lr = jnp.pad(gla_w_in[a, :, GLA_IN_MAIN:], ((0, 0), (0, LANES - GLA_RANK)))
            w_gate_a = jnp.pad(gla_w_gate[a], ((0, LANES - GLA_RANK), (0, 0)))
            gla_vecs = (gla_b_gate[a].reshape(1, GLA_QK), gla_norm[a].reshape(1, GLA_DV))
            s_zero = jnp.zeros((GLA_HEADS, GLA_DV, GLA_DK), F32)
            hm3, s_meta = _gla(hm[None], s_zero, g[2], g[3], gla_w_in, a, w_lr, w_gate_a, *gla_vecs, gla_w_out[a],
                               rows=META_ROWS, valid_rows=N_META, emit_state=True)
            hm = hm3[0]
            h = _gla(h.reshape(batch, seq, D_MODEL), s_meta[0], g[2], g[3], gla_w_in_b, a, w_lr.astype(BF16),
                     w_gate_a.astype(BF16), *gla_vecs, gla_w_out[a].astype(BF16), rows=gla_rows)[0]
            h = h.reshape(batch * seq, D_MODEL)
        else:
            b = layer - n_a
            wqt = (swa_w_q[b].astype(BF16) * (HEAD_DIM ** -0.5)).T
            h = _swa(h.reshape(batch, seq, D_MODEL), swa_sinks[b], g[2], g[3], wqt, swa_w_out[b].astype(BF16),
                     cos_r.T, sin_r.T, k_pad.reshape(batch, seq, K_PAD), vt, k_meta, vt_meta, rows=swa_rows)
            h = h.reshape(batch * seq, D_MODEL)
        if layer == n_a - 1:
            h, k_pad, vt = ffn_main(h, layer, 1, g[4], g[5], kv=kv_main, tiles_per_seq=seq // tm)
            _, k_m, vt_m = _ffn(hm, g[4], g[5], *ffn_w[(layer, 1)], tm=META_ROWS, kv=kv_meta)
            k_meta = k_m[:N_META]
            vt_meta = vt_m[0, :, :N_META]
        else:
            h = ffn_main(h, layer, 1, g[4], g[5])
            if not last and layer < n_a:
                hm = _ffn(hm, g[4], g[5], (w_ffn_in, (layer, 1)), (w_ffn_out, (layer, 1)), tm=META_ROWS)
    return h.reshape(batch, seq, D_MODEL)
```

```python
import functools

import jax
import jax.numpy as jnp
from jax import lax
from jax.experimental import pallas as pl
from jax.experimental.pallas import tpu as pltpu

F32 = jnp.float32
BF16 = jnp.bfloat16

D_MODEL = 1024
D_FF = 2816
N_META = 16
EPS = 1e-6
NEG_INF = -1e30

GLA_HEADS = 4
GLA_DK = 128
GLA_DV = 256
GLA_QK = GLA_HEADS * GLA_DK
GLA_V = GLA_HEADS * GLA_DV
GLA_RANK = 16
GLA_TAU = 16.0
GLA_CHUNK = 64
GLA_IN_MAIN = 2 * GLA_QK + 2 * GLA_V
GLA_ATT_ROWS = 256

N_Q_HEADS = 16
N_KV_HEADS = 4
GROUP = N_Q_HEADS // N_KV_HEADS
HEAD_DIM = 64
WINDOW = 128
ROT_DIM = HEAD_DIM // 4
ROT_HALF = ROT_DIM // 2
ROPE_THETA = 500000.0
KV_DIM = N_KV_HEADS * HEAD_DIM
LANES = 128
K_PAD = N_KV_HEADS * LANES

META_ROWS = 64

_VMEM_LIMIT = 56 * 1024 * 1024

_FF_CHUNKS = ((0, 768), (768, 1536), (1536, 2304), (2304, 2816))
_FFN_SUB_ROWS = 512
_BF16_SUBLANES = 16
_SWA_PASS_ROWS = 512

_NT = (((1,), (1,)), ((), ()))
_TN = (((0,), (0,)), ((), ()))


def _rmsnorm(x, g):
    return x * lax.rsqrt(jnp.mean(x * x, axis=-1, keepdims=True) + EPS) * g


def _sigmoid(x):
    return 1.0 / (1.0 + jnp.exp(-x))


def _mm(a, b, dims=None):
    precision = lax.Precision.HIGHEST if a.dtype == F32 else None
    if dims is None:
        return jnp.dot(a, b, preferred_element_type=F32, precision=precision)
    return lax.dot_general(a, b, dims, preferred_element_type=F32, precision=precision)


def _const_spec(shape):
    zeros = (0,) * len(shape)
    return pl.BlockSpec(shape, lambda *_: zeros, pipeline_mode=pl.Buffered(1))


def _rope_rows(xt, c, s, n_heads):
    parts = []
    for h in range(n_heads):
        base = h * HEAD_DIM
        x1 = xt[base:base + ROT_HALF]
        x2 = xt[base + ROT_HALF:base + ROT_DIM]
        parts += [x1 * c - x2 * s, x2 * c + x1 * s, xt[base + ROT_DIM:base + HEAD_DIM]]
    return jnp.concatenate(parts, axis=0)


def _ffn_kernel(*refs, with_kv, with_cast):
    refs = list(refs)
    act_ref = refs.pop()
    if with_cast:
        cast_dst = [refs.pop(), refs.pop()][::-1]
    if with_kv:
        k_ref, vt_ref = refs[-2:]
        del refs[-2:]
    o_ref = refs.pop()
    if with_cast == "next":
        cast_src = [refs.pop(), refs.pop()][::-1]
    if with_kv:
        gkv_ref, wk_ref, wvt_ref, rc_ref, ra_ref, rb_ref = refs[-6:]
        del refs[-6:]
    h_ref, gpre_ref, gpost_ref, win_ref, wout_ref = refs
    if with_cast == "self":
        cast_src = [win_ref, wout_ref]
    if with_cast:
        for src, dst in zip(cast_src, cast_dst):
            dst[...] = src[...].astype(dst.dtype)
    mxu = win_ref.dtype
    tm = h_ref.shape[0]
    n_sub = 2 if tm >= 2 * _FFN_SUB_ROWS else 1
    sub = tm // n_sub
    for blk in range(n_sub):
        rows = slice(blk * sub, (blk + 1) * sub)
        x = h_ref[rows, :]
        xn = _rmsnorm(x, gpre_ref[...]).astype(mxu)
        for lo, hi in _FF_CHUNKS:
            gate = _mm(xn, win_ref[:, lo:hi])
            up = _mm(xn, win_ref[:, D_FF + lo:D_FF + hi])
            act_ref[rows, lo:hi] = (gate * _sigmoid(gate) * up).astype(mxu)
        y = _mm(act_ref[rows, :], wout_ref[...])
        o_ref[rows, :] = x + 0.5 * _rmsnorm(y, gpost_ref[...])
    if with_kv:
        h_new = o_ref[...]
        kn = _rmsnorm(h_new, gkv_ref[...]).astype(mxu)
        k = _mm(kn, wk_ref[...])
        rc = rc_ref[...]
        ra = ra_ref[...]
        rb = rb_ref[...]
        cols = []
        for g in range(N_KV_HEADS):
            kg = k[:, g * LANES:(g + 1) * LANES]
            cols.append(kg * rc + pltpu.roll(kg, LANES - ROT_HALF, axis=1) * ra
                        + pltpu.roll(kg, ROT_HALF, axis=1) * rb)
        k_ref[...] = jnp.concatenate(cols, axis=1).astype(k_ref.dtype)
        vt_ref[0] = _mm(wvt_ref[...], kn, _NT).astype(vt_ref.dtype)


def _weight_arg(w, shape):
    if isinstance(w, tuple):
        stack, sel = w
        return stack, pl.BlockSpec((None, None) + shape, lambda *_: (sel[0], sel[1], 0, 0),
                                   pipeline_mode=pl.Buffered(1))
    return w, _const_spec(shape)


def _ffn(h, g_pre, g_post, w_in, w_out, *, tm, kv=None, tiles_per_seq=1, cast_next=None, cast_self=False):
    t = h.shape[0]
    n = t // tm
    row_spec = pl.BlockSpec((tm, D_MODEL), lambda i: (i, 0))
    w_in, w_in_spec = _weight_arg(w_in, (D_MODEL, 2 * D_FF))
    w_out, w_out_spec = _weight_arg(w_out, (D_FF, D_MODEL))
    in_specs = [row_spec, _const_spec((1, D_MODEL)), _const_spec((1, D_MODEL)), w_in_spec, w_out_spec]
    args = [h, g_pre, g_post, w_in, w_out]
    out_shape = [jax.ShapeDtypeStruct((t, D_MODEL), F32)]
    out_specs = [row_spec]
    if kv is not None:
        g_kv, wk, wvt, rc, ra, rb = kv
        rope_spec = pl.BlockSpec((tm, LANES), lambda i: (i % tiles_per_seq, 0))
        in_specs += [_const_spec((1, D_MODEL)), _const_spec((D_MODEL, K_PAD)), _const_spec((KV_DIM, D_MODEL)),
                     rope_spec, rope_spec, rope_spec]
        args += [g_kv, wk, wvt, rc, ra, rb]
        n_seq = n // tiles_per_seq
        out_shape += [jax.ShapeDtypeStruct((t, K_PAD), BF16),
                      jax.ShapeDtypeStruct((n_seq, KV_DIM, tiles_per_seq * tm), BF16)]
        out_specs += [pl.BlockSpec((tm, K_PAD), lambda i: (i, 0)),
                      pl.BlockSpec((1, KV_DIM, tm), lambda i: (i // tiles_per_seq, 0, i % tiles_per_seq))]
    if cast_next is not None:
        nxt_in, nxt_out, (lyr, which) = cast_next
        for src in (nxt_in, nxt_out):
            shape = src.shape[2:]
            per = 1
            while (shape[0] * per // n) % _BF16_SUBLANES:
                per *= 2
            blk = (shape[0] * per // n, shape[1])
            in_specs.append(pl.BlockSpec((None, None) + blk, lambda i, per=per: (lyr, which, i // per, 0)))
            args.append(src)
            out_shape.append(jax.ShapeDtypeStruct(shape, BF16))
            out_specs.append(pl.BlockSpec(blk, lambda i, per=per: (i // per, 0)))
    if cast_self:
        assert n == 1 and cast_next is None
        for shape in ((D_MODEL, 2 * D_FF), (D_FF, D_MODEL)):
            out_shape.append(jax.ShapeDtypeStruct(shape, BF16))
            out_specs.append(pl.BlockSpec(shape, lambda i: (0, 0)))
    with_cast = "next" if cast_next is not None else "self" if cast_self else None
    out = pl.pallas_call(
        functools.partial(_ffn_kernel, with_kv=kv is not None, with_cast=with_cast),
        grid=(n,),
        in_specs=in_specs,
        out_specs=out_specs,
        out_shape=out_shape,
        scratch_shapes=[pltpu.VMEM((tm, D_FF), w_in.dtype)],
        compiler_params=pltpu.CompilerParams(dimension_semantics=("arbitrary",), vmem_limit_bytes=_VMEM_LIMIT),
        name="ffn_kv" if kv is not None else "ffn",
    )(*args)
    return out if len(out) > 1 else out[0]


def _gla_kernel(*refs, rows, valid_rows, emit_state):
    if emit_state:
        (h_ref, s0_ref, gpre_ref, gpost_ref, win_ref, wlr_ref, wgate_ref, bgate_ref, ghead_ref, wout_ref,
         o_ref, sfin_ref, st_ref) = refs
    else:
        (h_ref, s0_ref, gpre_ref, gpost_ref, win_ref, wlr_ref, wgate_ref, bgate_ref, ghead_ref, wout_ref,
         o_ref, st_ref) = refs
    step = pl.program_id(1)
    n_chunks = rows // GLA_CHUNK
    blk_rows = min(rows, GLA_ATT_ROWS)
    bsl = [slice(b * blk_rows, (b + 1) * blk_rows) for b in range(rows // blk_rows)]

    @pl.when(step == 0)
    def _():
        st_ref[...] = s0_ref[...]

    mxu = win_ref.dtype
    x = h_ref[0]
    hn = _rmsnorm(x, gpre_ref[...]).astype(mxu)
    lr = _mm(hn, wlr_ref[...])
    gp = _mm(lr.astype(mxu), wgate_ref[...]) + bgate_ref[...]
    gk = (jnp.minimum(gp, 0.0) - jnp.log1p(jnp.exp(-jnp.abs(gp)))) * (1.0 / GLA_TAU)
    q = _mm(hn, win_ref[:, 0:GLA_QK])
    k = _mm(hn, win_ref[:, GLA_QK:2 * GLA_QK])
    if valid_rows is not None:
        live = lax.broadcasted_iota(jnp.int32, (rows, 1), 0) < valid_rows
        gk = jnp.where(live, gk, 0.0)
        k = jnp.where(live, k, 0.0)

    ri = lax.broadcasted_iota(jnp.int32, (blk_rows, blk_rows), 0)
    ci = lax.broadcasted_iota(jnp.int32, (blk_rows, blk_rows), 1)
    intra = ((ri // GLA_CHUNK) == (ci // GLA_CHUNK)) & (ci <= ri)
    tri = jnp.where(intra, 1.0, 0.0).astype(mxu)
    g_hi = gk.astype(mxu)
    rem = gk - g_hi.astype(F32)
    g_mid = rem.astype(mxu)
    g_lo = (rem - g_mid.astype(F32)).astype(mxu)
    g_split = jnp.concatenate([g_hi, g_mid, g_lo], axis=1)
    parts = jnp.concatenate([_mm(tri, g_split[rs]) for rs in bsl], axis=0)
    bcum = parts[:, 0:GLA_QK] + parts[:, GLA_QK:2 * GLA_QK] + parts[:, 2 * GLA_QK:3 * GLA_QK]
    btot = jnp.concatenate(
        [jnp.broadcast_to(bcum[(c + 1) * GLA_CHUNK - 1:(c + 1) * GLA_CHUNK], (GLA_CHUNK, GLA_QK))
         for c in range(n_chunks)], axis=0)

    v = _mm(hn, win_ref[:, 2 * GLA_QK:2 * GLA_QK + GLA_V])
    r = _mm(hn, win_ref[:, 2 * GLA_QK + GLA_V:2 * GLA_QK + 2 * GLA_V])
    q_dec = (q * (GLA_DK ** -0.5) * jnp.exp(bcum)).astype(mxu)
    k_dec = (k * jnp.exp(-bcum)).astype(mxu)
    k_rem = (k * jnp.exp(btot - bcum)).astype(mxu)
    decay = jnp.exp(btot)
    vb = v.astype(mxu)

    heads = range(GLA_HEADS)
    ksl = [slice(hd * GLA_DK, (hd + 1) * GLA_DK) for hd in heads]
    vsl = [slice(hd * GLA_DV, (hd + 1) * GLA_DV) for hd in heads]
    csl = [slice(c * GLA_CHUNK, (c + 1) * GLA_CHUNK) for c in range(n_chunks)]
    att = [[_mm(q_dec[rs, ksl[hd]], k_dec[rs, ksl[hd]], _NT) for rs in bsl] for hd in heads]
    upd = [[_mm(vb[csl[c], vsl[hd]], k_rem[csl[c], ksl[hd]], _TN) for c in range(n_chunks)] for hd in heads]
    o_intra = [jnp.concatenate([_mm(jnp.where(intra, att[hd][b], 0.0).astype(mxu), vb[rs, vsl[hd]])
                                for b, rs in enumerate(bsl)], axis=0) for hd in heads]

    outs = []
    for hd in heads:
        st_t = st_ref[hd]
        o_head = []
        for c in range(n_chunks):
            o_head.append(o_intra[hd][csl[c]] + _mm(q_dec[csl[c], ksl[hd]], st_t.astype(mxu), _NT))
            st_t = st_t * decay[c * GLA_CHUNK:c * GLA_CHUNK + 1, ksl[hd]] + upd[hd][c]
        st_ref[hd] = st_t
        outs.append(jnp.concatenate(o_head, axis=0))

    g_head = ghead_ref[...]
    gated = []
    for hd in heads:
        rg = r[:, vsl[hd]]
        gated.append((_rmsnorm(outs[hd], g_head) * (rg * _sigmoid(rg))).astype(mxu))
    for rs in bsl:
        mix = _mm(gated[0][rs], wout_ref[vsl[0], :])
        for hd in heads[1:]:
            mix = mix + _mm(gated[hd][rs], wout_ref[vsl[hd], :])
        o_ref[0, rs, :] = h_ref[0, rs, :] + _rmsnorm(mix, gpost_ref[...])

    if emit_state:
        @pl.when(step == pl.num_programs(1) - 1)
        def _():
            sfin_ref[0] = st_ref[...]


def _gla(h, s0, g_pre, g_post, w_in, layer, w_lr, w_gate, b_gate, g_head, w_out, *, rows, valid_rows=None,
         emit_state=False):
    b, length, _ = h.shape
    steps = length // rows
    seq_spec = pl.BlockSpec((1, rows, D_MODEL), lambda i, j: (i, j, 0))
    state_shape = (GLA_HEADS, GLA_DV, GLA_DK)
    in_specs = [seq_spec, _const_spec(state_shape), _const_spec((1, D_MODEL)), _const_spec((1, D_MODEL)),
                pl.BlockSpec((None, D_MODEL, GLA_IN_MAIN), lambda i, j: (layer, 0, 0), pipeline_mode=pl.Buffered(1)),
                _const_spec((D_MODEL, LANES)), _const_spec((LANES, GLA_QK)), _const_spec((1, GLA_QK)),
                _const_spec((1, GLA_DV)), _const_spec((GLA_V, D_MODEL))]
    out_shape = [jax.ShapeDtypeStruct(h.shape, F32)]
    out_specs = [seq_spec]
    if emit_state:
        out_shape.append(jax.ShapeDtypeStruct((b,) + state_shape, F32))
        out_specs.append(pl.BlockSpec((1,) + state_shape, lambda i, j: (i, 0, 0, 0)))
    return pl.pallas_call(
        functools.partial(_gla_kernel, rows=rows, valid_rows=valid_rows, emit_state=emit_state),
        grid=(b, steps),
        in_specs=in_specs,
        out_specs=out_specs,
        out_shape=out_shape,
        scratch_shapes=[pltpu.VMEM(state_shape, F32)],
        compiler_params=pltpu.CompilerParams(dimension_semantics=("parallel", "arbitrary"),
                                             vmem_limit_bytes=_VMEM_LIMIT),
        name="gla_meta" if emit_state else "gla",
    )(h, s0, g_pre, g_post, w_in, w_lr, w_gate, b_gate, g_head, w_out)


def _swa_kernel(sinks_ref, h_ref, gpre_ref, gpost_ref, wqt_ref, wo_ref, cos_ref, sin_ref,
                kc_ref, kp_ref, vtc_ref, vtp_ref, km_ref, vtm_ref, o_ref, ot_ref, st_ref, pt_ref, *, rows):
    step = pl.program_id(1)
    x = h_ref[0]
    hn = _rmsnorm(x, gpre_ref[...]).astype(BF16)
    qt = _mm(wqt_ref[...], hn, _NT)
    qtb = _rope_rows(qt, cos_ref[...], sin_ref[...], N_Q_HEADS).astype(BF16)

    key_j = lax.broadcasted_iota(jnp.int32, (WINDOW, WINDOW), 0)
    qry_i = lax.broadcasted_iota(jnp.int32, (WINDOW, WINDOW), 1)
    from_prev = key_j > qry_i
    first_prev_ok = key_j > qry_i + jnp.where(step > 0, 0, WINDOW)
    per_pass = min(rows, _SWA_PASS_ROWS) // WINDOW
    for first_blk in range(0, rows // WINDOW, per_pass):
        _swa_pass(first_blk, per_pass, sinks_ref, qtb, from_prev, first_prev_ok, kc_ref, kp_ref, vtc_ref, vtp_ref,
                  km_ref, vtm_ref, ot_ref, st_ref, pt_ref)

    mix = _mm(ot_ref[...].astype(BF16), wo_ref[...], _TN)
    o_ref[0] = x + _rmsnorm(mix, gpost_ref[...])


def _swa_pass(first_blk, n_blk, sinks_ref, qtb, from_prev, first_prev_ok, kc_ref, kp_ref, vtc_ref, vtp_ref,
              km_ref, vtm_ref, ot_ref, st_ref, pt_ref):
    units = [(blk, g) for blk in range(first_blk, first_blk + n_blk) for g in range(N_KV_HEADS)]

    for u, (blk, g) in enumerate(units):
        c0 = blk * WINDOW
        l0 = g * LANES
        k_prev = kp_ref[0, :, l0:l0 + HEAD_DIM] if blk == 0 else kc_ref[0, c0 - WINDOW:c0, l0:l0 + HEAD_DIM]
        kcat = jnp.concatenate([km_ref[:, l0:l0 + HEAD_DIM], k_prev,
                                kc_ref[0, c0:c0 + WINDOW, l0:l0 + HEAD_DIM]], axis=0)
        qg = jnp.concatenate([qtb[(GROUP * g + i) * HEAD_DIM:(GROUP * g + i + 1) * HEAD_DIM, c0:c0 + WINDOW]
                              for i in range(GROUP)], axis=1)
        st_ref[u] = _mm(kcat, qg)

    for u, (blk, g) in enumerate(units):
        c0 = blk * WINDOW
        d0 = g * HEAD_DIM
        inv = []
        for i in range(GROUP):
            sink = sinks_ref[GROUP * g + i]
            q0 = i * WINDOW
            s_meta = st_ref[u, 0:N_META, q0:q0 + WINDOW]
            s_prev = st_ref[u, N_META:N_META + WINDOW, q0:q0 + WINDOW]
            s_cur = st_ref[u, N_META + WINDOW:N_META + 2 * WINDOW, q0:q0 + WINDOW]
            if blk == 0:
                s_prev = jnp.where(first_prev_ok, s_prev, NEG_INF)
            s_band = jnp.where(from_prev, s_prev, s_cur)
            m = jnp.maximum(jnp.max(s_band, axis=0, keepdims=True), jnp.max(s_meta, axis=0, keepdims=True))
            m = jnp.maximum(m, sink)
            p_band = jnp.exp(s_band - m)
            p_meta = jnp.exp(s_meta - m)
            den = (jnp.sum(p_band, axis=0, keepdims=True) + jnp.sum(p_meta, axis=0, keepdims=True)
                   + jnp.exp(sink - m))
            inv.append(1.0 / den)
            pt_ref[u, :, q0:q0 + WINDOW] = jnp.concatenate(
                [p_meta, jnp.where(from_prev, p_band, 0.0), jnp.where(from_prev, 0.0, p_band)],
                axis=0).astype(BF16)
        vt_prev = vtp_ref[0, d0:d0 + HEAD_DIM, :] if blk == 0 else vtc_ref[0, d0:d0 + HEAD_DIM, c0 - WINDOW:c0]
        vt_band = jnp.concatenate([vt_prev, vtc_ref[0, d0:d0 + HEAD_DIM, c0:c0 + WINDOW]], axis=1)
        og = (_mm(vt_band, pt_ref[u, N_META:, :])
              + _mm(vtm_ref[d0:d0 + HEAD_DIM, :], pt_ref[u, 0:N_META, :]))
        for i in range(GROUP):
            h0 = (GROUP * g + i) * HEAD_DIM
            ot_ref[h0:h0 + HEAD_DIM, c0:c0 + WINDOW] = og[:, i * WINDOW:(i + 1) * WINDOW] * inv[i]


def _swa(h, sinks, g_pre, g_post, wqt, w_o, cos_t, sin_t, k_pad, vt, k_meta, vt_meta, *, rows):
    b, length, _ = h.shape
    steps = length // rows
    per = rows // WINDOW
    n_units = min(rows, _SWA_PASS_ROWS) // WINDOW * N_KV_HEADS
    seq_spec = pl.BlockSpec((1, rows, D_MODEL), lambda i, j: (i, j, 0))
    rope_spec = pl.BlockSpec((ROT_HALF, rows), lambda i, j: (0, j))
    prev_blk = lambda j: jnp.maximum(j * per - 1, 0)
    in_specs = [
        pl.BlockSpec(memory_space=pltpu.SMEM),
        seq_spec, _const_spec((1, D_MODEL)), _const_spec((1, D_MODEL)),
        _const_spec((D_MODEL, D_MODEL)), _const_spec((D_MODEL, D_MODEL)),
        rope_spec, rope_spec,
        pl.BlockSpec((1, rows, K_PAD), lambda i, j: (i, j, 0)),
        pl.BlockSpec((1, WINDOW, K_PAD), lambda i, j: (i, prev_blk(j), 0)),
        pl.BlockSpec((1, KV_DIM, rows), lambda i, j: (i, 0, j)),
        pl.BlockSpec((1, KV_DIM, WINDOW), lambda i, j: (i, 0, prev_blk(j))),
        _const_spec((N_META, K_PAD)), _const_spec((KV_DIM, N_META)),
    ]
    return pl.pallas_call(
        functools.partial(_swa_kernel, rows=rows),
        grid=(b, steps),
        in_specs=in_specs,
        out_specs=seq_spec,
        out_shape=jax.ShapeDtypeStruct(h.shape, F32),
        scratch_shapes=[pltpu.VMEM((D_MODEL, rows), F32),
                        pltpu.VMEM((n_units, N_META + 2 * WINDOW, GROUP * WINDOW), F32),
                        pltpu.VMEM((n_units, N_META + 2 * WINDOW, GROUP * WINDOW), BF16)],
        compiler_params=pltpu.CompilerParams(dimension_semantics=("parallel", "parallel"),
                                             vmem_limit_bytes=_VMEM_LIMIT),
        name="swa",
    )(sinks, h, g_pre, g_post, wqt, w_o, cos_t, sin_t, k_pad, k_pad, vt, vt, k_meta, vt_meta)


def _rope_angles(positions):
    inv_freq = ROPE_THETA ** (-jnp.arange(0, ROT_DIM, 2, dtype=F32) / ROT_DIM)
    ang = positions.astype(F32)[:, None] * inv_freq[None, :]
    return jnp.cos(ang), jnp.sin(ang)


def _k_rope_tables(cos, sin):
    n = cos.shape[0]
    rest = LANES - ROT_DIM
    rc = jnp.concatenate([cos, cos, jnp.ones((n, rest), F32)], axis=1)
    ra = jnp.concatenate([-sin, jnp.zeros((n, ROT_HALF + rest), F32)], axis=1)
    rb = jnp.concatenate([jnp.zeros((n, ROT_HALF), F32), sin, jnp.zeros((n, rest), F32)], axis=1)
    return rc, ra, rb


def kernel(x, meta_tokens, norm_gains, w_ffn_in, w_ffn_out, gla_w_in, gla_w_gate, gla_b_gate, gla_norm, gla_w_out,
           kv_norm, w_kv, swa_w_q, swa_sinks, swa_w_out):
    batch, seq, _ = x.shape
    depth = norm_gains.shape[0]
    n_a = gla_w_in.shape[0]
    tm = 1024
    gla_rows = 1024
    swa_rows = 1024

    gains = norm_gains.reshape(depth, 6, 1, D_MODEL)
    gla_w_in_b = gla_w_in.astype(BF16)
    ffn_order = [(layer, which) for layer in range(depth) for which in range(2)]
    ffn_w = {}

    def ffn_main(h, layer, which, g_pre, g_post, **kw):
        pos = ffn_order.index((layer, which))
        nxt = ffn_order[pos + 1] if pos + 1 < len(ffn_order) else None
        cast = (w_ffn_in, w_ffn_out, nxt) if nxt is not None else None
        out = _ffn(h, g_pre, g_post, *ffn_w[(layer, which)], tm=tm, cast_next=cast, **kw)
        if nxt is None:
            return out
        *out, nxt_in, nxt_out = out
        ffn_w[nxt] = (nxt_in, nxt_out)
        return out if len(out) > 1 else out[0]

    h = x.reshape(batch * seq, D_MODEL)
    hm = jnp.pad(meta_tokens.astype(x.dtype), ((0, META_ROWS - N_META), (0, 0)))

    cos_m, sin_m = _rope_angles(jnp.arange(META_ROWS))
    cos_r, sin_r = _rope_angles(jnp.arange(N_META, N_META + seq))
    w_kv_b = w_kv.astype(BF16)
    wk = jnp.pad(w_kv_b[:, :KV_DIM].reshape(D_MODEL, N_KV_HEADS, HEAD_DIM),
                 ((0, 0), (0, 0), (0, LANES - HEAD_DIM))).reshape(D_MODEL, K_PAD)
    kv_w = (kv_norm.reshape(1, D_MODEL), wk, w_kv_b[:, KV_DIM:].T)
    kv_main = kv_w + _k_rope_tables(cos_r, sin_r)
    kv_meta = kv_w + _k_rope_tables(cos_m, sin_m)

    k_pad = vt = k_meta = vt_meta = None
    for layer in range(depth):
        g = gains[layer]
        last = layer == depth - 1
        if layer < n_a:
            hm = _ffn(hm, g[0], g[1], (w_ffn_in, (layer, 0)), (w_ffn_out, (layer, 0)), tm=META_ROWS,
                      cast_self=layer == 0)
            if layer == 0:
                hm, *ffn_w[(0, 0)] = hm
        if (layer, 0) not in ffn_w:
            ffn_w[(layer, 0)] = (w_ffn_in[layer, 0].astype(BF16), w_ffn_out[layer, 0].astype(BF16))
        h = ffn_main(h, layer, 0, g[0], g[1])
        if layer < n_a:
            a = layer
            w_lr = jnp.pad(gla_w_in[a, :, GLA_IN_MAIN:], ((0, 0), (0, LANES - GLA_RANK)))
            w_gate_a = jnp.pad(gla_w_gate[a], ((0, LANES - GLA_RANK), (0, 0)))
            gla_vecs = (gla_b_gate[a].reshape(1, GLA_QK), gla_norm[a].reshape(1, GLA_DV))
            s_zero = jnp.zeros((GLA_HEADS, GLA_DV, GLA_DK), F32)
            hm3, s_meta = _gla(hm[None], s_zero, g[2], g[3], gla_w_in, a, w_lr, w_gate_a, *gla_vecs, gla_w_out[a],
                               rows=META_ROWS, valid_rows=N_META, emit_state=True)
            hm = hm3[0]
            h = _gla(h.reshape(batch, seq, D_MODEL), s_meta[0], g[2], g[3], gla_w_in_b, a, w_lr.astype(BF16),
                     w_gate_a.astype(BF16), *gla_vecs, gla_w_out[a].astype(BF16), rows=gla_rows)[0]
            h = h.reshape(batch * seq, D_MODEL)
        else:
            b = layer - n_a
            wqt = (swa_w_q[b].astype(BF16) * (HEAD_DIM ** -0.5)).T
            h = _swa(h.reshape(batch, seq, D_MODEL), swa_sinks[b], g[2], g[3], wqt, swa_w_out[b].astype(BF16),
                     cos_r.T, sin_r.T, k_pad.reshape(batch, seq, K_PAD), vt, k_meta, vt_meta, rows=swa_rows)
            h = h.reshape(batch * seq, D_MODEL)
        if layer == n_a - 1:
            h, k_pad, vt = ffn_main(h, layer, 1, g[4], g[5], kv=kv_main, tiles_per_seq=seq // tm)
            _, k_m, vt_m = _ffn(hm, g[4], g[5], *ffn_w[(layer, 1)], tm=META_ROWS, kv=kv_meta)
            k_meta = k_m[:N_META]
            vt_meta = vt_m[0, :, :N_META]
        else:
            h = ffn_main(h, layer, 1, g[4], g[5])
            if not last and layer < n_a:
                hm = _ffn(hm, g[4], g[5], (w_ffn_in, (layer, 1)), (w_ffn_out, (layer, 1)), tm=META_ROWS)
    return h.reshape(batch, seq, D_MODEL)
```

```python
import functools

import jax
import jax.numpy as jnp
from jax import lax
from jax.experimental import pallas as pl
from jax.experimental.pallas import tpu as pltpu

F32 = jnp.float32
BF16 = jnp.bfloat16

D_MODEL = 1024
D_FF = 2816
N_META = 16
EPS = 1e-6
NEG_INF = -1e30

GLA_HEADS = 4
GLA_DK = 128
GLA_DV = 256
GLA_QK = GLA_HEADS * GLA_DK
GLA_V = GLA_HEADS * GLA_DV
GLA_RANK = 16
GLA_TAU = 16.0
GLA_CHUNK = 64
GLA_IN_MAIN = 2 * GLA_QK + 2 * GLA_V
GLA_ATT_ROWS = 256

N_Q_HEADS = 16
N_KV_HEADS = 4
GROUP = N_Q_HEADS // N_KV_HEADS
HEAD_DIM = 64
WINDOW = 128
ROT_DIM = HEAD_DIM // 4
ROT_HALF = ROT_DIM // 2
ROPE_THETA = 500000.0
KV_DIM = N_KV_HEADS * HEAD_DIM
LANES = 128
K_PAD = N_KV_HEADS * LANES

META_ROWS = 64

_VMEM_LIMIT = 56 * 1024 * 1024

_FF_CHUNKS = ((0, 768), (768, 1536), (1536, 2304), (2304, 2816))
_FFN_SUB_ROWS = 512
_BF16_SUBLANES = 16
_SWA_PASS_ROWS = 512

_NT = (((1,), (1,)), ((), ()))
_TN = (((0,), (0,)), ((), ()))


def _rmsnorm(x, g):
    return x * lax.rsqrt(jnp.mean(x * x, axis=-1, keepdims=True) + EPS) * g


def _sigmoid(x):
    return 1.0 / (1.0 + jnp.exp(-x))


def _mm(a, b, dims=None):
    precision = lax.Precision.HIGHEST if a.dtype == F32 else None
    if dims is None:
        return jnp.dot(a, b, preferred_element_type=F32, precision=precision)
    return lax.dot_general(a, b, dims, preferred_element_type=F32, precision=precision)


def _const_spec(shape):
    zeros = (0,) * len(shape)
    return pl.BlockSpec(shape, lambda *_: zeros, pipeline_mode=pl.Buffered(1))


def _rope_rows(xt, c, s, n_heads):
    parts = []
    for h in range(n_heads):
        base = h * HEAD_DIM
        x1 = xt[base:base + ROT_HALF]
        x2 = xt[base + ROT_HALF:base + ROT_DIM]
        parts += [x1 * c - x2 * s, x2 * c + x1 * s, xt[base + ROT_DIM:base + HEAD_DIM]]
    return jnp.concatenate(parts, axis=0)


def _ffn_kernel(*refs, with_kv, cast_self, side_fns):
    refs = list(refs)
    act_ref = refs.pop()
    n_side = len(side_fns)
    side_dst = refs[len(refs) - n_side:]
    del refs[len(refs) - n_side:]
    if cast_self:
        self_dst = refs[-2:]
        del refs[-2:]
    if with_kv:
        k_ref, vt_ref = refs[-2:]
        del refs[-2:]
    o_ref = refs.pop()
    side_src = refs[len(refs) - n_side:]
    del refs[len(refs) - n_side:]
    if with_kv:
        gkv_ref, wk_ref, wvt_ref, rc_ref, ra_ref, rb_ref = refs[-6:]
        del refs[-6:]
    h_ref, gpre_ref, gpost_ref, win_ref, wout_ref = refs
    for fn, src, dst in zip(side_fns, side_src, side_dst):
        dst[...] = fn(src[...]).astype(dst.dtype)
    if cast_self:
        for src, dst in zip((win_ref, wout_ref), self_dst):
            dst[...] = src[...].astype(dst.dtype)
    mxu = win_ref.dtype
    tm = h_ref.shape[0]
    n_sub = 2 if tm >= 2 * _FFN_SUB_ROWS else 1
    sub = tm // n_sub
    for blk in range(n_sub):
        rows = slice(blk * sub, (blk + 1) * sub)
        x = h_ref[rows, :]
        xn = _rmsnorm(x, gpre_ref[...]).astype(mxu)
        for lo, hi in _FF_CHUNKS:
            gate = _mm(xn, win_ref[:, lo:hi])
            up = _mm(xn, win_ref[:, D_FF + lo:D_FF + hi])
            act_ref[rows, lo:hi] = (gate * _sigmoid(gate) * up).astype(mxu)
        y = _mm(act_ref[rows, :], wout_ref[...])
        o_ref[rows, :] = x + 0.5 * _rmsnorm(y, gpost_ref[...])
    if with_kv:
        h_new = o_ref[...]
        kn = _rmsnorm(h_new, gkv_ref[...]).astype(mxu)
        k = _mm(kn, wk_ref[...])
        rc = rc_ref[...]
        ra = ra_ref[...]
        rb = rb_ref[...]
        cols = []
        for g in range(N_KV_HEADS):
            kg = k[:, g * LANES:(g + 1) * LANES]
            cols.append(kg * rc + pltpu.roll(kg, LANES - ROT_HALF, axis=1) * ra
                        + pltpu.roll(kg, ROT_HALF, axis=1) * rb)
        k_ref[...] = jnp.concatenate(cols, axis=1).astype(k_ref.dtype)
        vt_ref[0] = _mm(wvt_ref[...], kn, _NT).astype(vt_ref.dtype)


def _weight_arg(w, shape):
    if isinstance(w, tuple):
        stack, sel = w
        return stack, pl.BlockSpec((None, None) + shape, lambda *_: (sel[0], sel[1], 0, 0),
                                   pipeline_mode=pl.Buffered(1))
    return w, _const_spec(shape)


def _cast_job(src, lead, n):
    shape = src.shape[len(lead):]
    per = 1
    while (shape[0] * per // n) % _BF16_SUBLANES:
        per *= 2
    blk = (shape[0] * per // n, shape[1])
    in_spec = pl.BlockSpec((None,) * len(lead) + blk, lambda i: tuple(lead) + (i // per, 0))
    return src, in_spec, jax.ShapeDtypeStruct(shape, BF16), pl.BlockSpec(blk, lambda i: (i // per, 0)), lambda x: x


def _transpose_job(src, lead, n, cols, scale=None):
    rows, width = src.shape[len(lead):]
    per = n * LANES // rows
    n_out = cols.stop - cols.start
    in_spec = pl.BlockSpec((None,) * len(lead) + (LANES, width), lambda i: tuple(lead) + (i // per, 0))
    out_spec = pl.BlockSpec((n_out, LANES), lambda i: (0, i // per))
    fn = lambda x: (x[:, cols] if scale is None else x[:, cols] * scale).T
    return src, in_spec, jax.ShapeDtypeStruct((n_out, rows), BF16), out_spec, fn


def _ffn(h, g_pre, g_post, w_in, w_out, *, tm, kv=None, tiles_per_seq=1, cast_next=None, cast_self=False,
         side=()):
    t = h.shape[0]
    n = t // tm
    row_spec = pl.BlockSpec((tm, D_MODEL), lambda i: (i, 0))
    w_in, w_in_spec = _weight_arg(w_in, (D_MODEL, 2 * D_FF))
    w_out, w_out_spec = _weight_arg(w_out, (D_FF, D_MODEL))
    in_specs = [row_spec, _const_spec((1, D_MODEL)), _const_spec((1, D_MODEL)), w_in_spec, w_out_spec]
    args = [h, g_pre, g_post, w_in, w_out]
    out_shape = [jax.ShapeDtypeStruct((t, D_MODEL), F32)]
    out_specs = [row_spec]
    if kv is not None:
        g_kv, wk, wvt, rc, ra, rb = kv
        rope_spec = pl.BlockSpec((tm, LANES), lambda i: (i % tiles_per_seq, 0))
        in_specs += [_const_spec((1, D_MODEL)), _const_spec((D_MODEL, K_PAD)), _const_spec((KV_DIM, D_MODEL)),
                     rope_spec, rope_spec, rope_spec]
        args += [g_kv, wk, wvt, rc, ra, rb]
        n_seq = n // tiles_per_seq
        out_shape += [jax.ShapeDtypeStruct((t, K_PAD), BF16),
                      jax.ShapeDtypeStruct((n_seq, KV_DIM, tiles_per_seq * tm), BF16)]
        out_specs += [pl.BlockSpec((tm, K_PAD), lambda i: (i, 0)),
                      pl.BlockSpec((1, KV_DIM, tm), lambda i: (i // tiles_per_seq, 0, i % tiles_per_seq))]
    if cast_self:
        assert n == 1
        for shape in ((D_MODEL, 2 * D_FF), (D_FF, D_MODEL)):
            out_shape.append(jax.ShapeDtypeStruct(shape, BF16))
            out_specs.append(pl.BlockSpec(shape, lambda i: (0, 0)))
    side = list(side)
    if cast_next is not None:
        nxt_in, nxt_out, sel = cast_next
        side += [_cast_job(nxt_in, sel, n), _cast_job(nxt_out, sel, n)]
    for src, in_spec, out_sds, out_spec, _ in side:
        args.append(src)
        in_specs.append(in_spec)
        out_shape.append(out_sds)
        out_specs.append(out_spec)
    out = pl.pallas_call(
        functools.partial(_ffn_kernel, with_kv=kv is not None, cast_self=cast_self,
                          side_fns=tuple(job[-1] for job in side)),
        grid=(n,),
        in_specs=in_specs,
        out_specs=out_specs,
        out_shape=out_shape,
        scratch_shapes=[pltpu.VMEM((tm, D_FF), w_in.dtype)],
        compiler_params=pltpu.CompilerParams(dimension_semantics=("arbitrary",), vmem_limit_bytes=_VMEM_LIMIT),
        name="ffn_kv" if kv is not None else "ffn",
    )(*args)
    return out if len(out) > 1 else out[0]


def _gla_kernel(*refs, rows, valid_rows, emit_state):
    if emit_state:
        (h_ref, s0_ref, gpre_ref, gpost_ref, win_ref, wlr_ref, wgate_ref, bgate_ref, ghead_ref, wout_ref,
         o_ref, sfin_ref, st_ref) = refs
    else:
        (h_ref, s0_ref, gpre_ref, gpost_ref, win_ref, wlr_ref, wgate_ref, bgate_ref, ghead_ref, wout_ref,
         o_ref, st_ref) = refs
    step = pl.program_id(1)
    n_chunks = rows // GLA_CHUNK
    blk_rows = min(rows, GLA_ATT_ROWS)
    bsl = [slice(b * blk_rows, (b + 1) * blk_rows) for b in range(rows // blk_rows)]

    @pl.when(step == 0)
    def _():
        st_ref[...] = s0_ref[...]

    mxu = win_ref.dtype
    x = h_ref[0]
    hn = _rmsnorm(x, gpre_ref[...]).astype(mxu)
    lr = _mm(hn, wlr_ref[...])
    gp = _mm(lr.astype(mxu), wgate_ref[...]) + bgate_ref[...]
    gk = (jnp.minimum(gp, 0.0) - jnp.log1p(jnp.exp(-jnp.abs(gp)))) * (1.0 / GLA_TAU)
    q = _mm(hn, win_ref[:, 0:GLA_QK])
    k = _mm(hn, win_ref[:, GLA_QK:2 * GLA_QK])
    if valid_rows is not None:
        live = lax.broadcasted_iota(jnp.int32, (rows, 1), 0) < valid_rows
        gk = jnp.where(live, gk, 0.0)
        k = jnp.where(live, k, 0.0)

    ri = lax.broadcasted_iota(jnp.int32, (blk_rows, blk_rows), 0)
    ci = lax.broadcasted_iota(jnp.int32, (blk_rows, blk_rows), 1)
    intra = ((ri // GLA_CHUNK) == (ci // GLA_CHUNK)) & (ci <= ri)
    tri = jnp.where(intra, 1.0, 0.0).astype(mxu)
    g_hi = gk.astype(mxu)
    rem = gk - g_hi.astype(F32)
    g_mid = rem.astype(mxu)
    g_lo = (rem - g_mid.astype(F32)).astype(mxu)
    g_split = jnp.concatenate([g_hi, g_mid, g_lo], axis=1)
    parts = jnp.concatenate([_mm(tri, g_split[rs]) for rs in bsl], axis=0)
    bcum = parts[:, 0:GLA_QK] + parts[:, GLA_QK:2 * GLA_QK] + parts[:, 2 * GLA_QK:3 * GLA_QK]
    btot = jnp.concatenate(
        [jnp.broadcast_to(bcum[(c + 1) * GLA_CHUNK - 1:(c + 1) * GLA_CHUNK], (GLA_CHUNK, GLA_QK))
         for c in range(n_chunks)], axis=0)

    v = _mm(hn, win_ref[:, 2 * GLA_QK:2 * GLA_QK + GLA_V])
    r = _mm(hn, win_ref[:, 2 * GLA_QK + GLA_V:2 * GLA_QK + 2 * GLA_V])
    q_dec = (q * (GLA_DK ** -0.5) * jnp.exp(bcum)).astype(mxu)
    k_dec = (k * jnp.exp(-bcum)).astype(mxu)
    k_rem = (k * jnp.exp(btot - bcum)).astype(mxu)
    vb = v.astype(mxu)
    decay = [jnp.exp(jnp.broadcast_to(bcum[(c + 1) * GLA_CHUNK - 1:(c + 1) * GLA_CHUNK], (LANES, GLA_QK)).T)
             for c in range(n_chunks)]

    heads = range(GLA_HEADS)
    ksl = [slice(hd * GLA_DK, (hd + 1) * GLA_DK) for hd in heads]
    vsl = [slice(hd * GLA_DV, (hd + 1) * GLA_DV) for hd in heads]
    csl = [slice(c * GLA_CHUNK, (c + 1) * GLA_CHUNK) for c in range(n_chunks)]
    att = [[_mm(q_dec[rs, ksl[hd]], k_dec[rs, ksl[hd]], _NT) for rs in bsl] for hd in heads]
    upd = [[_mm(k_rem[csl[c], ksl[hd]], vb[csl[c], vsl[hd]], _TN) for c in range(n_chunks)] for hd in heads]
    o_intra = [jnp.concatenate([_mm(jnp.where(intra, att[hd][b], 0.0).astype(mxu), vb[rs, vsl[hd]])
                                for b, rs in enumerate(bsl)], axis=0) for hd in heads]

    outs = []
    for hd in heads:
        st = st_ref[hd]
        o_head = []
        for c in range(n_chunks):
            o_head.append(o_intra[hd][csl[c]] + _mm(q_dec[csl[c], ksl[hd]], st.astype(mxu)))
            d = decay[c][ksl[hd], :]
            st = st * jnp.concatenate([d] * (GLA_DV // LANES), axis=1) + upd[hd][c]
        st_ref[hd] = st
        outs.append(jnp.concatenate(o_head, axis=0))

    g_head = ghead_ref[...]
    gated = []
    for hd in heads:
        rg = r[:, vsl[hd]]
        gated.append((_rmsnorm(outs[hd], g_head) * (rg * _sigmoid(rg))).astype(mxu))
    for rs in bsl:
        mix = _mm(gated[0][rs], wout_ref[vsl[0], :])
        for hd in heads[1:]:
            mix = mix + _mm(gated[hd][rs], wout_ref[vsl[hd], :])
        o_ref[0, rs, :] = h_ref[0, rs, :] + _rmsnorm(mix, gpost_ref[...])

    if emit_state:
        @pl.when(step == pl.num_programs(1) - 1)
        def _():
            sfin_ref[0] = st_ref[...]


def _gla(h, s0, g_pre, g_post, w_in, layer, w_lr, w_gate, b_gate, g_head, w_out, *, rows, valid_rows=None,
         emit_state=False):
    b, length, _ = h.shape
    steps = length // rows
    seq_spec = pl.BlockSpec((1, rows, D_MODEL), lambda i, j: (i, j, 0))
    state_shape = (GLA_HEADS, GLA_DK, GLA_DV)
    if w_in.ndim == 3:
        w_in_spec = pl.BlockSpec((None, D_MODEL, GLA_IN_MAIN), lambda i, j: (layer, 0, 0),
                                 pipeline_mode=pl.Buffered(1))
    else:
        w_in_spec = _const_spec((D_MODEL, GLA_IN_MAIN))
    in_specs = [seq_spec, _const_spec(state_shape), _const_spec((1, D_MODEL)), _const_spec((1, D_MODEL)),
                w_in_spec,
                _const_spec((D_MODEL, LANES)), _const_spec((LANES, GLA_QK)), _const_spec((1, GLA_QK)),
                _const_spec((1, GLA_DV)), _const_spec((GLA_V, D_MODEL))]
    out_shape = [jax.ShapeDtypeStruct(h.shape, F32)]
    out_specs = [seq_spec]
    if emit_state:
        out_shape.append(jax.ShapeDtypeStruct((b,) + state_shape, F32))
        out_specs.append(pl.BlockSpec((1,) + state_shape, lambda i, j: (i, 0, 0, 0)))
    return pl.pallas_call(
        functools.partial(_gla_kernel, rows=rows, valid_rows=valid_rows, emit_state=emit_state),
        grid=(b, steps),
        in_specs=in_specs,
        out_specs=out_specs,
        out_shape=out_shape,
        scratch_shapes=[pltpu.VMEM(state_shape, F32)],
        compiler_params=pltpu.CompilerParams(dimension_semantics=("parallel", "arbitrary"),
                                             vmem_limit_bytes=_VMEM_LIMIT),
        name="gla_meta" if emit_state else "gla",
    )(h, s0, g_pre, g_post, w_in, w_lr, w_gate, b_gate, g_head, w_out)


def _swa_kernel(sinks_ref, h_ref, gpre_ref, gpost_ref, wqt_ref, wo_ref, cos_ref, sin_ref,
                kc_ref, kp_ref, vtc_ref, vtp_ref, km_ref, vtm_ref, o_ref, ot_ref, st_ref, pt_ref, *, rows):
    step = pl.program_id(1)
    x = h_ref[0]
    hn = _rmsnorm(x, gpre_ref[...]).astype(BF16)
    qt = _mm(wqt_ref[...], hn, _NT)
    qtb = _rope_rows(qt, cos_ref[...], sin_ref[...], N_Q_HEADS).astype(BF16)

    key_j = lax.broadcasted_iota(jnp.int32, (WINDOW, WINDOW), 0)
    qry_i = lax.broadcasted_iota(jnp.int32, (WINDOW, WINDOW), 1)
    from_prev = key_j > qry_i
    first_prev_ok = key_j > qry_i + jnp.where(step > 0, 0, WINDOW)
    per_pass = min(rows, _SWA_PASS_ROWS) // WINDOW
    for first_blk in range(0, rows // WINDOW, per_pass):
        _swa_pass(first_blk, per_pass, sinks_ref, qtb, from_prev, first_prev_ok, kc_ref, kp_ref, vtc_ref, vtp_ref,
                  km_ref, vtm_ref, ot_ref, st_ref, pt_ref)

    mix = _mm(ot_ref[...].astype(BF16), wo_ref[...], _TN)
    o_ref[0] = x + _rmsnorm(mix, gpost_ref[...])


def _swa_pass(first_blk, n_blk, sinks_ref, qtb, from_prev, first_prev_ok, kc_ref, kp_ref, vtc_ref, vtp_ref,
              km_ref, vtm_ref, ot_ref, st_ref, pt_ref):
    units = [(blk, g) for blk in range(first_blk, first_blk + n_blk) for g in range(N_KV_HEADS)]

    for u, (blk, g) in enumerate(units):
        c0 = blk * WINDOW
        l0 = g * LANES
        k_prev = kp_ref[0, :, l0:l0 + HEAD_DIM] if blk == 0 else kc_ref[0, c0 - WINDOW:c0, l0:l0 + HEAD_DIM]
        kcat = jnp.concatenate([km_ref[:, l0:l0 + HEAD_DIM], k_prev,
                                kc_ref[0, c0:c0 + WINDOW, l0:l0 + HEAD_DIM]], axis=0)
        qg = jnp.concatenate([qtb[(GROUP * g + i) * HEAD_DIM:(GROUP * g + i + 1) * HEAD_DIM, c0:c0 + WINDOW]
                              for i in range(GROUP)], axis=1)
        st_ref[u] = _mm(kcat, qg)

    for u, (blk, g) in enumerate(units):
        c0 = blk * WINDOW
        d0 = g * HEAD_DIM
        inv = []
        for i in range(GROUP):
            sink = sinks_ref[GROUP * g + i]
            q0 = i * WINDOW
            s_meta = st_ref[u, 0:N_META, q0:q0 + WINDOW]
            s_prev = st_ref[u, N_META:N_META + WINDOW, q0:q0 + WINDOW]
            s_cur = st_ref[u, N_META + WINDOW:N_META + 2 * WINDOW, q0:q0 + WINDOW]
            if blk == 0:
                s_prev = jnp.where(first_prev_ok, s_prev, NEG_INF)
            s_band = jnp.where(from_prev, s_prev, s_cur)
            m = jnp.maximum(jnp.max(s_band, axis=0, keepdims=True), jnp.max(s_meta, axis=0, keepdims=True))
            m = jnp.maximum(m, sink)
            p_band = jnp.exp(s_band - m)
            p_meta = jnp.exp(s_meta - m)
            den = (jnp.sum(p_band, axis=0, keepdims=True) + jnp.sum(p_meta, axis=0, keepdims=True)
                   + jnp.exp(sink - m))
            inv.append(1.0 / den)
            pt_ref[u, :, q0:q0 + WINDOW] = jnp.concatenate(
                [p_meta, jnp.where(from_prev, p_band, 0.0), jnp.where(from_prev, 0.0, p_band)],
                axis=0).astype(BF16)
        vt_prev = vtp_ref[0, d0:d0 + HEAD_DIM, :] if blk == 0 else vtc_ref[0, d0:d0 + HEAD_DIM, c0 - WINDOW:c0]
        vt_band = jnp.concatenate([vt_prev, vtc_ref[0, d0:d0 + HEAD_DIM, c0:c0 + WINDOW]], axis=1)
        og = (_mm(vt_band, pt_ref[u, N_META:, :])
              + _mm(vtm_ref[d0:d0 + HEAD_DIM, :], pt_ref[u, 0:N_META, :]))
        for i in range(GROUP):
            h0 = (GROUP * g + i) * HEAD_DIM
            ot_ref[h0:h0 + HEAD_DIM, c0:c0 + WINDOW] = og[:, i * WINDOW:(i + 1) * WINDOW] * inv[i]


def _swa(h, sinks, g_pre, g_post, wqt, w_o, cos_t, sin_t, k_pad, vt, k_meta, vt_meta, *, rows):
    b, length, _ = h.shape
    steps = length // rows
    per = rows // WINDOW
    n_units = min(rows, _SWA_PASS_ROWS) // WINDOW * N_KV_HEADS
    seq_spec = pl.BlockSpec((1, rows, D_MODEL), lambda i, j: (i, j, 0))
    rope_spec = pl.BlockSpec((ROT_HALF, rows), lambda i, j: (0, j))
    prev_blk = lambda j: jnp.maximum(j * per - 1, 0)
    in_specs = [
        pl.BlockSpec(memory_space=pltpu.SMEM),
        seq_spec, _const_spec((1, D_MODEL)), _const_spec((1, D_MODEL)),
        _const_spec((D_MODEL, D_MODEL)), _const_spec((D_MODEL, D_MODEL)),
        rope_spec, rope_spec,
        pl.BlockSpec((1, rows, K_PAD), lambda i, j: (i, j, 0)),
        pl.BlockSpec((1, WINDOW, K_PAD), lambda i, j: (i, prev_blk(j), 0)),
        pl.BlockSpec((1, KV_DIM, rows), lambda i, j: (i, 0, j)),
        pl.BlockSpec((1, KV_DIM, WINDOW), lambda i, j: (i, 0, prev_blk(j))),
        _const_spec((N_META, K_PAD)), _const_spec((KV_DIM, N_META)),
    ]
    return pl.pallas_call(
        functools.partial(_swa_kernel, rows=rows),
        grid=(b, steps),
        in_specs=in_specs,
        out_specs=seq_spec,
        out_shape=jax.ShapeDtypeStruct(h.shape, F32),
        scratch_shapes=[pltpu.VMEM((D_MODEL, rows), F32),
                        pltpu.VMEM((n_units, N_META + 2 * WINDOW, GROUP * WINDOW), F32),
                        pltpu.VMEM((n_units, N_META + 2 * WINDOW, GROUP * WINDOW), BF16)],
        compiler_params=pltpu.CompilerParams(dimension_semantics=("parallel", "parallel"),
                                             vmem_limit_bytes=_VMEM_LIMIT),
        name="swa",
    )(sinks, h, g_pre, g_post, wqt, w_o, cos_t, sin_t, k_pad, k_pad, vt, vt, k_meta, vt_meta)


def _rope_angles(positions):
    inv_freq = ROPE_THETA ** (-jnp.arange(0, ROT_DIM, 2, dtype=F32) / ROT_DIM)
    ang = positions.astype(F32)[:, None] * inv_freq[None, :]
    return jnp.cos(ang), jnp.sin(ang)


def _k_rope_tables(cos, sin):
    n = cos.shape[0]
    rest = LANES - ROT_DIM
    rc = jnp.concatenate([cos, cos, jnp.ones((n, rest), F32)], axis=1)
    ra = jnp.concatenate([-sin, jnp.zeros((n, ROT_HALF + rest), F32)], axis=1)
    rb = jnp.concatenate([jnp.zeros((n, ROT_HALF), F32), sin, jnp.zeros((n, rest), F32)], axis=1)
    return rc, ra, rb


def kernel(x, meta_tokens, norm_gains, w_ffn_in, w_ffn_out, gla_w_in, gla_w_gate, gla_b_gate, gla_norm, gla_w_out,
           kv_norm, w_kv, swa_w_q, swa_sinks, swa_w_out):
    batch, seq, _ = x.shape
    depth = norm_gains.shape[0]
    n_a = gla_w_in.shape[0]
    tm = 1024
    gla_rows = 1024
    swa_rows = 1024

    gains = norm_gains.reshape(depth, 6, 1, D_MODEL)
    n_ffn = batch * seq // tm
    ffn_order = [(layer, which) for layer in range(depth) for which in range(2)]
    ffn_w = {}

    def ffn_main(h, layer, which, g_pre, g_post, **kw):
        pos = ffn_order.index((layer, which))
        nxt = ffn_order[pos + 1] if pos + 1 < len(ffn_order) else None
        cast = (w_ffn_in, w_ffn_out, nxt) if nxt is not None else None
        out = _ffn(h, g_pre, g_post, *ffn_w[(layer, which)], tm=tm, cast_next=cast, **kw)
        if nxt is None:
            return out
        *out, nxt_in, nxt_out = out
        ffn_w[nxt] = (nxt_in, nxt_out)
        return out if len(out) > 1 else out[0]

    h = x.reshape(batch * seq, D_MODEL)
    hm = jnp.pad(meta_tokens.astype(x.dtype), ((0, META_ROWS - N_META), (0, 0)))

    cos_m, sin_m = _rope_angles(jnp.arange(META_ROWS))
    cos_r, sin_r = _rope_angles(jnp.arange(N_META, N_META + seq))
    wk = jnp.pad(w_kv[:, :KV_DIM].astype(BF16).reshape(D_MODEL, N_KV_HEADS, HEAD_DIM),
                 ((0, 0), (0, 0), (0, LANES - HEAD_DIM))).reshape(D_MODEL, K_PAD)
    g_kv = kv_norm.reshape(1, D_MODEL)

    k_pad = vt = k_meta = vt_meta = wvt = None
    for layer in range(depth):
        g = gains[layer]
        last = layer == depth - 1
        if layer < n_a:
            hm = _ffn(hm, g[0], g[1], (w_ffn_in, (layer, 0)), (w_ffn_out, (layer, 0)), tm=META_ROWS,
                      cast_self=layer == 0)
            if layer == 0:
                hm, *ffn_w[(0, 0)] = hm
        if (layer, 0) not in ffn_w:
            ffn_w[(layer, 0)] = (w_ffn_in[layer, 0].astype(BF16), w_ffn_out[layer, 0].astype(BF16))
        if layer < n_a:
            a = layer
            side = [_cast_job(gla_w_in, (a,), n_ffn), _cast_job(gla_w_out, (a,), n_ffn)]
            if layer == 0:
                side.append(_transpose_job(w_kv, (), n_ffn, slice(KV_DIM, 2 * KV_DIM)))
                h, gla_w_in_b, gla_w_out_b, wvt = ffn_main(h, layer, 0, g[0], g[1], side=side)
            else:
                h, gla_w_in_b, gla_w_out_b = ffn_main(h, layer, 0, g[0], g[1], side=side)
            w_lr = jnp.pad(gla_w_in[a, :, GLA_IN_MAIN:], ((0, 0), (0, LANES - GLA_RANK)))
            w_gate_a = jnp.pad(gla_w_gate[a], ((0, LANES - GLA_RANK), (0, 0)))
            gla_vecs = (gla_b_gate[a].reshape(1, GLA_QK), gla_norm[a].reshape(1, GLA_DV))
            s_zero = jnp.zeros((GLA_HEADS, GLA_DK, GLA_DV), F32)
            hm3, s_meta = _gla(hm[None], s_zero, g[2], g[3], gla_w_in, a, w_lr, w_gate_a, *gla_vecs, gla_w_out[a],
                               rows=META_ROWS, valid_rows=N_META, emit_state=True)
            hm = hm3[0]
            h = _gla(h.reshape(batch, seq, D_MODEL), s_meta[0], g[2], g[3], gla_w_in_b, a, w_lr.astype(BF16),
                     w_gate_a.astype(BF16), *gla_vecs, gla_w_out_b, rows=gla_rows)[0]
            h = h.reshape(batch * seq, D_MODEL)
        else:
            b = layer - n_a
            side = [_transpose_job(swa_w_q, (b,), n_ffn, slice(0, D_MODEL), HEAD_DIM ** -0.5),
                    _cast_job(swa_w_out, (b,), n_ffn)]
            h, wqt, w_o = ffn_main(h, layer, 0, g[0], g[1], side=side)
            h = _swa(h.reshape(batch, seq, D_MODEL), swa_sinks[b], g[2], g[3], wqt, w_o,
                     cos_r.T, sin_r.T, k_pad.reshape(batch, seq, K_PAD), vt, k_meta, vt_meta, rows=swa_rows)
            h = h.reshape(batch * seq, D_MODEL)
        if layer == n_a - 1:
            kv_w = (g_kv, wk, wvt)
            h, k_pad, vt = ffn_main(h, layer, 1, g[4], g[5], kv=kv_w + _k_rope_tables(cos_r, sin_r),
                                    tiles_per_seq=seq // tm)
            _, k_m, vt_m = _ffn(hm, g[4], g[5], *ffn_w[(layer, 1)], tm=META_ROWS,
                                kv=kv_w + _k_rope_tables(cos_m, sin_m))
            k_meta = k_m[:N_META]
            vt_meta = vt_m[0, :, :N_META]
        else:
            h = ffn_main(h, layer, 1, g[4], g[5])
            if not last and layer < n_a:
                hm = _ffn(hm, g[4], g[5], (w_ffn_in, (layer, 1)), (w_ffn_out, (layer, 1)), tm=META_ROWS)
    return h.reshape(batch, seq, D_MODEL)
```

```python
import functools

import jax
import jax.numpy as jnp
from jax import lax
from jax.experimental import pallas as pl
from jax.experimental.pallas import tpu as pltpu

F32 = jnp.float32
BF16 = jnp.bfloat16

D_MODEL = 1024
D_FF = 2816
N_META = 16
EPS = 1e-6
NEG_INF = -1e30

GLA_HEADS = 4
GLA_DK = 128
GLA_DV = 256
GLA_QK = GLA_HEADS * GLA_DK
GLA_V = GLA_HEADS * GLA_DV
GLA_RANK = 16
GLA_TAU = 16.0
GLA_CHUNK = 64
GLA_IN_MAIN = 2 * GLA_QK + 2 * GLA_V
GLA_ATT_ROWS = 256

N_Q_HEADS = 16
N_KV_HEADS = 4
GROUP = N_Q_HEADS // N_KV_HEADS
HEAD_DIM = 64
WINDOW = 128
ROT_DIM = HEAD_DIM // 4
ROT_HALF = ROT_DIM // 2
ROPE_THETA = 500000.0
KV_DIM = N_KV_HEADS * HEAD_DIM
LANES = 128
K_PAD = N_KV_HEADS * LANES

META_ROWS = 64

_VMEM_LIMIT = 56 * 1024 * 1024

_FF_CHUNKS = ((0, 768), (768, 1536), (1536, 2304), (2304, 2816))
_FFN_SUB_ROWS = 512
_KV_LAG_CHUNKS = 2
_BF16_SUBLANES = 16
_SWA_PASS_ROWS = 512

_NT = (((1,), (1,)), ((), ()))
_TN = (((0,), (0,)), ((), ()))


def _rmsnorm(x, g):
    return x * lax.rsqrt(jnp.mean(x * x, axis=-1, keepdims=True) + EPS) * g


def _sigmoid(x):
    return 1.0 / (1.0 + jnp.exp(-x))


def _mm(a, b, dims=None):
    precision = lax.Precision.HIGHEST if a.dtype == F32 else None
    if dims is None:
        return jnp.dot(a, b, preferred_element_type=F32, precision=precision)
    return lax.dot_general(a, b, dims, preferred_element_type=F32, precision=precision)


def _const_spec(shape):
    zeros = (0,) * len(shape)
    return pl.BlockSpec(shape, lambda *_: zeros, pipeline_mode=pl.Buffered(1))


def _rope_rows(xt, c, s, n_heads):
    parts = []
    for h in range(n_heads):
        base = h * HEAD_DIM
        x1 = xt[base:base + ROT_HALF]
        x2 = xt[base + ROT_HALF:base + ROT_DIM]
        parts += [x1 * c - x2 * s, x2 * c + x1 * s, xt[base + ROT_DIM:base + HEAD_DIM]]
    return jnp.concatenate(parts, axis=0)


def _ffn_kernel(*refs, with_kv, cast_self, side_fns):
    refs = list(refs)
    act_ref = refs.pop()
    n_side = len(side_fns)
    side_dst = refs[len(refs) - n_side:]
    del refs[len(refs) - n_side:]
    if cast_self:
        self_dst = refs[-2:]
        del refs[-2:]
    if with_kv:
        k_ref, vt_ref = refs[-2:]
        del refs[-2:]
    o_ref = refs.pop()
    side_src = refs[len(refs) - n_side:]
    del refs[len(refs) - n_side:]
    if with_kv:
        gkv_ref, wkv_ref, rc_ref, ra_ref, rb_ref = refs[-5:]
        del refs[-5:]
    h_ref, gpre_ref, gpost_ref, win_ref, wout_ref = refs
    for fn, src, dst in zip(side_fns, side_src, side_dst):
        dst[...] = fn(src[...]).astype(dst.dtype)
    if cast_self:
        for src, dst in zip((win_ref, wout_ref), self_dst):
            dst[...] = src[...].astype(dst.dtype)
    mxu = win_ref.dtype
    tm = h_ref.shape[0]
    n_sub = 2 if tm >= 2 * _FFN_SUB_ROWS else 1
    sub = tm // n_sub

    def shared_kv(rows, h_new):
        kn = _rmsnorm(h_new, gkv_ref[...]).astype(mxu)
        kv = _mm(kn, wkv_ref[...])
        rc = rc_ref[rows, :]
        ra = ra_ref[rows, :]
        rb = rb_ref[rows, :]
        low = lax.broadcasted_iota(jnp.int32, (sub, LANES), 1) < HEAD_DIM
        cols = []
        for grp in range(KV_DIM // LANES):
            kg = kv[:, grp * LANES:(grp + 1) * LANES]
            kg = (kg * rc + pltpu.roll(kg, LANES - ROT_HALF, axis=1) * ra + pltpu.roll(kg, ROT_HALF, axis=1) * rb)
            cols += [jnp.where(low, kg, 0.0), jnp.where(low, pltpu.roll(kg, HEAD_DIM, axis=1), 0.0)]
        k_ref[rows, :] = jnp.concatenate(cols, axis=1).astype(k_ref.dtype)
        vt_ref[0, :, rows] = kv[:, KV_DIM:].T.astype(vt_ref.dtype)

    pending_kv = None
    for blk in range(n_sub):
        rows = slice(blk * sub, (blk + 1) * sub)
        x = h_ref[rows, :]
        xn = _rmsnorm(x, gpre_ref[...]).astype(mxu)
        for ci, (lo, hi) in enumerate(_FF_CHUNKS):
            gate = _mm(xn, win_ref[:, lo:hi])
            up = _mm(xn, win_ref[:, D_FF + lo:D_FF + hi])
            act_ref[rows, lo:hi] = (gate * _sigmoid(gate) * up).astype(mxu)
            if pending_kv is not None and ci == _KV_LAG_CHUNKS - 1:
                shared_kv(*pending_kv)
                pending_kv = None
        y = _mm(act_ref[rows, :], wout_ref[...])
        h_new = x + 0.5 * _rmsnorm(y, gpost_ref[...])
        o_ref[rows, :] = h_new
        if with_kv:
            pending_kv = (rows, h_new)
    if pending_kv is not None:
        shared_kv(*pending_kv)


def _weight_arg(w, shape):
    if isinstance(w, tuple):
        stack, sel = w
        return stack, pl.BlockSpec((None, None) + shape, lambda *_: (sel[0], sel[1], 0, 0),
                                   pipeline_mode=pl.Buffered(1))
    return w, _const_spec(shape)


def _cast_job(src, lead, n):
    shape = src.shape[len(lead):]
    per = 1
    while (shape[0] * per // n) % _BF16_SUBLANES:
        per *= 2
    blk = (shape[0] * per // n, shape[1])
    in_spec = pl.BlockSpec((None,) * len(lead) + blk, lambda i: tuple(lead) + (i // per, 0))
    return src, in_spec, jax.ShapeDtypeStruct(shape, BF16), pl.BlockSpec(blk, lambda i: (i // per, 0)), lambda x: x


def _transpose_job(src, lead, n, cols, scale=None):
    rows, width = src.shape[len(lead):]
    per = n * LANES // rows
    n_out = cols.stop - cols.start
    in_spec = pl.BlockSpec((None,) * len(lead) + (LANES, width), lambda i: tuple(lead) + (i // per, 0))
    out_spec = pl.BlockSpec((n_out, LANES), lambda i: (0, i // per))
    fn = lambda x: (x[:, cols] if scale is None else x[:, cols] * scale).T
    return src, in_spec, jax.ShapeDtypeStruct((n_out, rows), BF16), out_spec, fn


def _ffn(h, g_pre, g_post, w_in, w_out, *, tm, kv=None, tiles_per_seq=1, cast_next=None, cast_self=False,
         side=()):
    t = h.shape[0]
    n = t // tm
    row_spec = pl.BlockSpec((tm, D_MODEL), lambda i: (i, 0))
    w_in, w_in_spec = _weight_arg(w_in, (D_MODEL, 2 * D_FF))
    w_out, w_out_spec = _weight_arg(w_out, (D_FF, D_MODEL))
    in_specs = [row_spec, _const_spec((1, D_MODEL)), _const_spec((1, D_MODEL)), w_in_spec, w_out_spec]
    args = [h, g_pre, g_post, w_in, w_out]
    out_shape = [jax.ShapeDtypeStruct((t, D_MODEL), F32)]
    out_specs = [row_spec]
    if kv is not None:
        g_kv, wkv, rc, ra, rb = kv
        rope_spec = pl.BlockSpec((tm, LANES), lambda i: (i % tiles_per_seq, 0))
        in_specs += [_const_spec((1, D_MODEL)), _const_spec((D_MODEL, 2 * KV_DIM)),
                     rope_spec, rope_spec, rope_spec]
        args += [g_kv, wkv, rc, ra, rb]
        n_seq = n // tiles_per_seq
        out_shape += [jax.ShapeDtypeStruct((t, K_PAD), BF16),
                      jax.ShapeDtypeStruct((n_seq, KV_DIM, tiles_per_seq * tm), BF16)]
        out_specs += [pl.BlockSpec((tm, K_PAD), lambda i: (i, 0)),
                      pl.BlockSpec((1, KV_DIM, tm), lambda i: (i // tiles_per_seq, 0, i % tiles_per_seq))]
    if cast_self:
        assert n == 1
        for shape in ((D_MODEL, 2 * D_FF), (D_FF, D_MODEL)):
            out_shape.append(jax.ShapeDtypeStruct(shape, BF16))
            out_specs.append(pl.BlockSpec(shape, lambda i: (0, 0)))
    side = list(side)
    if cast_next is not None:
        nxt_in, nxt_out, sel = cast_next
        side += [_cast_job(nxt_in, sel, n), _cast_job(nxt_out, sel, n)]
    for src, in_spec, out_sds, out_spec, _ in side:
        args.append(src)
        in_specs.append(in_spec)
        out_shape.append(out_sds)
        out_specs.append(out_spec)
    out = pl.pallas_call(
        functools.partial(_ffn_kernel, with_kv=kv is not None, cast_self=cast_self,
                          side_fns=tuple(job[-1] for job in side)),
        grid=(n,),
        in_specs=in_specs,
        out_specs=out_specs,
        out_shape=out_shape,
        scratch_shapes=[pltpu.VMEM((tm, D_FF), w_in.dtype)],
        compiler_params=pltpu.CompilerParams(dimension_semantics=("arbitrary",), vmem_limit_bytes=_VMEM_LIMIT),
        name="ffn_kv" if kv is not None else "ffn",
    )(*args)
    return out if len(out) > 1 else out[0]


def _gla_kernel(*refs, rows, valid_rows, emit_state):
    if emit_state:
        (h_ref, s0_ref, gpre_ref, gpost_ref, win_ref, wlr_ref, wgate_ref, bgate_ref, ghead_ref, wout_ref,
         o_ref, sfin_ref, st_ref) = refs
    else:
        (h_ref, s0_ref, gpre_ref, gpost_ref, win_ref, wlr_ref, wgate_ref, bgate_ref, ghead_ref, wout_ref,
         o_ref, st_ref) = refs
    step = pl.program_id(1)
    n_chunks = rows // GLA_CHUNK
    blk_rows = min(rows, GLA_ATT_ROWS)
    bsl = [slice(b * blk_rows, (b + 1) * blk_rows) for b in range(rows // blk_rows)]

    @pl.when(step == 0)
    def _():
        st_ref[...] = s0_ref[...]

    mxu = win_ref.dtype
    x = h_ref[0]
    hn = _rmsnorm(x, gpre_ref[...]).astype(mxu)
    lr = _mm(hn, wlr_ref[...])
    gp = _mm(lr.astype(mxu), wgate_ref[...]) + bgate_ref[...]
    gk = (jnp.minimum(gp, 0.0) - jnp.log1p(jnp.exp(-jnp.abs(gp)))) * (1.0 / GLA_TAU)
    q = _mm(hn, win_ref[:, 0:GLA_QK])
    k = _mm(hn, win_ref[:, GLA_QK:2 * GLA_QK])
    if valid_rows is not None:
        live = lax.broadcasted_iota(jnp.int32, (rows, 1), 0) < valid_rows
        gk = jnp.where(live, gk, 0.0)
        k = jnp.where(live, k, 0.0)

    ri = lax.broadcasted_iota(jnp.int32, (blk_rows, blk_rows), 0)
    ci = lax.broadcasted_iota(jnp.int32, (blk_rows, blk_rows), 1)
    intra = ((ri // GLA_CHUNK) == (ci // GLA_CHUNK)) & (ci <= ri)
    tri = jnp.where(intra, 1.0, 0.0).astype(mxu)
    g_hi = gk.astype(mxu)
    rem = gk - g_hi.astype(F32)
    g_mid = rem.astype(mxu)
    g_lo = (rem - g_mid.astype(F32)).astype(mxu)
    g_split = jnp.concatenate([g_hi, g_mid, g_lo], axis=1)
    parts = jnp.concatenate([_mm(tri, g_split[rs]) for rs in bsl], axis=0)
    bcum = parts[:, 0:GLA_QK] + parts[:, GLA_QK:2 * GLA_QK] + parts[:, 2 * GLA_QK:3 * GLA_QK]
    btot = jnp.concatenate(
        [jnp.broadcast_to(bcum[(c + 1) * GLA_CHUNK - 1:(c + 1) * GLA_CHUNK], (GLA_CHUNK, GLA_QK))
         for c in range(n_chunks)], axis=0)

    v = _mm(hn, win_ref[:, 2 * GLA_QK:2 * GLA_QK + GLA_V])
    r = _mm(hn, win_ref[:, 2 * GLA_QK + GLA_V:2 * GLA_QK + 2 * GLA_V])
    q_dec = (q * (GLA_DK ** -0.5) * jnp.exp(bcum)).astype(mxu)
    k_dec = (k * jnp.exp(-bcum)).astype(mxu)
    k_rem = (k * jnp.exp(btot - bcum)).astype(mxu)
    vb = v.astype(mxu)
    decay = [jnp.exp(jnp.broadcast_to(bcum[(c + 1) * GLA_CHUNK - 1:(c + 1) * GLA_CHUNK], (LANES, GLA_QK)).T)
             for c in range(n_chunks)]

    heads = range(GLA_HEADS)
    ksl = [slice(hd * GLA_DK, (hd + 1) * GLA_DK) for hd in heads]
    vsl = [slice(hd * GLA_DV, (hd + 1) * GLA_DV) for hd in heads]
    csl = [slice(c * GLA_CHUNK, (c + 1) * GLA_CHUNK) for c in range(n_chunks)]
    att = [[_mm(q_dec[rs, ksl[hd]], k_dec[rs, ksl[hd]], _NT) for rs in bsl] for hd in heads]
    upd = [[_mm(k_rem[csl[c], ksl[hd]], vb[csl[c], vsl[hd]], _TN) for c in range(n_chunks)] for hd in heads]
    o_intra = [jnp.concatenate([_mm(jnp.where(intra, att[hd][b], 0.0).astype(mxu), vb[rs, vsl[hd]])
                                for b, rs in enumerate(bsl)], axis=0) for hd in heads]

    outs = []
    for hd in heads:
        st = st_ref[hd]
        o_head = []
        for c in range(n_chunks):
            o_head.append(o_intra[hd][csl[c]] + _mm(q_dec[csl[c], ksl[hd]], st.astype(mxu)))
            d = decay[c][ksl[hd], :]
            st = st * jnp.concatenate([d] * (GLA_DV // LANES), axis=1) + upd[hd][c]
        st_ref[hd] = st
        outs.append(jnp.concatenate(o_head, axis=0))

    g_head = ghead_ref[...]
    gated = []
    for hd in heads:
        rg = r[:, vsl[hd]]
        gated.append((_rmsnorm(outs[hd], g_head) * (rg * _sigmoid(rg))).astype(mxu))
    for rs in bsl:
        mix = _mm(gated[0][rs], wout_ref[vsl[0], :])
        for hd in heads[1:]:
            mix = mix + _mm(gated[hd][rs], wout_ref[vsl[hd], :])
        o_ref[0, rs, :] = h_ref[0, rs, :] + _rmsnorm(mix, gpost_ref[...])

    if emit_state:
        @pl.when(step == pl.num_programs(1) - 1)
        def _():
            sfin_ref[0] = st_ref[...]


def _gla(h, s0, g_pre, g_post, w_in, layer, w_lr, w_gate, b_gate, g_head, w_out, *, rows, valid_rows=None,
         emit_state=False):
    b, length, _ = h.shape
    steps = length // rows
    seq_spec = pl.BlockSpec((1, rows, D_MODEL), lambda i, j: (i, j, 0))
    state_shape = (GLA_HEADS, GLA_DK, GLA_DV)
    if w_in.ndim == 3:
        w_in_spec = pl.BlockSpec((None, D_MODEL, GLA_IN_MAIN), lambda i, j: (layer, 0, 0),
                                 pipeline_mode=pl.Buffered(1))
    else:
        w_in_spec = _const_spec((D_MODEL, GLA_IN_MAIN))
    in_specs = [seq_spec, _const_spec(state_shape), _const_spec((1, D_MODEL)), _const_spec((1, D_MODEL)),
                w_in_spec,
                _const_spec((D_MODEL, LANES)), _const_spec((LANES, GLA_QK)), _const_spec((1, GLA_QK)),
                _const_spec((1, GLA_DV)), _const_spec((GLA_V, D_MODEL))]
    out_shape = [jax.ShapeDtypeStruct(h.shape, F32)]
    out_specs = [seq_spec]
    if emit_state:
        out_shape.append(jax.ShapeDtypeStruct((b,) + state_shape, F32))
        out_specs.append(pl.BlockSpec((1,) + state_shape, lambda i, j: (i, 0, 0, 0)))
    return pl.pallas_call(
        functools.partial(_gla_kernel, rows=rows, valid_rows=valid_rows, emit_state=emit_state),
        grid=(b, steps),
        in_specs=in_specs,
        out_specs=out_specs,
        out_shape=out_shape,
        scratch_shapes=[pltpu.VMEM(state_shape, F32)],
        compiler_params=pltpu.CompilerParams(dimension_semantics=("parallel", "arbitrary"),
                                             vmem_limit_bytes=_VMEM_LIMIT),
        name="gla_meta" if emit_state else "gla",
    )(h, s0, g_pre, g_post, w_in, w_lr, w_gate, b_gate, g_head, w_out)


def _swa_kernel(sinks_ref, h_ref, gpre_ref, gpost_ref, wqt_ref, wo_ref, cos_ref, sin_ref,
                kc_ref, kp_ref, vtc_ref, vtp_ref, km_ref, vtm_ref, o_ref, ot_ref, st_ref, pt_ref, *, rows):
    step = pl.program_id(1)
    x = h_ref[0]
    hn = _rmsnorm(x, gpre_ref[...]).astype(BF16)
    qt = _mm(wqt_ref[...], hn, _NT)
    qtb = _rope_rows(qt, cos_ref[...], sin_ref[...], N_Q_HEADS).astype(BF16)

    key_j = lax.broadcasted_iota(jnp.int32, (WINDOW, WINDOW), 0)
    qry_i = lax.broadcasted_iota(jnp.int32, (WINDOW, WINDOW), 1)
    from_prev = key_j > qry_i
    first_prev_ok = key_j > qry_i + jnp.where(step > 0, 0, WINDOW)
    per_pass = min(rows, _SWA_PASS_ROWS) // WINDOW
    for first_blk in range(0, rows // WINDOW, per_pass):
        _swa_pass(first_blk, per_pass, sinks_ref, qtb, from_prev, first_prev_ok, kc_ref, kp_ref, vtc_ref, vtp_ref,
                  km_ref, vtm_ref, ot_ref, st_ref, pt_ref)

    mix = _mm(ot_ref[...].astype(BF16), wo_ref[...], _TN)
    o_ref[0] = x + _rmsnorm(mix, gpost_ref[...])


def _swa_pass(first_blk, n_blk, sinks_ref, qtb, from_prev, first_prev_ok, kc_ref, kp_ref, vtc_ref, vtp_ref,
              km_ref, vtm_ref, ot_ref, st_ref, pt_ref):
    units = [(blk, g) for blk in range(first_blk, first_blk + n_blk) for g in range(N_KV_HEADS)]

    for u, (blk, g) in enumerate(units):
        c0 = blk * WINDOW
        l0 = g * LANES
        k_prev = kp_ref[0, :, l0:l0 + HEAD_DIM] if blk == 0 else kc_ref[0, c0 - WINDOW:c0, l0:l0 + HEAD_DIM]
        kcat = jnp.concatenate([km_ref[:, l0:l0 + HEAD_DIM], k_prev,
                                kc_ref[0, c0:c0 + WINDOW, l0:l0 + HEAD_DIM]], axis=0)
        qg = jnp.concatenate([qtb[(GROUP * g + i) * HEAD_DIM:(GROUP * g + i + 1) * HEAD_DIM, c0:c0 + WINDOW]
                              for i in range(GROUP)], axis=1)
        st_ref[u] = _mm(kcat, qg)

    for u, (blk, g) in enumerate(units):
        c0 = blk * WINDOW
        d0 = g * HEAD_DIM
        inv = []
        for i in range(GROUP):
            sink = sinks_ref[GROUP * g + i]
            q0 = i * WINDOW
            s_meta = st_ref[u, 0:N_META, q0:q0 + WINDOW]
            s_prev = st_ref[u, N_META:N_META + WINDOW, q0:q0 + WINDOW]
            s_cur = st_ref[u, N_META + WINDOW:N_META + 2 * WINDOW, q0:q0 + WINDOW]
            if blk == 0:
                s_prev = jnp.where(first_prev_ok, s_prev, NEG_INF)
            s_band = jnp.where(from_prev, s_prev, s_cur)
            m = jnp.maximum(jnp.max(s_band, axis=0, keepdims=True), jnp.max(s_meta, axis=0, keepdims=True))
            m = jnp.maximum(m, sink)
            p_band = jnp.exp(s_band - m)
            p_meta = jnp.exp(s_meta - m)
            den = (jnp.sum(p_band, axis=0, keepdims=True) + jnp.sum(p_meta, axis=0, keepdims=True)
                   + jnp.exp(sink - m))
            inv.append(1.0 / den)
            pt_ref[u, :, q0:q0 + WINDOW] = jnp.concatenate(
                [p_meta, jnp.where(from_prev, p_band, 0.0), jnp.where(from_prev, 0.0, p_band)],
                axis=0).astype(BF16)
        vt_prev = vtp_ref[0, d0:d0 + HEAD_DIM, :] if blk == 0 else vtc_ref[0, d0:d0 + HEAD_DIM, c0 - WINDOW:c0]
        vt_band = jnp.concatenate([vt_prev, vtc_ref[0, d0:d0 + HEAD_DIM, c0:c0 + WINDOW]], axis=1)
        og = (_mm(vt_band, pt_ref[u, N_META:, :])
              + _mm(vtm_ref[d0:d0 + HEAD_DIM, :], pt_ref[u, 0:N_META, :]))
        for i in range(GROUP):
            h0 = (GROUP * g + i) * HEAD_DIM
            ot_ref[h0:h0 + HEAD_DIM, c0:c0 + WINDOW] = og[:, i * WINDOW:(i + 1) * WINDOW] * inv[i]


def _swa(h, sinks, g_pre, g_post, wqt, w_o, cos_t, sin_t, k_pad, vt, k_meta, vt_meta, *, rows):
    b, length, _ = h.shape
    steps = length // rows
    per = rows // WINDOW
    n_units = min(rows, _SWA_PASS_ROWS) // WINDOW * N_KV_HEADS
    seq_spec = pl.BlockSpec((1, rows, D_MODEL), lambda i, j: (i, j, 0))
    rope_spec = pl.BlockSpec((ROT_HALF, rows), lambda i, j: (0, j))
    prev_blk = lambda j: jnp.maximum(j * per - 1, 0)
    in_specs = [
        pl.BlockSpec(memory_space=pltpu.SMEM),
        seq_spec, _const_spec((1, D_MODEL)), _const_spec((1, D_MODEL)),
        _const_spec((D_MODEL, D_MODEL)), _const_spec((D_MODEL, D_MODEL)),
        rope_spec, rope_spec,
        pl.BlockSpec((1, rows, K_PAD), lambda i, j: (i, j, 0)),
        pl.BlockSpec((1, WINDOW, K_PAD), lambda i, j: (i, prev_blk(j), 0)),
        pl.BlockSpec((1, KV_DIM, rows), lambda i, j: (i, 0, j)),
        pl.BlockSpec((1, KV_DIM, WINDOW), lambda i, j: (i, 0, prev_blk(j))),
        _const_spec((N_META, K_PAD)), _const_spec((KV_DIM, N_META)),
    ]
    return pl.pallas_call(
        functools.partial(_swa_kernel, rows=rows),
        grid=(b, steps),
        in_specs=in_specs,
        out_specs=seq_spec,
        out_shape=jax.ShapeDtypeStruct(h.shape, F32),
        scratch_shapes=[pltpu.VMEM((D_MODEL, rows), F32),
                        pltpu.VMEM((n_units, N_META + 2 * WINDOW, GROUP * WINDOW), F32),
                        pltpu.VMEM((n_units, N_META + 2 * WINDOW, GROUP * WINDOW), BF16)],
        compiler_params=pltpu.CompilerParams(dimension_semantics=("parallel", "parallel"),
                                             vmem_limit_bytes=_VMEM_LIMIT),
        name="swa",
    )(sinks, h, g_pre, g_post, wqt, w_o, cos_t, sin_t, k_pad, k_pad, vt, vt, k_meta, vt_meta)


def _rope_angles(positions):
    inv_freq = ROPE_THETA ** (-jnp.arange(0, ROT_DIM, 2, dtype=F32) / ROT_DIM)
    ang = positions.astype(F32)[:, None] * inv_freq[None, :]
    return jnp.cos(ang), jnp.sin(ang)


def _k_rope_tables(cos, sin):
    n = cos.shape[0]
    rest = HEAD_DIM - ROT_DIM
    rc = jnp.concatenate([cos, cos, jnp.ones((n, rest), F32)], axis=1)
    ra = jnp.concatenate([-sin, jnp.zeros((n, ROT_HALF + rest), F32)], axis=1)
    rb = jnp.concatenate([jnp.zeros((n, ROT_HALF), F32), sin, jnp.zeros((n, rest), F32)], axis=1)
    return tuple(jnp.tile(t, (1, LANES // HEAD_DIM)) for t in (rc, ra, rb))


def kernel(x, meta_tokens, norm_gains, w_ffn_in, w_ffn_out, gla_w_in, gla_w_gate, gla_b_gate, gla_norm, gla_w_out,
           kv_norm, w_kv, swa_w_q, swa_sinks, swa_w_out):
    batch, seq, _ = x.shape
    depth = norm_gains.shape[0]
    n_a = gla_w_in.shape[0]
    tm = 1024
    gla_rows = 1024
    swa_rows = 1024

    gains = norm_gains.reshape(depth, 6, 1, D_MODEL)
    n_ffn = batch * seq // tm
    ffn_order = [(layer, which) for layer in range(depth) for which in range(2)]
    ffn_w = {}

    def ffn_main(h, layer, which, g_pre, g_post, **kw):
        pos = ffn_order.index((layer, which))
        nxt = ffn_order[pos + 1] if pos + 1 < len(ffn_order) else None
        cast = (w_ffn_in, w_ffn_out, nxt) if nxt is not None else None
        out = _ffn(h, g_pre, g_post, *ffn_w[(layer, which)], tm=tm, cast_next=cast, **kw)
        if nxt is None:
            return out
        *out, nxt_in, nxt_out = out
        ffn_w[nxt] = (nxt_in, nxt_out)
        return out if len(out) > 1 else out[0]

    h = x.reshape(batch * seq, D_MODEL)
    hm = jnp.pad(meta_tokens.astype(x.dtype), ((0, META_ROWS - N_META), (0, 0)))

    cos_m, sin_m = _rope_angles(jnp.arange(META_ROWS))
    cos_r, sin_r = _rope_angles(jnp.arange(N_META, N_META + seq))
    g_kv = kv_norm.reshape(1, D_MODEL)

    k_pad = vt = k_meta = vt_meta = w_kv_b = None
    for layer in range(depth):
        g = gains[layer]
        last = layer == depth - 1
        if layer < n_a:
            hm = _ffn(hm, g[0], g[1], (w_ffn_in, (layer, 0)), (w_ffn_out, (layer, 0)), tm=META_ROWS,
                      cast_self=layer == 0)
            if layer == 0:
                hm, *ffn_w[(0, 0)] = hm
        if (layer, 0) not in ffn_w:
            ffn_w[(layer, 0)] = (w_ffn_in[layer, 0].astype(BF16), w_ffn_out[layer, 0].astype(BF16))
        if layer < n_a:
            a = layer
            side = [_cast_job(gla_w_in, (a,), n_ffn), _cast_job(gla_w_out, (a,), n_ffn)]
            if layer == 0:
                side.append(_cast_job(w_kv, (), n_ffn))
                h, gla_w_in_b, gla_w_out_b, w_kv_b = ffn_main(h, layer, 0, g[0], g[1], side=side)
            else:
                h, gla_w_in_b, gla_w_out_b = ffn_main(h, layer, 0, g[0], g[1], side=side)
            w_lr = jnp.pad(gla_w_in[a, :, GLA_IN_MAIN:], ((0, 0), (0, LANES - GLA_RANK)))
            w_gate_a = jnp.pad(gla_w_gate[a], ((0, LANES - GLA_RANK), (0, 0)))
            gla_vecs = (gla_b_gate[a].reshape(1, GLA_QK), gla_norm[a].reshape(1, GLA_DV))
            s_zero = jnp.zeros((GLA_HEADS, GLA_DK, GLA_DV), F32)
            hm3, s_meta = _gla(hm[None], s_zero, g[2], g[3], gla_w_in, a, w_lr, w_gate_a, *gla_vecs, gla_w_out[a],
                               rows=META_ROWS, valid_rows=N_META, emit_state=True)
            hm = hm3[0]
            h = _gla(h.reshape(batch, seq, D_MODEL), s_meta[0], g[2], g[3], gla_w_in_b, a, w_lr.astype(BF16),
                     w_gate_a.astype(BF16), *gla_vecs, gla_w_out_b, rows=gla_rows)[0]
            h = h.reshape(batch * seq, D_MODEL)
        else:
            b = layer - n_a
            side = [_transpose_job(swa_w_q, (b,), n_ffn, slice(0, D_MODEL), HEAD_DIM ** -0.5),
                    _cast_job(swa_w_out, (b,), n_ffn)]
            h, wqt, w_o = ffn_main(h, layer, 0, g[0], g[1], side=side)
            h = _swa(h.reshape(batch, seq, D_MODEL), swa_sinks[b], g[2], g[3], wqt, w_o,
                     cos_r.T, sin_r.T, k_pad.reshape(batch, seq, K_PAD), vt, k_meta, vt_meta, rows=swa_rows)
            h = h.reshape(batch * seq, D_MODEL)
        if layer == n_a - 1:
            kv_w = (g_kv, w_kv_b)
            h, k_pad, vt = ffn_main(h, layer, 1, g[4], g[5], kv=kv_w + _k_rope_tables(cos_r, sin_r),
                                    tiles_per_seq=seq // tm)
            _, k_m, vt_m = _ffn(hm, g[4], g[5], *ffn_w[(layer, 1)], tm=META_ROWS,
                                kv=kv_w + _k_rope_tables(cos_m, sin_m))
            k_meta = k_m[:N_META]
            vt_meta = vt_m[0, :, :N_META]
        else:
            h = ffn_main(h, layer, 1, g[4], g[5])
            if not last and layer < n_a:
                hm = _ffn(hm, g[4], g[5], (w_ffn_in, (layer, 1)), (w_ffn_out, (layer, 1)), tm=META_ROWS)
    return h.reshape(batch, seq, D_MODEL)
```

```python
import functools

import jax
import jax.numpy as jnp
from jax import lax
from jax.experimental import pallas as pl
from jax.experimental.pallas import tpu as pltpu

F32 = jnp.float32
BF16 = jnp.bfloat16

D_MODEL = 1024
D_FF = 2816
N_META = 16
EPS = 1e-6
NEG_INF = -1e30

GLA_HEADS = 4
GLA_DK = 128
GLA_DV = 256
GLA_QK = GLA_HEADS * GLA_DK
GLA_V = GLA_HEADS * GLA_DV
GLA_RANK = 16
GLA_TAU = 16.0
GLA_CHUNK = 64
GLA_IN_MAIN = 2 * GLA_QK + 2 * GLA_V
GLA_ATT_ROWS = 256

N_Q_HEADS = 16
N_KV_HEADS = 4
GROUP = N_Q_HEADS // N_KV_HEADS
HEAD_DIM = 64
WINDOW = 128
ROT_DIM = HEAD_DIM // 4
ROT_HALF = ROT_DIM // 2
ROPE_THETA = 500000.0
KV_DIM = N_KV_HEADS * HEAD_DIM
LANES = 128
K_PAD = N_KV_HEADS * LANES

META_ROWS = 64

_VMEM_LIMIT = 56 * 1024 * 1024

_FF_CHUNKS = ((0, 768), (768, 1536), (1536, 2304), (2304, 2816))
_FFN_SUB_ROWS = 512
_KV_LAG_CHUNKS = 2
_BF16_SUBLANES = 16
_SWA_PASS_ROWS = 512

_NT = (((1,), (1,)), ((), ()))
_TN = (((0,), (0,)), ((), ()))


def _rmsnorm(x, g):
    return x * lax.rsqrt(jnp.mean(x * x, axis=-1, keepdims=True) + EPS) * g


def _sigmoid(x):
    return 1.0 / (1.0 + jnp.exp(-x))


def _mm(a, b, dims=None):
    precision = lax.Precision.HIGHEST if a.dtype == F32 else None
    if dims is None:
        return jnp.dot(a, b, preferred_element_type=F32, precision=precision)
    return lax.dot_general(a, b, dims, preferred_element_type=F32, precision=precision)


def _const_spec(shape):
    zeros = (0,) * len(shape)
    return pl.BlockSpec(shape, lambda *_: zeros, pipeline_mode=pl.Buffered(1))


def _rope_rows(xt, c, s, n_heads):
    parts = []
    for h in range(n_heads):
        base = h * HEAD_DIM
        x1 = xt[base:base + ROT_HALF]
        x2 = xt[base + ROT_HALF:base + ROT_DIM]
        parts += [x1 * c - x2 * s, x2 * c + x1 * s, xt[base + ROT_DIM:base + HEAD_DIM]]
    return jnp.concatenate(parts, axis=0)


def _ffn_kernel(*refs, with_kv, cast_self, side_fns):
    refs = list(refs)
    act_ref = refs.pop()
    n_side = len(side_fns)
    side_dst = refs[len(refs) - n_side:]
    del refs[len(refs) - n_side:]
    if cast_self:
        self_dst = refs[-2:]
        del refs[-2:]
    if with_kv:
        k_ref, vt_ref = refs[-2:]
        del refs[-2:]
    o_ref = refs.pop()
    side_src = refs[len(refs) - n_side:]
    del refs[len(refs) - n_side:]
    if with_kv:
        gkv_ref, wkv_ref, rc_ref, ra_ref, rb_ref = refs[-5:]
        del refs[-5:]
    h_ref, gpre_ref, gpost_ref, win_ref, wout_ref = refs
    for fn, src, dst in zip(side_fns, side_src, side_dst):
        dst[...] = fn(src[...]).astype(dst.dtype)
    if cast_self:
        for src, dst in zip((win_ref, wout_ref), self_dst):
            dst[...] = src[...].astype(dst.dtype)
    mxu = win_ref.dtype
    tm = h_ref.shape[0]
    n_sub = 2 if tm >= 2 * _FFN_SUB_ROWS else 1
    sub = tm // n_sub

    def shared_kv(rows, h_new):
        kn = _rmsnorm(h_new, gkv_ref[...]).astype(mxu)
        kv = _mm(kn, wkv_ref[...])
        rc = rc_ref[rows, :]
        ra = ra_ref[rows, :]
        rb = rb_ref[rows, :]
        low = lax.broadcasted_iota(jnp.int32, (sub, LANES), 1) < HEAD_DIM
        cols = []
        for grp in range(KV_DIM // LANES):
            kg = kv[:, grp * LANES:(grp + 1) * LANES]
            kg = (kg * rc + pltpu.roll(kg, LANES - ROT_HALF, axis=1) * ra + pltpu.roll(kg, ROT_HALF, axis=1) * rb)
            cols += [jnp.where(low, kg, 0.0), jnp.where(low, pltpu.roll(kg, HEAD_DIM, axis=1), 0.0)]
        k_ref[rows, :] = jnp.concatenate(cols, axis=1).astype(k_ref.dtype)
        vt_ref[0, :, rows] = kv[:, KV_DIM:].T.astype(vt_ref.dtype)

    pending_kv = None
    for blk in range(n_sub):
        rows = slice(blk * sub, (blk + 1) * sub)
        x = h_ref[rows, :]
        if with_kv == "in":
            pending_kv = (rows, x)
        xn = _rmsnorm(x, gpre_ref[...]).astype(mxu)
        for ci, (lo, hi) in enumerate(_FF_CHUNKS):
            gate = _mm(xn, win_ref[:, lo:hi])
            up = _mm(xn, win_ref[:, D_FF + lo:D_FF + hi])
            act_ref[rows, lo:hi] = (gate * _sigmoid(gate) * up).astype(mxu)
            if pending_kv is not None and ci == _KV_LAG_CHUNKS - 1:
                shared_kv(*pending_kv)
                pending_kv = None
        y = _mm(act_ref[rows, :], wout_ref[...])
        h_new = x + 0.5 * _rmsnorm(y, gpost_ref[...])
        o_ref[rows, :] = h_new
        if with_kv == "out":
            pending_kv = (rows, h_new)
    if pending_kv is not None:
        shared_kv(*pending_kv)


def _weight_arg(w, shape):
    if isinstance(w, tuple):
        stack, sel = w
        return stack, pl.BlockSpec((None, None) + shape, lambda *_: (sel[0], sel[1], 0, 0),
                                   pipeline_mode=pl.Buffered(1))
    return w, _const_spec(shape)


def _cast_job(src, lead, n):
    shape = src.shape[len(lead):]
    per = 1
    while (shape[0] * per // n) % _BF16_SUBLANES:
        per *= 2
    blk = (shape[0] * per // n, shape[1])
    in_spec = pl.BlockSpec((None,) * len(lead) + blk, lambda i: tuple(lead) + (i // per, 0))
    return src, in_spec, jax.ShapeDtypeStruct(shape, BF16), pl.BlockSpec(blk, lambda i: (i // per, 0)), lambda x: x


def _transpose_job(src, lead, n, cols, scale=None):
    rows, width = src.shape[len(lead):]
    per = n * LANES // rows
    n_out = cols.stop - cols.start
    in_spec = pl.BlockSpec((None,) * len(lead) + (LANES, width), lambda i: tuple(lead) + (i // per, 0))
    out_spec = pl.BlockSpec((n_out, LANES), lambda i: (0, i // per))
    fn = lambda x: (x[:, cols] if scale is None else x[:, cols] * scale).T
    return src, in_spec, jax.ShapeDtypeStruct((n_out, rows), BF16), out_spec, fn


def _ffn(h, g_pre, g_post, w_in, w_out, *, tm, kv=None, kv_of="out", tiles_per_seq=1, cast_next=None,
         cast_self=False, side=()):
    t = h.shape[0]
    n = t // tm
    row_spec = pl.BlockSpec((tm, D_MODEL), lambda i: (i, 0))
    w_in, w_in_spec = _weight_arg(w_in, (D_MODEL, 2 * D_FF))
    w_out, w_out_spec = _weight_arg(w_out, (D_FF, D_MODEL))
    in_specs = [row_spec, _const_spec((1, D_MODEL)), _const_spec((1, D_MODEL)), w_in_spec, w_out_spec]
    args = [h, g_pre, g_post, w_in, w_out]
    out_shape = [jax.ShapeDtypeStruct((t, D_MODEL), F32)]
    out_specs = [row_spec]
    if kv is not None:
        g_kv, wkv, rc, ra, rb = kv
        rope_spec = pl.BlockSpec((tm, LANES), lambda i: (i % tiles_per_seq, 0))
        in_specs += [_const_spec((1, D_MODEL)), _const_spec((D_MODEL, 2 * KV_DIM)),
                     rope_spec, rope_spec, rope_spec]
        args += [g_kv, wkv, rc, ra, rb]
        n_seq = n // tiles_per_seq
        out_shape += [jax.ShapeDtypeStruct((t, K_PAD), BF16),
                      jax.ShapeDtypeStruct((n_seq, KV_DIM, tiles_per_seq * tm), BF16)]
        out_specs += [pl.BlockSpec((tm, K_PAD), lambda i: (i, 0)),
                      pl.BlockSpec((1, KV_DIM, tm), lambda i: (i // tiles_per_seq, 0, i % tiles_per_seq))]
    if cast_self:
        assert n == 1
        for shape in ((D_MODEL, 2 * D_FF), (D_FF, D_MODEL)):
            out_shape.append(jax.ShapeDtypeStruct(shape, BF16))
            out_specs.append(pl.BlockSpec(shape, lambda i: (0, 0)))
    side = list(side)
    if cast_next is not None:
        nxt_in, nxt_out, sel = cast_next
        side += [_cast_job(nxt_in, sel, n), _cast_job(nxt_out, sel, n)]
    for src, in_spec, out_sds, out_spec, _ in side:
        args.append(src)
        in_specs.append(in_spec)
        out_shape.append(out_sds)
        out_specs.append(out_spec)
    out = pl.pallas_call(
        functools.partial(_ffn_kernel, with_kv=kv_of if kv is not None else None, cast_self=cast_self,
                          side_fns=tuple(job[-1] for job in side)),
        grid=(n,),
        in_specs=in_specs,
        out_specs=out_specs,
        out_shape=out_shape,
        scratch_shapes=[pltpu.VMEM((tm, D_FF), w_in.dtype)],
        compiler_params=pltpu.CompilerParams(dimension_semantics=("arbitrary",), vmem_limit_bytes=_VMEM_LIMIT),
        name="ffn_kv" if kv is not None else "ffn",
    )(*args)
    return out if len(out) > 1 else out[0]


def _gla_kernel(*refs, rows, valid_rows, emit_state):
    if emit_state:
        (h_ref, s0_ref, gpre_ref, gpost_ref, win_ref, wlr_ref, wgate_ref, bgate_ref, ghead_ref, wout_ref,
         o_ref, sfin_ref, st_ref) = refs
    else:
        (h_ref, s0_ref, gpre_ref, gpost_ref, win_ref, wlr_ref, wgate_ref, bgate_ref, ghead_ref, wout_ref,
         o_ref, st_ref) = refs
    step = pl.program_id(1)
    n_chunks = rows // GLA_CHUNK
    blk_rows = min(rows, GLA_ATT_ROWS)
    bsl = [slice(b * blk_rows, (b + 1) * blk_rows) for b in range(rows // blk_rows)]

    @pl.when(step == 0)
    def _():
        st_ref[...] = s0_ref[...]

    mxu = win_ref.dtype
    x = h_ref[0]
    hn = _rmsnorm(x, gpre_ref[...]).astype(mxu)
    lr = _mm(hn, wlr_ref[...])
    gp = _mm(lr.astype(mxu), wgate_ref[...]) + bgate_ref[...]
    gk = (jnp.minimum(gp, 0.0) - jnp.log1p(jnp.exp(-jnp.abs(gp)))) * (1.0 / GLA_TAU)
    q = _mm(hn, win_ref[:, 0:GLA_QK])
    k = _mm(hn, win_ref[:, GLA_QK:2 * GLA_QK])
    if valid_rows is not None:
        live = lax.broadcasted_iota(jnp.int32, (rows, 1), 0) < valid_rows
        gk = jnp.where(live, gk, 0.0)
        k = jnp.where(live, k, 0.0)

    ri = lax.broadcasted_iota(jnp.int32, (blk_rows, blk_rows), 0)
    ci = lax.broadcasted_iota(jnp.int32, (blk_rows, blk_rows), 1)
    intra = ((ri // GLA_CHUNK) == (ci // GLA_CHUNK)) & (ci <= ri)
    tri = jnp.where(intra, 1.0, 0.0).astype(mxu)
    g_hi = gk.astype(mxu)
    rem = gk - g_hi.astype(F32)
    g_mid = rem.astype(mxu)
    g_lo = (rem - g_mid.astype(F32)).astype(mxu)
    g_split = jnp.concatenate([g_hi, g_mid, g_lo], axis=1)
    parts = jnp.concatenate([_mm(tri, g_split[rs]) for rs in bsl], axis=0)
    bcum = parts[:, 0:GLA_QK] + parts[:, GLA_QK:2 * GLA_QK] + parts[:, 2 * GLA_QK:3 * GLA_QK]
    btot = jnp.concatenate(
        [jnp.broadcast_to(bcum[(c + 1) * GLA_CHUNK - 1:(c + 1) * GLA_CHUNK], (GLA_CHUNK, GLA_QK))
         for c in range(n_chunks)], axis=0)

    v = _mm(hn, win_ref[:, 2 * GLA_QK:2 * GLA_QK + GLA_V])
    r = _mm(hn, win_ref[:, 2 * GLA_QK + GLA_V:2 * GLA_QK + 2 * GLA_V])
    q_dec = (q * (GLA_DK ** -0.5) * jnp.exp(bcum)).astype(mxu)
    k_dec = (k * jnp.exp(-bcum)).astype(mxu)
    k_rem = (k * jnp.exp(btot - bcum)).astype(mxu)
    vb = v.astype(mxu)
    decay = [jnp.exp(jnp.broadcast_to(bcum[(c + 1) * GLA_CHUNK - 1:(c + 1) * GLA_CHUNK], (LANES, GLA_QK)).T)
             for c in range(n_chunks)]

    heads = range(GLA_HEADS)
    ksl = [slice(hd * GLA_DK, (hd + 1) * GLA_DK) for hd in heads]
    vsl = [slice(hd * GLA_DV, (hd + 1) * GLA_DV) for hd in heads]
    csl = [slice(c * GLA_CHUNK, (c + 1) * GLA_CHUNK) for c in range(n_chunks)]
    att = [[_mm(q_dec[rs, ksl[hd]], k_dec[rs, ksl[hd]], _NT) for rs in bsl] for hd in heads]
    upd = [[_mm(k_rem[csl[c], ksl[hd]], vb[csl[c], vsl[hd]], _TN) for c in range(n_chunks)] for hd in heads]
    o_intra = [jnp.concatenate([_mm(jnp.where(intra, att[hd][b], 0.0).astype(mxu), vb[rs, vsl[hd]])
                                for b, rs in enumerate(bsl)], axis=0) for hd in heads]

    outs = []
    for hd in heads:
        st = st_ref[hd]
        o_head = []
        for c in range(n_chunks):
            o_head.append(o_intra[hd][csl[c]] + _mm(q_dec[csl[c], ksl[hd]], st.astype(mxu)))
            d = decay[c][ksl[hd], :]
            st = st * jnp.concatenate([d] * (GLA_DV // LANES), axis=1) + upd[hd][c]
        st_ref[hd] = st
        outs.append(jnp.concatenate(o_head, axis=0))

    g_head = ghead_ref[...]
    gated = []
    for hd in heads:
        rg = r[:, vsl[hd]]
        gated.append((_rmsnorm(outs[hd], g_head) * (rg * _sigmoid(rg))).astype(mxu))
    for rs in bsl:
        mix = _mm(gated[0][rs], wout_ref[vsl[0], :])
        for hd in heads[1:]:
            mix = mix + _mm(gated[hd][rs], wout_ref[vsl[hd], :])
        o_ref[0, rs, :] = h_ref[0, rs, :] + _rmsnorm(mix, gpost_ref[...])

    if emit_state:
        @pl.when(step == pl.num_programs(1) - 1)
        def _():
            sfin_ref[0] = st_ref[...]


def _gla(h, s0, g_pre, g_post, w_in, layer, w_lr, w_gate, b_gate, g_head, w_out, *, rows, valid_rows=None,
         emit_state=False):
    b, length, _ = h.shape
    steps = length // rows
    seq_spec = pl.BlockSpec((1, rows, D_MODEL), lambda i, j: (i, j, 0))
    state_shape = (GLA_HEADS, GLA_DK, GLA_DV)
    if w_in.ndim == 3:
        w_in_spec = pl.BlockSpec((None, D_MODEL, GLA_IN_MAIN), lambda i, j: (layer, 0, 0),
                                 pipeline_mode=pl.Buffered(1))
    else:
        w_in_spec = _const_spec((D_MODEL, GLA_IN_MAIN))
    in_specs = [seq_spec, _const_spec(state_shape), _const_spec((1, D_MODEL)), _const_spec((1, D_MODEL)),
                w_in_spec,
                _const_spec((D_MODEL, LANES)), _const_spec((LANES, GLA_QK)), _const_spec((1, GLA_QK)),
                _const_spec((1, GLA_DV)), _const_spec((GLA_V, D_MODEL))]
    out_shape = [jax.ShapeDtypeStruct(h.shape, F32)]
    out_specs = [seq_spec]
    if emit_state:
        out_shape.append(jax.ShapeDtypeStruct((b,) + state_shape, F32))
        out_specs.append(pl.BlockSpec((1,) + state_shape, lambda i, j: (i, 0, 0, 0)))
    return pl.pallas_call(
        functools.partial(_gla_kernel, rows=rows, valid_rows=valid_rows, emit_state=emit_state),
        grid=(b, steps),
        in_specs=in_specs,
        out_specs=out_specs,
        out_shape=out_shape,
        scratch_shapes=[pltpu.VMEM(state_shape, F32)],
        compiler_params=pltpu.CompilerParams(dimension_semantics=("parallel", "arbitrary"),
                                             vmem_limit_bytes=_VMEM_LIMIT),
        name="gla_meta" if emit_state else "gla",
    )(h, s0, g_pre, g_post, w_in, w_lr, w_gate, b_gate, g_head, w_out)


def _swa_kernel(sinks_ref, h_ref, gpre_ref, gpost_ref, wqt_ref, wo_ref, cos_ref, sin_ref,
                kc_ref, kp_ref, vtc_ref, vtp_ref, km_ref, vtm_ref, o_ref, ot_ref, st_ref, pt_ref, *, rows):
    step = pl.program_id(1)
    x = h_ref[0]
    hn = _rmsnorm(x, gpre_ref[...]).astype(BF16)
    qt = _mm(wqt_ref[...], hn, _NT)
    qtb = _rope_rows(qt, cos_ref[...], sin_ref[...], N_Q_HEADS).astype(BF16)

    key_j = lax.broadcasted_iota(jnp.int32, (WINDOW, WINDOW), 0)
    qry_i = lax.broadcasted_iota(jnp.int32, (WINDOW, WINDOW), 1)
    from_prev = key_j > qry_i
    first_prev_ok = key_j > qry_i + jnp.where(step > 0, 0, WINDOW)
    per_pass = min(rows, _SWA_PASS_ROWS) // WINDOW
    for first_blk in range(0, rows // WINDOW, per_pass):
        _swa_pass(first_blk, per_pass, sinks_ref, qtb, from_prev, first_prev_ok, kc_ref, kp_ref, vtc_ref, vtp_ref,
                  km_ref, vtm_ref, ot_ref, st_ref, pt_ref)
        rs = slice(first_blk * WINDOW, (first_blk + per_pass) * WINDOW)
        mix = _mm(ot_ref[:, rs].astype(BF16), wo_ref[...], _TN)
        o_ref[0, rs, :] = h_ref[0, rs, :] + _rmsnorm(mix, gpost_ref[...])


def _swa_pass(first_blk, n_blk, sinks_ref, qtb, from_prev, first_prev_ok, kc_ref, kp_ref, vtc_ref, vtp_ref,
              km_ref, vtm_ref, ot_ref, st_ref, pt_ref):
    units = [(blk, g) for blk in range(first_blk, first_blk + n_blk) for g in range(N_KV_HEADS)]

    for u, (blk, g) in enumerate(units):
        c0 = blk * WINDOW
        l0 = g * LANES
        k_prev = kp_ref[0, :, l0:l0 + HEAD_DIM] if blk == 0 else kc_ref[0, c0 - WINDOW:c0, l0:l0 + HEAD_DIM]
        kcat = jnp.concatenate([km_ref[:, l0:l0 + HEAD_DIM], k_prev,
                                kc_ref[0, c0:c0 + WINDOW, l0:l0 + HEAD_DIM]], axis=0)
        qg = jnp.concatenate([qtb[(GROUP * g + i) * HEAD_DIM:(GROUP * g + i + 1) * HEAD_DIM, c0:c0 + WINDOW]
                              for i in range(GROUP)], axis=1)
        st_ref[u] = _mm(kcat, qg)

    for u, (blk, g) in enumerate(units):
        c0 = blk * WINDOW
        d0 = g * HEAD_DIM
        inv = []
        for i in range(GROUP):
            sink = sinks_ref[GROUP * g + i]
            q0 = i * WINDOW
            s_meta = st_ref[u, 0:N_META, q0:q0 + WINDOW]
            s_prev = st_ref[u, N_META:N_META + WINDOW, q0:q0 + WINDOW]
            s_cur = st_ref[u, N_META + WINDOW:N_META + 2 * WINDOW, q0:q0 + WINDOW]
            if blk == 0:
                s_prev = jnp.where(first_prev_ok, s_prev, NEG_INF)
            s_band = jnp.where(from_prev, s_prev, s_cur)
            m = jnp.maximum(jnp.max(s_band, axis=0, keepdims=True), jnp.max(s_meta, axis=0, keepdims=True))
            m = jnp.maximum(m, sink)
            p_band = jnp.exp(s_band - m)
            p_meta = jnp.exp(s_meta - m)
            den = (jnp.sum(p_band, axis=0, keepdims=True) + jnp.sum(p_meta, axis=0, keepdims=True)
                   + jnp.exp(sink - m))
            inv.append(1.0 / den)
            pt_ref[u, :, q0:q0 + WINDOW] = jnp.concatenate(
                [p_meta, jnp.where(from_prev, p_band, 0.0), jnp.where(from_prev, 0.0, p_band)],
                axis=0).astype(BF16)
        vt_prev = vtp_ref[0, d0:d0 + HEAD_DIM, :] if blk == 0 else vtc_ref[0, d0:d0 + HEAD_DIM, c0 - WINDOW:c0]
        vt_band = jnp.concatenate([vt_prev, vtc_ref[0, d0:d0 + HEAD_DIM, c0:c0 + WINDOW]], axis=1)
        og = (_mm(vt_band, pt_ref[u, N_META:, :])
              + _mm(vtm_ref[d0:d0 + HEAD_DIM, :], pt_ref[u, 0:N_META, :]))
        for i in range(GROUP):
            h0 = (GROUP * g + i) * HEAD_DIM
            ot_ref[h0:h0 + HEAD_DIM, c0:c0 + WINDOW] = og[:, i * WINDOW:(i + 1) * WINDOW] * inv[i]


def _swa(h, sinks, g_pre, g_post, wqt, w_o, cos_t, sin_t, k_pad, vt, k_meta, vt_meta, *, rows):
    b, length, _ = h.shape
    steps = length // rows
    per = rows // WINDOW
    n_units = min(rows, _SWA_PASS_ROWS) // WINDOW * N_KV_HEADS
    seq_spec = pl.BlockSpec((1, rows, D_MODEL), lambda i, j: (i, j, 0))
    rope_spec = pl.BlockSpec((ROT_HALF, rows), lambda i, j: (0, j))
    prev_blk = lambda j: jnp.maximum(j * per - 1, 0)
    in_specs = [
        pl.BlockSpec(memory_space=pltpu.SMEM),
        seq_spec, _const_spec((1, D_MODEL)), _const_spec((1, D_MODEL)),
        _const_spec((D_MODEL, D_MODEL)), _const_spec((D_MODEL, D_MODEL)),
        rope_spec, rope_spec,
        pl.BlockSpec((1, rows, K_PAD), lambda i, j: (i, j, 0)),
        pl.BlockSpec((1, WINDOW, K_PAD), lambda i, j: (i, prev_blk(j), 0)),
        pl.BlockSpec((1, KV_DIM, rows), lambda i, j: (i, 0, j)),
        pl.BlockSpec((1, KV_DIM, WINDOW), lambda i, j: (i, 0, prev_blk(j))),
        _const_spec((N_META, K_PAD)), _const_spec((KV_DIM, N_META)),
    ]
    return pl.pallas_call(
        functools.partial(_swa_kernel, rows=rows),
        grid=(b, steps),
        in_specs=in_specs,
        out_specs=seq_spec,
        out_shape=jax.ShapeDtypeStruct(h.shape, F32),
        scratch_shapes=[pltpu.VMEM((D_MODEL, rows), F32),
                        pltpu.VMEM((n_units, N_META + 2 * WINDOW, GROUP * WINDOW), F32),
                        pltpu.VMEM((n_units, N_META + 2 * WINDOW, GROUP * WINDOW), BF16)],
        compiler_params=pltpu.CompilerParams(dimension_semantics=("parallel", "parallel"),
                                             vmem_limit_bytes=_VMEM_LIMIT),
        name="swa",
    )(sinks, h, g_pre, g_post, wqt, w_o, cos_t, sin_t, k_pad, k_pad, vt, vt, k_meta, vt_meta)


def _rope_angles(positions):
    inv_freq = ROPE_THETA ** (-jnp.arange(0, ROT_DIM, 2, dtype=F32) / ROT_DIM)
    ang = positions.astype(F32)[:, None] * inv_freq[None, :]
    return jnp.cos(ang), jnp.sin(ang)


def _k_rope_tables(cos, sin):
    n = cos.shape[0]
    rest = HEAD_DIM - ROT_DIM
    rc = jnp.concatenate([cos, cos, jnp.ones((n, rest), F32)], axis=1)
    ra = jnp.concatenate([-sin, jnp.zeros((n, ROT_HALF + rest), F32)], axis=1)
    rb = jnp.concatenate([jnp.zeros((n, ROT_HALF), F32), sin, jnp.zeros((n, rest), F32)], axis=1)
    return tuple(jnp.tile(t, (1, LANES // HEAD_DIM)) for t in (rc, ra, rb))


def kernel(x, meta_tokens, norm_gains, w_ffn_in, w_ffn_out, gla_w_in, gla_w_gate, gla_b_gate, gla_norm, gla_w_out,
           kv_norm, w_kv, swa_w_q, swa_sinks, swa_w_out):
    batch, seq, _ = x.shape
    depth = norm_gains.shape[0]
    n_a = gla_w_in.shape[0]
    tm = 1024
    gla_rows = 1024
    swa_rows = 1024

    gains = norm_gains.reshape(depth, 6, 1, D_MODEL)
    n_ffn = batch * seq // tm
    ffn_order = [(layer, which) for layer in range(depth) for which in range(2)]
    ffn_w = {}

    def ffn_main(h, layer, which, g_pre, g_post, **kw):
        pos = ffn_order.index((layer, which))
        nxt = ffn_order[pos + 1] if pos + 1 < len(ffn_order) else None
        cast = (w_ffn_in, w_ffn_out, nxt) if nxt is not None else None
        out = _ffn(h, g_pre, g_post, *ffn_w[(layer, which)], tm=tm, cast_next=cast, **kw)
        if nxt is None:
            return out
        *out, nxt_in, nxt_out = out
        ffn_w[nxt] = (nxt_in, nxt_out)
        return out if len(out) > 1 else out[0]

    h = x.reshape(batch * seq, D_MODEL)
    hm = jnp.pad(meta_tokens.astype(x.dtype), ((0, META_ROWS - N_META), (0, 0)))

    cos_m, sin_m = _rope_angles(jnp.arange(META_ROWS))
    cos_r, sin_r = _rope_angles(jnp.arange(N_META, N_META + seq))
    g_kv = kv_norm.reshape(1, D_MODEL)

    k_pad = vt = k_meta = vt_meta = w_kv_b = kv_w = None
    for layer in range(depth):
        g = gains[layer]
        last = layer == depth - 1
        if layer < n_a:
            hm = _ffn(hm, g[0], g[1], (w_ffn_in, (layer, 0)), (w_ffn_out, (layer, 0)), tm=META_ROWS,
                      cast_self=layer == 0)
            if layer == 0:
                hm, *ffn_w[(0, 0)] = hm
        if (layer, 0) not in ffn_w:
            ffn_w[(layer, 0)] = (w_ffn_in[layer, 0].astype(BF16), w_ffn_out[layer, 0].astype(BF16))
        if layer < n_a:
            a = layer
            side = [_cast_job(gla_w_in, (a,), n_ffn), _cast_job(gla_w_out, (a,), n_ffn)]
            if layer == 0:
                side.append(_cast_job(w_kv, (), n_ffn))
                h, gla_w_in_b, gla_w_out_b, w_kv_b = ffn_main(h, layer, 0, g[0], g[1], side=side)
            else:
                h, gla_w_in_b, gla_w_out_b = ffn_main(h, layer, 0, g[0], g[1], side=side)
            w_lr = jnp.pad(gla_w_in[a, :, GLA_IN_MAIN:], ((0, 0), (0, LANES - GLA_RANK)))
            w_gate_a = jnp.pad(gla_w_gate[a], ((0, LANES - GLA_RANK), (0, 0)))
            gla_vecs = (gla_b_gate[a].reshape(1, GLA_QK), gla_norm[a].reshape(1, GLA_DV))
            s_zero = jnp.zeros((GLA_HEADS, GLA_DK, GLA_DV), F32)
            hm3, s_meta = _gla(hm[None], s_zero, g[2], g[3], gla_w_in, a, w_lr, w_gate_a, *gla_vecs, gla_w_out[a],
                               rows=META_ROWS, valid_rows=N_META, emit_state=True)
            hm = hm3[0]
            h = _gla(h.reshape(batch, seq, D_MODEL), s_meta[0], g[2], g[3], gla_w_in_b, a, w_lr.astype(BF16),
                     w_gate_a.astype(BF16), *gla_vecs, gla_w_out_b, rows=gla_rows)[0]
            h = h.reshape(batch * seq, D_MODEL)
        else:
            b = layer - n_a
            side = [_transpose_job(swa_w_q, (b,), n_ffn, slice(0, D_MODEL), HEAD_DIM ** -0.5),
                    _cast_job(swa_w_out, (b,), n_ffn)]
            if b == 0:
                h, k_pad, vt, wqt, w_o = ffn_main(h, layer, 0, g[0], g[1], side=side, kv_of="in",
                                                  kv=kv_w + _k_rope_tables(cos_r, sin_r), tiles_per_seq=seq // tm)
            else:
                h, wqt, w_o = ffn_main(h, layer, 0, g[0], g[1], side=side)
            h = _swa(h.reshape(batch, seq, D_MODEL), swa_sinks[b], g[2], g[3], wqt, w_o,
                     cos_r.T, sin_r.T, k_pad.reshape(batch, seq, K_PAD), vt, k_meta, vt_meta, rows=swa_rows)
            h = h.reshape(batch * seq, D_MODEL)
        if layer == n_a - 1:
            kv_w = (g_kv, w_kv_b)
            h = ffn_main(h, layer, 1, g[4], g[5])
            _, k_m, vt_m = _ffn(hm, g[4], g[5], *ffn_w[(layer, 1)], tm=META_ROWS,
                                kv=kv_w + _k_rope_tables(cos_m, sin_m))
            k_meta = k_m[:N_META]
            vt_meta = vt_m[0, :, :N_META]
        else:
            h = ffn_main(h, layer, 1, g[4], g[5])
            if not last and layer < n_a:
                hm = _ffn(hm, g[4], g[5], (w_ffn_in, (layer, 1)), (w_ffn_out, (layer, 1)), tm=META_ROWS)
    return h.reshape(batch, seq, D_MODEL)
```

```python
import functools

import jax
import jax.numpy as jnp
from jax import lax
from jax.experimental import pallas as pl
from jax.experimental.pallas import tpu as pltpu

F32 = jnp.float32
BF16 = jnp.bfloat16

D_MODEL = 1024
D_FF = 2816
N_META = 16
EPS = 1e-6
NEG_INF = -1e30

GLA_HEADS = 4
GLA_DK = 128
GLA_DV = 256
GLA_QK = GLA_HEADS * GLA_DK
GLA_V = GLA_HEADS * GLA_DV
GLA_RANK = 16
GLA_TAU = 16.0
GLA_CHUNK = 64
GLA_IN_MAIN = 2 * GLA_QK + 2 * GLA_V
GLA_ATT_ROWS = 256

N_Q_HEADS = 16
N_KV_HEADS = 4
GROUP = N_Q_HEADS // N_KV_HEADS
HEAD_DIM = 64
WINDOW = 128
ROT_DIM = HEAD_DIM // 4
ROT_HALF = ROT_DIM // 2
ROPE_THETA = 500000.0
KV_DIM = N_KV_HEADS * HEAD_DIM
LANES = 128
K_PAD = N_KV_HEADS * LANES

META_ROWS = 64

_VMEM_LIMIT = 56 * 1024 * 1024

_FF_CHUNKS = ((0, 768), (768, 1536), (1536, 2304), (2304, 2816))
_FFN_SUB_ROWS = 512
_KV_LAG_CHUNKS = 2
_BF16_SUBLANES = 16
_SWA_PASS_ROWS = 512

_LOG2_E = 1.4426950408889634

_NT = (((1,), (1,)), ((), ()))
_TN = (((0,), (0,)), ((), ()))


def _rmsnorm(x, g):
    return x * lax.rsqrt(jnp.mean(x * x, axis=-1, keepdims=True) + EPS) * g


def _sigmoid(x):
    return 1.0 / (1.0 + jnp.exp(-x))


def _mm(a, b, dims=None):
    precision = lax.Precision.HIGHEST if a.dtype == F32 else None
    if dims is None:
        return jnp.dot(a, b, preferred_element_type=F32, precision=precision)
    return lax.dot_general(a, b, dims, preferred_element_type=F32, precision=precision)


def _const_spec(shape):
    zeros = (0,) * len(shape)
    return pl.BlockSpec(shape, lambda *_: zeros, pipeline_mode=pl.Buffered(1))


def _rope_rows(xt, c, s, n_heads):
    parts = []
    for h in range(n_heads):
        base = h * HEAD_DIM
        x1 = xt[base:base + ROT_HALF]
        x2 = xt[base + ROT_HALF:base + ROT_DIM]
        parts += [x1 * c - x2 * s, x2 * c + x1 * s, xt[base + ROT_DIM:base + HEAD_DIM]]
    return jnp.concatenate(parts, axis=0)


def _ffn_kernel(*refs, with_kv, cast_self, side_fns):
    refs = list(refs)
    act_ref = refs.pop()
    n_side = len(side_fns)
    side_dst = refs[len(refs) - n_side:]
    del refs[len(refs) - n_side:]
    if cast_self:
        self_dst = refs[-2:]
        del refs[-2:]
    if with_kv:
        k_ref, vt_ref = refs[-2:]
        del refs[-2:]
    o_ref = refs.pop()
    side_src = refs[len(refs) - n_side:]
    del refs[len(refs) - n_side:]
    if with_kv:
        gkv_ref, wkv_ref, rc_ref, ra_ref, rb_ref = refs[-5:]
        del refs[-5:]
    h_ref, gpre_ref, gpost_ref, win_ref, wout_ref = refs
    for fn, src, dst in zip(side_fns, side_src, side_dst):
        dst[...] = fn(src[...]).astype(dst.dtype)
    if cast_self:
        for src, dst in zip((win_ref, wout_ref), self_dst):
            dst[...] = src[...].astype(dst.dtype)
    mxu = win_ref.dtype
    tm = h_ref.shape[0]
    n_sub = 2 if tm >= 2 * _FFN_SUB_ROWS else 1
    sub = tm // n_sub

    def shared_kv(rows, h_new):
        kn = _rmsnorm(h_new, gkv_ref[...]).astype(mxu)
        kv = _mm(kn, wkv_ref[...])
        rc = rc_ref[rows, :]
        ra = ra_ref[rows, :]
        rb = rb_ref[rows, :]
        low = lax.broadcasted_iota(jnp.int32, (sub, LANES), 1) < HEAD_DIM
        cols = []
        for grp in range(KV_DIM // LANES):
            kg = kv[:, grp * LANES:(grp + 1) * LANES]
            kg = (kg * rc + pltpu.roll(kg, LANES - ROT_HALF, axis=1) * ra + pltpu.roll(kg, ROT_HALF, axis=1) * rb)
            cols += [jnp.where(low, kg, 0.0), jnp.where(low, pltpu.roll(kg, HEAD_DIM, axis=1), 0.0)]
        k_ref[rows, :] = jnp.concatenate(cols, axis=1).astype(k_ref.dtype)
        vt_ref[0, :, rows] = kv[:, KV_DIM:].T.astype(vt_ref.dtype)

    pending_kv = None
    for blk in range(n_sub):
        rows = slice(blk * sub, (blk + 1) * sub)
        x = h_ref[rows, :]
        if with_kv == "in":
            pending_kv = (rows, x)
        xn = _rmsnorm(x, gpre_ref[...]).astype(mxu)
        for ci, (lo, hi) in enumerate(_FF_CHUNKS):
            gate = _mm(xn, win_ref[:, lo:hi])
            up = _mm(xn, win_ref[:, D_FF + lo:D_FF + hi])
            act_ref[rows, lo:hi] = (gate * _sigmoid(gate) * up).astype(mxu)
            if pending_kv is not None and ci == _KV_LAG_CHUNKS - 1:
                shared_kv(*pending_kv)
                pending_kv = None
        y = _mm(act_ref[rows, :], wout_ref[...])
        h_new = x + 0.5 * _rmsnorm(y, gpost_ref[...])
        o_ref[rows, :] = h_new
        if with_kv == "out":
            pending_kv = (rows, h_new)
    if pending_kv is not None:
        shared_kv(*pending_kv)


def _weight_arg(w, shape):
    if isinstance(w, tuple):
        stack, sel = w
        return stack, pl.BlockSpec((None, None) + shape, lambda *_: (sel[0], sel[1], 0, 0),
                                   pipeline_mode=pl.Buffered(1))
    return w, _const_spec(shape)


def _cast_job(src, lead, n):
    shape = src.shape[len(lead):]
    per = 1
    while (shape[0] * per // n) % _BF16_SUBLANES:
        per *= 2
    blk = (shape[0] * per // n, shape[1])
    in_spec = pl.BlockSpec((None,) * len(lead) + blk, lambda i: tuple(lead) + (i // per, 0))
    return src, in_spec, jax.ShapeDtypeStruct(shape, BF16), pl.BlockSpec(blk, lambda i: (i // per, 0)), lambda x: x


def _transpose_job(src, lead, n, cols, scale=None):
    rows, width = src.shape[len(lead):]
    per = n * LANES // rows
    n_out = cols.stop - cols.start
    in_spec = pl.BlockSpec((None,) * len(lead) + (LANES, width), lambda i: tuple(lead) + (i // per, 0))
    out_spec = pl.BlockSpec((n_out, LANES), lambda i: (0, i // per))
    fn = lambda x: (x[:, cols] if scale is None else x[:, cols] * scale).T
    return src, in_spec, jax.ShapeDtypeStruct((n_out, rows), BF16), out_spec, fn


def _ffn(h, g_pre, g_post, w_in, w_out, *, tm, kv=None, kv_of="out", tiles_per_seq=1, cast_next=None,
         cast_self=False, side=()):
    t = h.shape[0]
    n = t // tm
    row_spec = pl.BlockSpec((tm, D_MODEL), lambda i: (i, 0))
    w_in, w_in_spec = _weight_arg(w_in, (D_MODEL, 2 * D_FF))
    w_out, w_out_spec = _weight_arg(w_out, (D_FF, D_MODEL))
    in_specs = [row_spec, _const_spec((1, D_MODEL)), _const_spec((1, D_MODEL)), w_in_spec, w_out_spec]
    args = [h, g_pre, g_post, w_in, w_out]
    out_shape = [jax.ShapeDtypeStruct((t, D_MODEL), F32)]
    out_specs = [row_spec]
    if kv is not None:
        g_kv, wkv, rc, ra, rb = kv
        rope_spec = pl.BlockSpec((tm, LANES), lambda i: (i % tiles_per_seq, 0))
        in_specs += [_const_spec((1, D_MODEL)), _const_spec((D_MODEL, 2 * KV_DIM)),
                     rope_spec, rope_spec, rope_spec]
        args += [g_kv, wkv, rc, ra, rb]
        n_seq = n // tiles_per_seq
        out_shape += [jax.ShapeDtypeStruct((t, K_PAD), BF16),
                      jax.ShapeDtypeStruct((n_seq, KV_DIM, tiles_per_seq * tm), BF16)]
        out_specs += [pl.BlockSpec((tm, K_PAD), lambda i: (i, 0)),
                      pl.BlockSpec((1, KV_DIM, tm), lambda i: (i // tiles_per_seq, 0, i % tiles_per_seq))]
    if cast_self:
        assert n == 1
        for shape in ((D_MODEL, 2 * D_FF), (D_FF, D_MODEL)):
            out_shape.append(jax.ShapeDtypeStruct(shape, BF16))
            out_specs.append(pl.BlockSpec(shape, lambda i: (0, 0)))
    side = list(side)
    if cast_next is not None:
        nxt_in, nxt_out, sel = cast_next
        side += [_cast_job(nxt_in, sel, n), _cast_job(nxt_out, sel, n)]
    for src, in_spec, out_sds, out_spec, _ in side:
        args.append(src)
        in_specs.append(in_spec)
        out_shape.append(out_sds)
        out_specs.append(out_spec)
    out = pl.pallas_call(
        functools.partial(_ffn_kernel, with_kv=kv_of if kv is not None else None, cast_self=cast_self,
                          side_fns=tuple(job[-1] for job in side)),
        grid=(n,),
        in_specs=in_specs,
        out_specs=out_specs,
        out_shape=out_shape,
        scratch_shapes=[pltpu.VMEM((tm, D_FF), w_in.dtype)],
        compiler_params=pltpu.CompilerParams(dimension_semantics=("arbitrary",), vmem_limit_bytes=_VMEM_LIMIT),
        name="ffn_kv" if kv is not None else "ffn",
    )(*args)
    return out if len(out) > 1 else out[0]


def _gla_kernel(*refs, rows, valid_rows, emit_state):
    if emit_state:
        (h_ref, s0_ref, gpre_ref, gpost_ref, win_ref, wlr_ref, wgate_ref, bgate_ref, ghead_ref, wout_ref,
         o_ref, sfin_ref, st_ref) = refs
    else:
        (h_ref, s0_ref, gpre_ref, gpost_ref, win_ref, wlr_ref, wgate_ref, bgate_ref, ghead_ref, wout_ref,
         o_ref, st_ref) = refs
    step = pl.program_id(1)
    n_chunks = rows // GLA_CHUNK
    blk_rows = min(rows, GLA_ATT_ROWS)
    bsl = [slice(b * blk_rows, (b + 1) * blk_rows) for b in range(rows // blk_rows)]

    @pl.when(step == 0)
    def _():
        st_ref[...] = s0_ref[...]

    mxu = win_ref.dtype
    x = h_ref[0]
    hn = _rmsnorm(x, gpre_ref[...]).astype(mxu)
    lr = _mm(hn, wlr_ref[...])
    gp = _mm(lr.astype(mxu), wgate_ref[...]) + bgate_ref[...]
    gk = (jnp.minimum(gp, 0.0) - jnp.log1p(jnp.exp(-jnp.abs(gp)))) * (_LOG2_E / GLA_TAU)
    q = _mm(hn, win_ref[:, 0:GLA_QK])
    k = _mm(hn, win_ref[:, GLA_QK:2 * GLA_QK])
    if valid_rows is not None:
        live = lax.broadcasted_iota(jnp.int32, (rows, 1), 0) < valid_rows
        gk = jnp.where(live, gk, 0.0)
        k = jnp.where(live, k, 0.0)

    ri = lax.broadcasted_iota(jnp.int32, (blk_rows, blk_rows), 0)
    ci = lax.broadcasted_iota(jnp.int32, (blk_rows, blk_rows), 1)
    intra = ((ri // GLA_CHUNK) == (ci // GLA_CHUNK)) & (ci <= ri)
    tri = jnp.where(intra, 1.0, 0.0).astype(mxu)
    g_hi = gk.astype(mxu)
    rem = gk - g_hi.astype(F32)
    g_mid = rem.astype(mxu)
    g_lo = (rem - g_mid.astype(F32)).astype(mxu)
    g_split = jnp.concatenate([g_hi, g_mid, g_lo], axis=1)
    parts = jnp.concatenate([_mm(tri, g_split[rs]) for rs in bsl], axis=0)
    bcum = parts[:, 0:GLA_QK] + parts[:, GLA_QK:2 * GLA_QK] + parts[:, 2 * GLA_QK:3 * GLA_QK]
    btot = jnp.concatenate(
        [jnp.broadcast_to(bcum[(c + 1) * GLA_CHUNK - 1:(c + 1) * GLA_CHUNK], (GLA_CHUNK, GLA_QK))
         for c in range(n_chunks)], axis=0)

    v = _mm(hn, win_ref[:, 2 * GLA_QK:2 * GLA_QK + GLA_V])
    r = _mm(hn, win_ref[:, 2 * GLA_QK + GLA_V:2 * GLA_QK + 2 * GLA_V])
    q_dec = (q * (GLA_DK ** -0.5) * jnp.exp2(bcum)).astype(mxu)
    k_dec = (k * jnp.exp2(-bcum)).astype(mxu)
    k_rem = (k * jnp.exp2(btot - bcum)).astype(mxu)
    vb = v.astype(mxu)
    decay = [jnp.exp2(jnp.broadcast_to(bcum[(c + 1) * GLA_CHUNK - 1:(c + 1) * GLA_CHUNK], (LANES, GLA_QK)).T)
             for c in range(n_chunks)]

    heads = range(GLA_HEADS)
    ksl = [slice(hd * GLA_DK, (hd + 1) * GLA_DK) for hd in heads]
    vsl = [slice(hd * GLA_DV, (hd + 1) * GLA_DV) for hd in heads]
    csl = [slice(c * GLA_CHUNK, (c + 1) * GLA_CHUNK) for c in range(n_chunks)]
    att = [[_mm(q_dec[rs, ksl[hd]], k_dec[rs, ksl[hd]], _NT) for rs in bsl] for hd in heads]
    upd = [[_mm(k_rem[csl[c], ksl[hd]], vb[csl[c], vsl[hd]], _TN) for c in range(n_chunks)] for hd in heads]
    o_intra = [jnp.concatenate([_mm(jnp.where(intra, att[hd][b], 0.0).astype(mxu), vb[rs, vsl[hd]])
                                for b, rs in enumerate(bsl)], axis=0) for hd in heads]

    outs = []
    for hd in heads:
        st = st_ref[hd]
        o_head = []
        for c in range(n_chunks):
            o_head.append(o_intra[hd][csl[c]] + _mm(q_dec[csl[c], ksl[hd]], st.astype(mxu)))
            d = decay[c][ksl[hd], :]
            st = st * jnp.concatenate([d] * (GLA_DV // LANES), axis=1) + upd[hd][c]
        st_ref[hd] = st
        outs.append(jnp.concatenate(o_head, axis=0))

    g_head = ghead_ref[...]
    gated = []
    for hd in heads:
        rg = r[:, vsl[hd]]
        gated.append((_rmsnorm(outs[hd], g_head) * (rg * _sigmoid(rg))).astype(mxu))
    for rs in bsl:
        mix = _mm(gated[0][rs], wout_ref[vsl[0], :])
        for hd in heads[1:]:
            mix = mix + _mm(gated[hd][rs], wout_ref[vsl[hd], :])
        o_ref[0, rs, :] = h_ref[0, rs, :] + _rmsnorm(mix, gpost_ref[...])

    if emit_state:
        @pl.when(step == pl.num_programs(1) - 1)
        def _():
            sfin_ref[0] = st_ref[...]


def _gla(h, s0, g_pre, g_post, w_in, layer, w_lr, w_gate, b_gate, g_head, w_out, *, rows, valid_rows=None,
         emit_state=False):
    b, length, _ = h.shape
    steps = length // rows
    seq_spec = pl.BlockSpec((1, rows, D_MODEL), lambda i, j: (i, j, 0))
    state_shape = (GLA_HEADS, GLA_DK, GLA_DV)
    if w_in.ndim == 3:
        w_in_spec = pl.BlockSpec((None, D_MODEL, GLA_IN_MAIN), lambda i, j: (layer, 0, 0),
                                 pipeline_mode=pl.Buffered(1))
    else:
        w_in_spec = _const_spec((D_MODEL, GLA_IN_MAIN))
    in_specs = [seq_spec, _const_spec(state_shape), _const_spec((1, D_MODEL)), _const_spec((1, D_MODEL)),
                w_in_spec,
                _const_spec((D_MODEL, LANES)), _const_spec((LANES, GLA_QK)), _const_spec((1, GLA_QK)),
                _const_spec((1, GLA_DV)), _const_spec((GLA_V, D_MODEL))]
    out_shape = [jax.ShapeDtypeStruct(h.shape, F32)]
    out_specs = [seq_spec]
    if emit_state:
        out_shape.append(jax.ShapeDtypeStruct((b,) + state_shape, F32))
        out_specs.append(pl.BlockSpec((1,) + state_shape, lambda i, j: (i, 0, 0, 0)))
    return pl.pallas_call(
        functools.partial(_gla_kernel, rows=rows, valid_rows=valid_rows, emit_state=emit_state),
        grid=(b, steps),
        in_specs=in_specs,
        out_specs=out_specs,
        out_shape=out_shape,
        scratch_shapes=[pltpu.VMEM(state_shape, F32)],
        compiler_params=pltpu.CompilerParams(dimension_semantics=("parallel", "arbitrary"),
                                             vmem_limit_bytes=_VMEM_LIMIT),
        name="gla_meta" if emit_state else "gla",
    )(h, s0, g_pre, g_post, w_in, w_lr, w_gate, b_gate, g_head, w_out)


def _swa_kernel(sinks_ref, h_ref, gpre_ref, gpost_ref, wqt_ref, wo_ref, cos_ref, sin_ref,
                kc_ref, kp_ref, vtc_ref, vtp_ref, km_ref, vtm_ref, o_ref, ot_ref, st_ref, pt_ref, *, rows):
    step = pl.program_id(1)
    x = h_ref[0]
    hn = _rmsnorm(x, gpre_ref[...]).astype(BF16)
    qt = _mm(wqt_ref[...], hn, _NT)
    qtb = _rope_rows(qt, cos_ref[...], sin_ref[...], N_Q_HEADS).astype(BF16)

    key_j = lax.broadcasted_iota(jnp.int32, (WINDOW, WINDOW), 0)
    qry_i = lax.broadcasted_iota(jnp.int32, (WINDOW, WINDOW), 1)
    from_prev = key_j > qry_i
    first_prev_ok = key_j > qry_i + jnp.where(step > 0, 0, WINDOW)
    per_pass = min(rows, _SWA_PASS_ROWS) // WINDOW
    for first_blk in range(0, rows // WINDOW, per_pass):
        _swa_pass(first_blk, per_pass, sinks_ref, qtb, from_prev, first_prev_ok, kc_ref, kp_ref, vtc_ref, vtp_ref,
                  km_ref, vtm_ref, ot_ref, st_ref, pt_ref)
        rs = slice(first_blk * WINDOW, (first_blk + per_pass) * WINDOW)
        mix = _mm(ot_ref[:, rs].astype(BF16), wo_ref[...], _TN)
        o_ref[0, rs, :] = h_ref[0, rs, :] + _rmsnorm(mix, gpost_ref[...])


def _swa_pass(first_blk, n_blk, sinks_ref, qtb, from_prev, first_prev_ok, kc_ref, kp_ref, vtc_ref, vtp_ref,
              km_ref, vtm_ref, ot_ref, st_ref, pt_ref):
    units = [(blk, g) for blk in range(first_blk, first_blk + n_blk) for g in range(N_KV_HEADS)]

    for u, (blk, g) in enumerate(units):
        c0 = blk * WINDOW
        l0 = g * LANES
        k_prev = kp_ref[0, :, l0:l0 + HEAD_DIM] if blk == 0 else kc_ref[0, c0 - WINDOW:c0, l0:l0 + HEAD_DIM]
        kcat = jnp.concatenate([km_ref[:, l0:l0 + HEAD_DIM], k_prev,
                                kc_ref[0, c0:c0 + WINDOW, l0:l0 + HEAD_DIM]], axis=0)
        qg = jnp.concatenate([qtb[(GROUP * g + i) * HEAD_DIM:(GROUP * g + i + 1) * HEAD_DIM, c0:c0 + WINDOW]
                              for i in range(GROUP)], axis=1)
        st_ref[u] = _mm(kcat, qg)

    for u, (blk, g) in enumerate(units):
        c0 = blk * WINDOW
        d0 = g * HEAD_DIM
        inv = []
        for i in range(GROUP):
            sink = sinks_ref[GROUP * g + i] * _LOG2_E
            q0 = i * WINDOW
            s_meta = st_ref[u, 0:N_META, q0:q0 + WINDOW]
            s_prev = st_ref[u, N_META:N_META + WINDOW, q0:q0 + WINDOW]
            s_cur = st_ref[u, N_META + WINDOW:N_META + 2 * WINDOW, q0:q0 + WINDOW]
            if blk == 0:
                s_prev = jnp.where(first_prev_ok, s_prev, NEG_INF)
            s_band = jnp.where(from_prev, s_prev, s_cur)
            m = jnp.maximum(jnp.max(s_band, axis=0, keepdims=True), jnp.max(s_meta, axis=0, keepdims=True))
            m = jnp.maximum(m, sink)
            p_band = jnp.exp2(s_band - m)
            p_meta = jnp.exp2(s_meta - m)
            den = (jnp.sum(p_band, axis=0, keepdims=True) + jnp.sum(p_meta, axis=0, keepdims=True)
                   + jnp.exp2(sink - m))
            inv.append(1.0 / den)
            pt_ref[u, :, q0:q0 + WINDOW] = jnp.concatenate(
                [p_meta, jnp.where(from_prev, p_band, 0.0), jnp.where(from_prev, 0.0, p_band)],
                axis=0).astype(BF16)
        vt_prev = vtp_ref[0, d0:d0 + HEAD_DIM, :] if blk == 0 else vtc_ref[0, d0:d0 + HEAD_DIM, c0 - WINDOW:c0]
        vt_band = jnp.concatenate([vt_prev, vtc_ref[0, d0:d0 + HEAD_DIM, c0:c0 + WINDOW]], axis=1)
        og = (_mm(vt_band, pt_ref[u, N_META:, :])
              + _mm(vtm_ref[d0:d0 + HEAD_DIM, :], pt_ref[u, 0:N_META, :]))
        for i in range(GROUP):
            h0 = (GROUP * g + i) * HEAD_DIM
            ot_ref[h0:h0 + HEAD_DIM, c0:c0 + WINDOW] = og[:, i * WINDOW:(i + 1) * WINDOW] * inv[i]


def _swa(h, sinks, g_pre, g_post, wqt, w_o, cos_t, sin_t, k_pad, vt, k_meta, vt_meta, *, rows):
    b, length, _ = h.shape
    steps = length // rows
    per = rows // WINDOW
    n_units = min(rows, _SWA_PASS_ROWS) // WINDOW * N_KV_HEADS
    seq_spec = pl.BlockSpec((1, rows, D_MODEL), lambda i, j: (i, j, 0))
    rope_spec = pl.BlockSpec((ROT_HALF, rows), lambda i, j: (0, j))
    prev_blk = lambda j: jnp.maximum(j * per - 1, 0)
    in_specs = [
        pl.BlockSpec(memory_space=pltpu.SMEM),
        seq_spec, _const_spec((1, D_MODEL)), _const_spec((1, D_MODEL)),
        _const_spec((D_MODEL, D_MODEL)), _const_spec((D_MODEL, D_MODEL)),
        rope_spec, rope_spec,
        pl.BlockSpec((1, rows, K_PAD), lambda i, j: (i, j, 0)),
        pl.BlockSpec((1, WINDOW, K_PAD), lambda i, j: (i, prev_blk(j), 0)),
        pl.BlockSpec((1, KV_DIM, rows), lambda i, j: (i, 0, j)),
        pl.BlockSpec((1, KV_DIM, WINDOW), lambda i, j: (i, 0, prev_blk(j))),
        _const_spec((N_META, K_PAD)), _const_spec((KV_DIM, N_META)),
    ]
    return pl.pallas_call(
        functools.partial(_swa_kernel, rows=rows),
        grid=(b, steps),
        in_specs=in_specs,
        out_specs=seq_spec,
        out_shape=jax.ShapeDtypeStruct(h.shape, F32),
        scratch_shapes=[pltpu.VMEM((D_MODEL, rows), F32),
                        pltpu.VMEM((n_units, N_META + 2 * WINDOW, GROUP * WINDOW), F32),
                        pltpu.VMEM((n_units, N_META + 2 * WINDOW, GROUP * WINDOW), BF16)],
        compiler_params=pltpu.CompilerParams(dimension_semantics=("parallel", "parallel"),
                                             vmem_limit_bytes=_VMEM_LIMIT),
        name="swa",
    )(sinks, h, g_pre, g_post, wqt, w_o, cos_t, sin_t, k_pad, k_pad, vt, vt, k_meta, vt_meta)


def _rope_angles(positions):
    inv_freq = ROPE_THETA ** (-jnp.arange(0, ROT_DIM, 2, dtype=F32) / ROT_DIM)
    ang = positions.astype(F32)[:, None] * inv_freq[None, :]
    return jnp.cos(ang), jnp.sin(ang)


def _k_rope_tables(cos, sin):
    n = cos.shape[0]
    rest = HEAD_DIM - ROT_DIM
    rc = jnp.concatenate([cos, cos, jnp.ones((n, rest), F32)], axis=1)
    ra = jnp.concatenate([-sin, jnp.zeros((n, ROT_HALF + rest), F32)], axis=1)
    rb = jnp.concatenate([jnp.zeros((n, ROT_HALF), F32), sin, jnp.zeros((n, rest), F32)], axis=1)
    return tuple(jnp.tile(t, (1, LANES // HEAD_DIM)) for t in (rc, ra, rb))


def kernel(x, meta_tokens, norm_gains, w_ffn_in, w_ffn_out, gla_w_in, gla_w_gate, gla_b_gate, gla_norm, gla_w_out,
           kv_norm, w_kv, swa_w_q, swa_sinks, swa_w_out):
    batch, seq, _ = x.shape
    depth = norm_gains.shape[0]
    n_a = gla_w_in.shape[0]
    tm = 1024
    gla_rows = 1024
    swa_rows = 1024

    gains = norm_gains.reshape(depth, 6, 1, D_MODEL)
    n_ffn = batch * seq // tm
    ffn_order = [(layer, which) for layer in range(depth) for which in range(2)]
    ffn_w = {}

    def ffn_main(h, layer, which, g_pre, g_post, **kw):
        pos = ffn_order.index((layer, which))
        nxt = ffn_order[pos + 1] if pos + 1 < len(ffn_order) else None
        cast = (w_ffn_in, w_ffn_out, nxt) if nxt is not None else None
        out = _ffn(h, g_pre, g_post, *ffn_w[(layer, which)], tm=tm, cast_next=cast, **kw)
        if nxt is None:
            return out
        *out, nxt_in, nxt_out = out
        ffn_w[nxt] = (nxt_in, nxt_out)
        return out if len(out) > 1 else out[0]

    h = x.reshape(batch * seq, D_MODEL)
    hm = jnp.pad(meta_tokens.astype(x.dtype), ((0, META_ROWS - N_META), (0, 0)))

    cos_m, sin_m = _rope_angles(jnp.arange(META_ROWS))
    cos_r, sin_r = _rope_angles(jnp.arange(N_META, N_META + seq))
    g_kv = kv_norm.reshape(1, D_MODEL)

    k_pad = vt = k_meta = vt_meta = w_kv_b = kv_w = None
    for layer in range(depth):
        g = gains[layer]
        last = layer == depth - 1
        if layer < n_a:
            hm = _ffn(hm, g[0], g[1], (w_ffn_in, (layer, 0)), (w_ffn_out, (layer, 0)), tm=META_ROWS,
                      cast_self=layer == 0)
            if layer == 0:
                hm, *ffn_w[(0, 0)] = hm
        if (layer, 0) not in ffn_w:
            ffn_w[(layer, 0)] = (w_ffn_in[layer, 0].astype(BF16), w_ffn_out[layer, 0].astype(BF16))
        if layer < n_a:
            a = layer
            side = [_cast_job(gla_w_in, (a,), n_ffn), _cast_job(gla_w_out, (a,), n_ffn)]
            if layer == 0:
                side.append(_cast_job(w_kv, (), n_ffn))
                h, gla_w_in_b, gla_w_out_b, w_kv_b = ffn_main(h, layer, 0, g[0], g[1], side=side)
            else:
                h, gla_w_in_b, gla_w_out_b = ffn_main(h, layer, 0, g[0], g[1], side=side)
            w_lr = jnp.pad(gla_w_in[a, :, GLA_IN_MAIN:], ((0, 0), (0, LANES - GLA_RANK)))
            w_gate_a = jnp.pad(gla_w_gate[a], ((0, LANES - GLA_RANK), (0, 0)))
            gla_vecs = (gla_b_gate[a].reshape(1, GLA_QK), gla_norm[a].reshape(1, GLA_DV))
            s_zero = jnp.zeros((GLA_HEADS, GLA_DK, GLA_DV), F32)
            hm3, s_meta = _gla(hm[None], s_zero, g[2], g[3], gla_w_in, a, w_lr, w_gate_a, *gla_vecs, gla_w_out[a],
                               rows=META_ROWS, valid_rows=N_META, emit_state=True)
            hm = hm3[0]
            h = _gla(h.reshape(batch, seq, D_MODEL), s_meta[0], g[2], g[3], gla_w_in_b, a, w_lr.astype(BF16),
                     w_gate_a.astype(BF16), *gla_vecs, gla_w_out_b, rows=gla_rows)[0]
            h = h.reshape(batch * seq, D_MODEL)
        else:
            b = layer - n_a
            side = [_transpose_job(swa_w_q, (b,), n_ffn, slice(0, D_MODEL), HEAD_DIM ** -0.5 * _LOG2_E),
                    _cast_job(swa_w_out, (b,), n_ffn)]
            if b == 0:
                h, k_pad, vt, wqt, w_o = ffn_main(h, layer, 0, g[0], g[1], side=side, kv_of="in",
                                                  kv=kv_w + _k_rope_tables(cos_r, sin_r), tiles_per_seq=seq // tm)
            else:
                h, wqt, w_o = ffn_main(h, layer, 0, g[0], g[1], side=side)
            h = _swa(h.reshape(batch, seq, D_MODEL), swa_sinks[b], g[2], g[3], wqt, w_o,
                     cos_r.T, sin_r.T, k_pad.reshape(batch, seq, K_PAD), vt, k_meta, vt_meta, rows=swa_rows)
            h = h.reshape(batch * seq, D_MODEL)
        if layer == n_a - 1:
            kv_w = (g_kv, w_kv_b)
            h = ffn_main(h, layer, 1, g[4], g[5])
            _, k_m, vt_m = _ffn(hm, g[4], g[5], *ffn_w[(layer, 1)], tm=META_ROWS,
                                kv=kv_w + _k_rope_tables(cos_m, sin_m))
            k_meta = k_m[:N_META]
            vt_meta = vt_m[0, :, :N_META]
        else:
            h = ffn_main(h, layer, 1, g[4], g[5])
            if not last and layer < n_a:
                hm = _ffn(hm, g[4], g[5], (w_ffn_in, (layer, 1)), (w_ffn_out, (layer, 1)), tm=META_ROWS)
    return h.reshape(batch, seq, D_MODEL)
```

```python
import functools

import jax
import jax.numpy as jnp
from jax import lax
from jax.experimental import pallas as pl
from jax.experimental.pallas import tpu as pltpu

F32 = jnp.float32
BF16 = jnp.bfloat16

D_MODEL = 1024
D_FF = 2816
N_META = 16
EPS = 1e-6
NEG_INF = -1e30

GLA_HEADS = 4
GLA_DK = 128
GLA_DV = 256
GLA_QK = GLA_HEADS * GLA_DK
GLA_V = GLA_HEADS * GLA_DV
GLA_RANK = 16
GLA_TAU = 16.0
GLA_CHUNK = 64
GLA_IN_MAIN = 2 * GLA_QK + 2 * GLA_V
GLA_ATT_ROWS = 256

N_Q_HEADS = 16
N_KV_HEADS = 4
GROUP = N_Q_HEADS // N_KV_HEADS
HEAD_DIM = 64
WINDOW = 128
ROT_DIM = HEAD_DIM // 4
ROT_HALF = ROT_DIM // 2
ROPE_THETA = 500000.0
KV_DIM = N_KV_HEADS * HEAD_DIM
LANES = 128
K_PAD = N_KV_HEADS * LANES

META_ROWS = 64

_VMEM_LIMIT = 56 * 1024 * 1024

_FF_CHUNKS = ((0, 768), (768, 1536), (1536, 2304), (2304, 2816))
_FFN_SUB_ROWS = 512
_KV_LAG_CHUNKS = 2
_BF16_SUBLANES = 16
_SWA_PASS_ROWS = 512

_LOG2_E = 1.4426950408889634

_NT = (((1,), (1,)), ((), ()))
_TN = (((0,), (0,)), ((), ()))


def _rmsnorm(x, g):
    return x * lax.rsqrt(jnp.mean(x * x, axis=-1, keepdims=True) + EPS) * g


def _sigmoid(x):
    return 1.0 / (1.0 + jnp.exp(-x))


def _mm(a, b, dims=None):
    precision = lax.Precision.HIGHEST if a.dtype == F32 else None
    if dims is None:
        return jnp.dot(a, b, preferred_element_type=F32, precision=precision)
    return lax.dot_general(a, b, dims, preferred_element_type=F32, precision=precision)


def _const_spec(shape):
    zeros = (0,) * len(shape)
    return pl.BlockSpec(shape, lambda *_: zeros, pipeline_mode=pl.Buffered(1))


def _rope_rows(xt, c, s, n_heads):
    parts = []
    for h in range(n_heads):
        base = h * HEAD_DIM
        x1 = xt[base:base + ROT_HALF]
        x2 = xt[base + ROT_HALF:base + ROT_DIM]
        parts += [x1 * c - x2 * s, x2 * c + x1 * s, xt[base + ROT_DIM:base + HEAD_DIM]]
    return jnp.concatenate(parts, axis=0)


def _ffn_kernel(*refs, with_kv, cast_self, side_fns):
    refs = list(refs)
    act_ref = refs.pop()
    n_side = len(side_fns)
    side_dst = refs[len(refs) - n_side:]
    del refs[len(refs) - n_side:]
    if cast_self:
        self_dst = refs[-2:]
        del refs[-2:]
    if with_kv:
        k_ref, vt_ref = refs[-2:]
        del refs[-2:]
    o_ref = refs.pop()
    side_src = refs[len(refs) - n_side:]
    del refs[len(refs) - n_side:]
    if with_kv:
        gkv_ref, wkv_ref, rc_ref, ra_ref, rb_ref = refs[-5:]
        del refs[-5:]
    h_ref, gpre_ref, gpost_ref, win_ref, wout_ref = refs
    for fn, src, dst in zip(side_fns, side_src, side_dst):
        dst[...] = fn(src[...]).astype(dst.dtype)
    if cast_self:
        for src, dst in zip((win_ref, wout_ref), self_dst):
            dst[...] = src[...].astype(dst.dtype)
    mxu = win_ref.dtype
    tm = h_ref.shape[0]
    n_sub = 2 if tm >= 2 * _FFN_SUB_ROWS else 1
    sub = tm // n_sub

    def shared_kv(rows, h_new):
        kn = _rmsnorm(h_new, gkv_ref[...]).astype(mxu)
        kv = _mm(kn, wkv_ref[...])
        rc = rc_ref[rows, :]
        ra = ra_ref[rows, :]
        rb = rb_ref[rows, :]
        low = lax.broadcasted_iota(jnp.int32, (sub, LANES), 1) < HEAD_DIM
        cols = []
        for grp in range(KV_DIM // LANES):
            kg = kv[:, grp * LANES:(grp + 1) * LANES]
            kg = (kg * rc + pltpu.roll(kg, LANES - ROT_HALF, axis=1) * ra + pltpu.roll(kg, ROT_HALF, axis=1) * rb)
            cols += [jnp.where(low, kg, 0.0), jnp.where(low, pltpu.roll(kg, HEAD_DIM, axis=1), 0.0)]
        k_ref[rows, :] = jnp.concatenate(cols, axis=1).astype(k_ref.dtype)
        vt_ref[0, :, rows] = kv[:, KV_DIM:].T.astype(vt_ref.dtype)

    pending_kv = None
    for blk in range(n_sub):
        rows = slice(blk * sub, (blk + 1) * sub)
        x = h_ref[rows, :]
        if with_kv == "in":
            pending_kv = (rows, x)
        xn = _rmsnorm(x, gpre_ref[...]).astype(mxu)
        for ci, (lo, hi) in enumerate(_FF_CHUNKS):
            gate = _mm(xn, win_ref[:, lo:hi])
            up = _mm(xn, win_ref[:, D_FF + lo:D_FF + hi])
            act_ref[rows, lo:hi] = (gate * _sigmoid(gate) * up).astype(mxu)
            if pending_kv is not None and ci == _KV_LAG_CHUNKS - 1:
                shared_kv(*pending_kv)
                pending_kv = None
        y = _mm(act_ref[rows, :], wout_ref[...])
        h_new = x + 0.5 * _rmsnorm(y, gpost_ref[...])
        o_ref[rows, :] = h_new
        if with_kv == "out":
            pending_kv = (rows, h_new)
    if pending_kv is not None:
        shared_kv(*pending_kv)


def _weight_arg(w, shape):
    if isinstance(w, tuple):
        stack, sel = w
        return stack, pl.BlockSpec((None, None) + shape, lambda *_: (sel[0], sel[1], 0, 0),
                                   pipeline_mode=pl.Buffered(1))
    return w, _const_spec(shape)


def _cast_job(src, lead, n):
    shape = src.shape[len(lead):]
    per = 1
    while (shape[0] * per // n) % _BF16_SUBLANES:
        per *= 2
    blk = (shape[0] * per // n, shape[1])
    in_spec = pl.BlockSpec((None,) * len(lead) + blk, lambda i: tuple(lead) + (i // per, 0))
    return src, in_spec, jax.ShapeDtypeStruct(shape, BF16), pl.BlockSpec(blk, lambda i: (i // per, 0)), lambda x: x


def _transpose_job(src, lead, n, cols, scale=None):
    rows, width = src.shape[len(lead):]
    per = n * LANES // rows
    n_out = cols.stop - cols.start
    in_spec = pl.BlockSpec((None,) * len(lead) + (LANES, width), lambda i: tuple(lead) + (i // per, 0))
    out_spec = pl.BlockSpec((n_out, LANES), lambda i: (0, i // per))
    fn = lambda x: (x[:, cols] if scale is None else x[:, cols] * scale).T
    return src, in_spec, jax.ShapeDtypeStruct((n_out, rows), BF16), out_spec, fn


def _ffn(h, g_pre, g_post, w_in, w_out, *, tm, kv=None, kv_of="out", tiles_per_seq=1, cast_next=None,
         cast_self=False, side=()):
    t = h.shape[0]
    n = t // tm
    row_spec = pl.BlockSpec((tm, D_MODEL), lambda i: (i, 0))
    w_in, w_in_spec = _weight_arg(w_in, (D_MODEL, 2 * D_FF))
    w_out, w_out_spec = _weight_arg(w_out, (D_FF, D_MODEL))
    in_specs = [row_spec, _const_spec((1, D_MODEL)), _const_spec((1, D_MODEL)), w_in_spec, w_out_spec]
    args = [h, g_pre, g_post, w_in, w_out]
    out_shape = [jax.ShapeDtypeStruct((t, D_MODEL), F32)]
    out_specs = [row_spec]
    if kv is not None:
        g_kv, wkv, rc, ra, rb = kv
        rope_spec = pl.BlockSpec((tm, LANES), lambda i: (i % tiles_per_seq, 0))
        in_specs += [_const_spec((1, D_MODEL)), _const_spec((D_MODEL, 2 * KV_DIM)),
                     rope_spec, rope_spec, rope_spec]
        args += [g_kv, wkv, rc, ra, rb]
        n_seq = n // tiles_per_seq
        out_shape += [jax.ShapeDtypeStruct((t, K_PAD), BF16),
                      jax.ShapeDtypeStruct((n_seq, KV_DIM, tiles_per_seq * tm), BF16)]
        out_specs += [pl.BlockSpec((tm, K_PAD), lambda i: (i, 0)),
                      pl.BlockSpec((1, KV_DIM, tm), lambda i: (i // tiles_per_seq, 0, i % tiles_per_seq))]
    if cast_self:
        assert n == 1
        for shape in ((D_MODEL, 2 * D_FF), (D_FF, D_MODEL)):
            out_shape.append(jax.ShapeDtypeStruct(shape, BF16))
            out_specs.append(pl.BlockSpec(shape, lambda i: (0, 0)))
    side = list(side)
    if cast_next is not None:
        nxt_in, nxt_out, sel = cast_next
        side += [_cast_job(nxt_in, sel, n), _cast_job(nxt_out, sel, n)]
    for src, in_spec, out_sds, out_spec, _ in side:
        args.append(src)
        in_specs.append(in_spec)
        out_shape.append(out_sds)
        out_specs.append(out_spec)
    out = pl.pallas_call(
        functools.partial(_ffn_kernel, with_kv=kv_of if kv is not None else None, cast_self=cast_self,
                          side_fns=tuple(job[-1] for job in side)),
        grid=(n,),
        in_specs=in_specs,
        out_specs=out_specs,
        out_shape=out_shape,
        scratch_shapes=[pltpu.VMEM((tm, D_FF), w_in.dtype)],
        compiler_params=pltpu.CompilerParams(dimension_semantics=("arbitrary",), vmem_limit_bytes=_VMEM_LIMIT),
        name="ffn_kv" if kv is not None else "ffn",
    )(*args)
    return out if len(out) > 1 else out[0]


def _gla_kernel(*refs, rows, valid_rows, emit_state):
    if emit_state:
        (h_ref, s0_ref, gpre_ref, gpost_ref, win_ref, wlr_ref, wgate_ref, bgate_ref, ghead_ref, wout_ref,
         o_ref, sfin_ref, st_ref) = refs
    else:
        (h_ref, s0_ref, gpre_ref, gpost_ref, win_ref, wlr_ref, wgate_ref, bgate_ref, ghead_ref, wout_ref,
         o_ref, st_ref) = refs
    step = pl.program_id(1)
    n_chunks = rows // GLA_CHUNK
    blk_rows = min(rows, GLA_ATT_ROWS)
    bsl = [slice(b * blk_rows, (b + 1) * blk_rows) for b in range(rows // blk_rows)]

    @pl.when(step == 0)
    def _():
        st_ref[...] = s0_ref[...]

    mxu = win_ref.dtype
    x = h_ref[0]
    hn = _rmsnorm(x, gpre_ref[...]).astype(mxu)
    lr = _mm(hn, wlr_ref[...])
    gp = _mm(lr.astype(mxu), wgate_ref[...]) + bgate_ref[...]
    gk = (jnp.minimum(gp, 0.0) - jnp.log1p(jnp.exp(-jnp.abs(gp)))) * (_LOG2_E / GLA_TAU)
    q = _mm(hn, win_ref[:, 0:GLA_QK])
    k = _mm(hn, win_ref[:, GLA_QK:2 * GLA_QK])
    if valid_rows is not None:
        live = lax.broadcasted_iota(jnp.int32, (rows, 1), 0) < valid_rows
        gk = jnp.where(live, gk, 0.0)
        k = jnp.where(live, k, 0.0)

    ri = lax.broadcasted_iota(jnp.int32, (blk_rows, blk_rows), 0)
    ci = lax.broadcasted_iota(jnp.int32, (blk_rows, blk_rows), 1)
    intra = ((ri // GLA_CHUNK) == (ci // GLA_CHUNK)) & (ci <= ri)
    tri = jnp.where(intra, 1.0, 0.0).astype(mxu)
    g_hi = gk.astype(mxu)
    rem = gk - g_hi.astype(F32)
    g_mid = rem.astype(mxu)
    g_lo = (rem - g_mid.astype(F32)).astype(mxu)
    g_split = jnp.concatenate([g_hi, g_mid, g_lo], axis=1)
    parts = jnp.concatenate([_mm(tri, g_split[rs]) for rs in bsl], axis=0)
    bcum = parts[:, 0:GLA_QK] + parts[:, GLA_QK:2 * GLA_QK] + parts[:, 2 * GLA_QK:3 * GLA_QK]
    btot = jnp.concatenate(
        [jnp.broadcast_to(bcum[(c + 1) * GLA_CHUNK - 1:(c + 1) * GLA_CHUNK], (GLA_CHUNK, GLA_QK))
         for c in range(n_chunks)], axis=0)

    v = _mm(hn, win_ref[:, 2 * GLA_QK:2 * GLA_QK + GLA_V])
    r = _mm(hn, win_ref[:, 2 * GLA_QK + GLA_V:2 * GLA_QK + 2 * GLA_V])
    q_dec = (q * (GLA_DK ** -0.5) * jnp.exp2(bcum)).astype(mxu)
    k_dec = (k * jnp.exp2(-bcum)).astype(mxu)
    k_rem = (k * jnp.exp2(btot - bcum)).astype(mxu)
    vb = v.astype(mxu)
    decay = [jnp.exp2(jnp.broadcast_to(bcum[(c + 1) * GLA_CHUNK - 1:(c + 1) * GLA_CHUNK], (LANES, GLA_QK)).T)
             for c in range(n_chunks)]

    heads = range(GLA_HEADS)
    ksl = [slice(hd * GLA_DK, (hd + 1) * GLA_DK) for hd in heads]
    vsl = [slice(hd * GLA_DV, (hd + 1) * GLA_DV) for hd in heads]
    csl = [slice(c * GLA_CHUNK, (c + 1) * GLA_CHUNK) for c in range(n_chunks)]
    att = [[_mm(q_dec[rs, ksl[hd]], k_dec[rs, ksl[hd]], _NT) for rs in bsl] for hd in heads]
    upd = [[_mm(k_rem[csl[c], ksl[hd]], vb[csl[c], vsl[hd]], _TN) for c in range(n_chunks)] for hd in heads]
    o_intra = [jnp.concatenate([_mm(jnp.where(intra, att[hd][b], 0.0).astype(mxu), vb[rs, vsl[hd]])
                                for b, rs in enumerate(bsl)], axis=0) for hd in heads]

    outs = []
    for hd in heads:
        st = st_ref[hd]
        o_head = []
        for c in range(n_chunks):
            o_head.append(o_intra[hd][csl[c]] + _mm(q_dec[csl[c], ksl[hd]], st.astype(mxu)))
            d = decay[c][ksl[hd], :]
            st = st * jnp.concatenate([d] * (GLA_DV // LANES), axis=1) + upd[hd][c]
        st_ref[hd] = st
        outs.append(jnp.concatenate(o_head, axis=0))

    g_head = ghead_ref[...]
    gated = []
    for hd in heads:
        rg = r[:, vsl[hd]]
        gated.append((_rmsnorm(outs[hd], g_head) * (rg * _sigmoid(rg))).astype(mxu))
    for rs in bsl:
        mix = _mm(gated[0][rs], wout_ref[vsl[0], :])
        for hd in heads[1:]:
            mix = mix + _mm(gated[hd][rs], wout_ref[vsl[hd], :])
        o_ref[0, rs, :] = h_ref[0, rs, :] + _rmsnorm(mix, gpost_ref[...])

    if emit_state:
        @pl.when(step == pl.num_programs(1) - 1)
        def _():
            sfin_ref[0] = st_ref[...]


def _gla(h, s0, g_pre, g_post, w_in, layer, w_lr, w_gate, b_gate, g_head, w_out, *, rows, valid_rows=None,
         emit_state=False):
    b, length, _ = h.shape
    steps = length // rows
    seq_spec = pl.BlockSpec((1, rows, D_MODEL), lambda i, j: (i, j, 0))
    state_shape = (GLA_HEADS, GLA_DK, GLA_DV)
    if w_in.ndim == 3:
        w_in_spec = pl.BlockSpec((None, D_MODEL, GLA_IN_MAIN), lambda i, j: (layer, 0, 0),
                                 pipeline_mode=pl.Buffered(1))
    else:
        w_in_spec = _const_spec((D_MODEL, GLA_IN_MAIN))
    in_specs = [seq_spec, _const_spec(state_shape), _const_spec((1, D_MODEL)), _const_spec((1, D_MODEL)),
                w_in_spec,
                _const_spec((D_MODEL, LANES)), _const_spec((LANES, GLA_QK)), _const_spec((1, GLA_QK)),
                _const_spec((1, GLA_DV)), _const_spec((GLA_V, D_MODEL))]
    out_shape = [jax.ShapeDtypeStruct(h.shape, F32)]
    out_specs = [seq_spec]
    if emit_state:
        out_shape.append(jax.ShapeDtypeStruct((b,) + state_shape, F32))
        out_specs.append(pl.BlockSpec((1,) + state_shape, lambda i, j: (i, 0, 0, 0)))
    return pl.pallas_call(
        functools.partial(_gla_kernel, rows=rows, valid_rows=valid_rows, emit_state=emit_state),
        grid=(b, steps),
        in_specs=in_specs,
        out_specs=out_specs,
        out_shape=out_shape,
        scratch_shapes=[pltpu.VMEM(state_shape, F32)],
        compiler_params=pltpu.CompilerParams(dimension_semantics=("parallel", "arbitrary"),
                                             vmem_limit_bytes=_VMEM_LIMIT),
        name="gla_meta" if emit_state else "gla",
    )(h, s0, g_pre, g_post, w_in, w_lr, w_gate, b_gate, g_head, w_out)


def _swa_kernel(sinks_ref, h_ref, gpre_ref, gpost_ref, wqt_ref, wo_ref, cos_ref, sin_ref,
                kc_ref, kp_ref, vtc_ref, vtp_ref, km_ref, vtm_ref, o_ref, ot_ref, st_ref, pt_ref, *, rows):
    step = pl.program_id(1)
    x = h_ref[0]
    hn = _rmsnorm(x, gpre_ref[...]).astype(BF16)
    qt = _mm(wqt_ref[...], hn, _NT)
    qtb = _rope_rows(qt, cos_ref[...], sin_ref[...], N_Q_HEADS).astype(BF16)

    key_j = lax.broadcasted_iota(jnp.int32, (WINDOW, WINDOW), 0)
    qry_i = lax.broadcasted_iota(jnp.int32, (WINDOW, WINDOW), 1)
    from_prev = key_j > qry_i
    first_prev_ok = key_j > qry_i + jnp.where(step > 0, 0, WINDOW)
    per_pass = min(rows, _SWA_PASS_ROWS) // WINDOW
    for first_blk in range(0, rows // WINDOW, per_pass):
        _swa_pass(first_blk, per_pass, sinks_ref, qtb, from_prev, first_prev_ok, kc_ref, kp_ref, vtc_ref, vtp_ref,
                  km_ref, vtm_ref, ot_ref, st_ref, pt_ref)
        rs = slice(first_blk * WINDOW, (first_blk + per_pass) * WINDOW)
        mix = _mm(ot_ref[:, rs].astype(BF16), wo_ref[...], _TN)
        o_ref[0, rs, :] = h_ref[0, rs, :] + _rmsnorm(mix, gpost_ref[...])


def _swa_pass(first_blk, n_blk, sinks_ref, qtb, from_prev, first_prev_ok, kc_ref, kp_ref, vtc_ref, vtp_ref,
              km_ref, vtm_ref, ot_ref, st_ref, pt_ref):
    units = [(blk, g) for blk in range(first_blk, first_blk + n_blk) for g in range(N_KV_HEADS)]

    for u, (blk, g) in enumerate(units):
        c0 = blk * WINDOW
        l0 = g * LANES
        k_prev = kp_ref[0, :, l0:l0 + HEAD_DIM] if blk == 0 else kc_ref[0, c0 - WINDOW:c0, l0:l0 + HEAD_DIM]
        kcat = jnp.concatenate([km_ref[:, l0:l0 + HEAD_DIM], k_prev,
                                kc_ref[0, c0:c0 + WINDOW, l0:l0 + HEAD_DIM]], axis=0)
        qg = jnp.concatenate([qtb[(GROUP * g + i) * HEAD_DIM:(GROUP * g + i + 1) * HEAD_DIM, c0:c0 + WINDOW]
                              for i in range(GROUP)], axis=1)
        st_ref[u] = _mm(kcat, qg)

    for u, (blk, g) in enumerate(units):
        c0 = blk * WINDOW
        d0 = g * HEAD_DIM
        inv = []
        for i in range(GROUP):
            sink = sinks_ref[GROUP * g + i] * _LOG2_E
            q0 = i * WINDOW
            s_meta = st_ref[u, 0:N_META, q0:q0 + WINDOW]
            s_prev = st_ref[u, N_META:N_META + WINDOW, q0:q0 + WINDOW]
            s_cur = st_ref[u, N_META + WINDOW:N_META + 2 * WINDOW, q0:q0 + WINDOW]
            if blk == 0:
                s_prev = jnp.where(first_prev_ok, s_prev, NEG_INF)
            s_all = jnp.concatenate([s_meta, jnp.where(from_prev, s_prev, s_cur)], axis=0)
            m = jnp.maximum(jnp.max(s_all, axis=0, keepdims=True), sink)
            p_all = jnp.exp2(s_all - m)
            inv.append(1.0 / (jnp.sum(p_all, axis=0, keepdims=True) + jnp.exp2(sink - m)))
            p_meta = p_all[0:N_META]
            p_band = p_all[N_META:]
            pt_ref[u, :, q0:q0 + WINDOW] = jnp.concatenate(
                [p_meta, jnp.where(from_prev, p_band, 0.0), jnp.where(from_prev, 0.0, p_band)],
                axis=0).astype(BF16)
        vt_prev = vtp_ref[0, d0:d0 + HEAD_DIM, :] if blk == 0 else vtc_ref[0, d0:d0 + HEAD_DIM, c0 - WINDOW:c0]
        vt_band = jnp.concatenate([vt_prev, vtc_ref[0, d0:d0 + HEAD_DIM, c0:c0 + WINDOW]], axis=1)
        og = (_mm(vt_band, pt_ref[u, N_META:, :])
              + _mm(vtm_ref[d0:d0 + HEAD_DIM, :], pt_ref[u, 0:N_META, :]))
        for i in range(GROUP):
            h0 = (GROUP * g + i) * HEAD_DIM
            ot_ref[h0:h0 + HEAD_DIM, c0:c0 + WINDOW] = og[:, i * WINDOW:(i + 1) * WINDOW] * inv[i]


def _swa(h, sinks, g_pre, g_post, wqt, w_o, cos_t, sin_t, k_pad, vt, k_meta, vt_meta, *, rows):
    b, length, _ = h.shape
    steps = length // rows
    per = rows // WINDOW
    n_units = min(rows, _SWA_PASS_ROWS) // WINDOW * N_KV_HEADS
    seq_spec = pl.BlockSpec((1, rows, D_MODEL), lambda i, j: (i, j, 0))
    rope_spec = pl.BlockSpec((ROT_HALF, rows), lambda i, j: (0, j))
    prev_blk = lambda j: jnp.maximum(j * per - 1, 0)
    in_specs = [
        pl.BlockSpec(memory_space=pltpu.SMEM),
        seq_spec, _const_spec((1, D_MODEL)), _const_spec((1, D_MODEL)),
        _const_spec((D_MODEL, D_MODEL)), _const_spec((D_MODEL, D_MODEL)),
        rope_spec, rope_spec,
        pl.BlockSpec((1, rows, K_PAD), lambda i, j: (i, j, 0)),
        pl.BlockSpec((1, WINDOW, K_PAD), lambda i, j: (i, prev_blk(j), 0)),
        pl.BlockSpec((1, KV_DIM, rows), lambda i, j: (i, 0, j)),
        pl.BlockSpec((1, KV_DIM, WINDOW), lambda i, j: (i, 0, prev_blk(j))),
        _const_spec((N_META, K_PAD)), _const_spec((KV_DIM, N_META)),
    ]
    return pl.pallas_call(
        functools.partial(_swa_kernel, rows=rows),
        grid=(b, steps),
        in_specs=in_specs,
        out_specs=seq_spec,
        out_shape=jax.ShapeDtypeStruct(h.shape, F32),
        scratch_shapes=[pltpu.VMEM((D_MODEL, rows), F32),
                        pltpu.VMEM((n_units, N_META + 2 * WINDOW, GROUP * WINDOW), F32),
                        pltpu.VMEM((n_units, N_META + 2 * WINDOW, GROUP * WINDOW), BF16)],
        compiler_params=pltpu.CompilerParams(dimension_semantics=("parallel", "parallel"),
                                             vmem_limit_bytes=_VMEM_LIMIT),
        name="swa",
    )(sinks, h, g_pre, g_post, wqt, w_o, cos_t, sin_t, k_pad, k_pad, vt, vt, k_meta, vt_meta)


def _rope_angles(positions):
    inv_freq = ROPE_THETA ** (-jnp.arange(0, ROT_DIM, 2, dtype=F32) / ROT_DIM)
    ang = positions.astype(F32)[:, None] * inv_freq[None, :]
    return jnp.cos(ang), jnp.sin(ang)


def _k_rope_tables(cos, sin):
    n = cos.shape[0]
    rest = HEAD_DIM - ROT_DIM
    rc = jnp.concatenate([cos, cos, jnp.ones((n, rest), F32)], axis=1)
    ra = jnp.concatenate([-sin, jnp.zeros((n, ROT_HALF + rest), F32)], axis=1)
    rb = jnp.concatenate([jnp.zeros((n, ROT_HALF), F32), sin, jnp.zeros((n, rest), F32)], axis=1)
    return tuple(jnp.tile(t, (1, LANES // HEAD_DIM)) for t in (rc, ra, rb))


def kernel(x, meta_tokens, norm_gains, w_ffn_in, w_ffn_out, gla_w_in, gla_w_gate, gla_b_gate, gla_norm, gla_w_out,
           kv_norm, w_kv, swa_w_q, swa_sinks, swa_w_out):
    batch, seq, _ = x.shape
    depth = norm_gains.shape[0]
    n_a = gla_w_in.shape[0]
    tm = 1024
    gla_rows = 1024
    swa_rows = 1024

    gains = norm_gains.reshape(depth, 6, 1, D_MODEL)
    n_ffn = batch * seq // tm
    ffn_order = [(layer, which) for layer in range(depth) for which in range(2)]
    ffn_w = {}

    def ffn_main(h, layer, which, g_pre, g_post, **kw):
        pos = ffn_order.index((layer, which))
        nxt = ffn_order[pos + 1] if pos + 1 < len(ffn_order) else None
        cast = (w_ffn_in, w_ffn_out, nxt) if nxt is not None else None
        out = _ffn(h, g_pre, g_post, *ffn_w[(layer, which)], tm=tm, cast_next=cast, **kw)
        if nxt is None:
            return out
        *out, nxt_in, nxt_out = out
        ffn_w[nxt] = (nxt_in, nxt_out)
        return out if len(out) > 1 else out[0]

    h = x.reshape(batch * seq, D_MODEL)
    hm = jnp.pad(meta_tokens.astype(x.dtype), ((0, META_ROWS - N_META), (0, 0)))

    cos_m, sin_m = _rope_angles(jnp.arange(META_ROWS))
    cos_r, sin_r = _rope_angles(jnp.arange(N_META, N_META + seq))
    g_kv = kv_norm.reshape(1, D_MODEL)

    k_pad = vt = k_meta = vt_meta = w_kv_b = kv_w = None
    for layer in range(depth):
        g = gains[layer]
        last = layer == depth - 1
        if layer < n_a:
            hm = _ffn(hm, g[0], g[1], (w_ffn_in, (layer, 0)), (w_ffn_out, (layer, 0)), tm=META_ROWS,
                      cast_self=layer == 0)
            if layer == 0:
                hm, *ffn_w[(0, 0)] = hm
        if (layer, 0) not in ffn_w:
            ffn_w[(layer, 0)] = (w_ffn_in[layer, 0].astype(BF16), w_ffn_out[layer, 0].astype(BF16))
        if layer < n_a:
            a = layer
            w_in_a = gla_w_in[a]
            side = [_cast_job(w_in_a, (), n_ffn), _cast_job(gla_w_out, (a,), n_ffn)]
            if layer == 0:
                side.append(_cast_job(w_kv, (), n_ffn))
                h, gla_w_in_b, gla_w_out_b, w_kv_b = ffn_main(h, layer, 0, g[0], g[1], side=side)
            else:
                h, gla_w_in_b, gla_w_out_b = ffn_main(h, layer, 0, g[0], g[1], side=side)
            w_lr = jnp.pad(w_in_a[:, GLA_IN_MAIN:], ((0, 0), (0, LANES - GLA_RANK)))
            w_gate_a = jnp.pad(gla_w_gate[a], ((0, LANES - GLA_RANK), (0, 0)))
            gla_vecs = (gla_b_gate[a].reshape(1, GLA_QK), gla_norm[a].reshape(1, GLA_DV))
            s_zero = jnp.zeros((GLA_HEADS, GLA_DK, GLA_DV), F32)
            hm3, s_meta = _gla(hm[None], s_zero, g[2], g[3], w_in_a, a, w_lr, w_gate_a, *gla_vecs, gla_w_out[a],
                               rows=META_ROWS, valid_rows=N_META, emit_state=True)
            hm = hm3[0]
            h = _gla(h.reshape(batch, seq, D_MODEL), s_meta[0], g[2], g[3], gla_w_in_b, a, w_lr.astype(BF16),
                     w_gate_a.astype(BF16), *gla_vecs, gla_w_out_b, rows=gla_rows)[0]
            h = h.reshape(batch * seq, D_MODEL)
        else:
            b = layer - n_a
            side = [_transpose_job(swa_w_q, (b,), n_ffn, slice(0, D_MODEL), HEAD_DIM ** -0.5 * _LOG2_E),
                    _cast_job(swa_w_out, (b,), n_ffn)]
            if b == 0:
                h, k_pad, vt, wqt, w_o = ffn_main(h, layer, 0, g[0], g[1], side=side, kv_of="in",
                                                  kv=kv_w + _k_rope_tables(cos_r, sin_r), tiles_per_seq=seq // tm)
            else:
                h, wqt, w_o = ffn_main(h, layer, 0, g[0], g[1], side=side)
            h = _swa(h.reshape(batch, seq, D_MODEL), swa_sinks[b], g[2], g[3], wqt, w_o,
                     cos_r.T, sin_r.T, k_pad.reshape(batch, seq, K_PAD), vt, k_meta, vt_meta, rows=swa_rows)
            h = h.reshape(batch * seq, D_MODEL)
        if layer == n_a - 1:
            kv_w = (g_kv, w_kv_b)
            h = ffn_main(h, layer, 1, g[4], g[5])
            _, k_m, vt_m = _ffn(hm, g[4], g[5], *ffn_w[(layer, 1)], tm=META_ROWS,
                                kv=kv_w + _k_rope_tables(cos_m, sin_m))
            k_meta = k_m[:N_META]
            vt_meta = vt_m[0, :, :N_META]
        else:
            h = ffn_main(h, layer, 1, g[4], g[5])
            if not last and layer < n_a:
                hm = _ffn(hm, g[4], g[5], (w_ffn_in, (layer, 1)), (w_ffn_out, (layer, 1)), tm=META_ROWS)
    return h.reshape(batch, seq, D_MODEL)
```

```python
import functools

import jax
import jax.numpy as jnp
from jax import lax
from jax.experimental import pallas as pl
from jax.experimental.pallas import tpu as pltpu

F32 = jnp.float32
BF16 = jnp.bfloat16

D_MODEL = 1024
D_FF = 2816
N_META = 16
EPS = 1e-6
NEG_INF = -1e30

GLA_HEADS = 4
GLA_DK = 128
GLA_DV = 256
GLA_QK = GLA_HEADS * GLA_DK
GLA_V = GLA_HEADS * GLA_DV
GLA_RANK = 16
GLA_TAU = 16.0
GLA_CHUNK = 64
GLA_IN_MAIN = 2 * GLA_QK + 2 * GLA_V
GLA_ATT_ROWS = 256

N_Q_HEADS = 16
N_KV_HEADS = 4
GROUP = N_Q_HEADS // N_KV_HEADS
HEAD_DIM = 64
WINDOW = 128
ROT_DIM = HEAD_DIM // 4
ROT_HALF = ROT_DIM // 2
ROPE_THETA = 500000.0
KV_DIM = N_KV_HEADS * HEAD_DIM
LANES = 128
K_PAD = N_KV_HEADS * LANES

META_ROWS = 64

_VMEM_LIMIT = 56 * 1024 * 1024

_FF_CHUNKS = ((0, 768), (768, 1536), (1536, 2304), (2304, 2816))
_FFN_SUB_ROWS = 512
_KV_LAG_CHUNKS = 2
_BF16_SUBLANES = 16
_SWA_PASS_ROWS = 512

_LOG2_E = 1.4426950408889634

_NT = (((1,), (1,)), ((), ()))
_TN = (((0,), (0,)), ((), ()))


def _rmsnorm(x, g):
    return x * lax.rsqrt(jnp.mean(x * x, axis=-1, keepdims=True) + EPS) * g


def _sigmoid(x):
    return 1.0 / (1.0 + jnp.exp(-x))


def _mm(a, b, dims=None):
    precision = lax.Precision.HIGHEST if a.dtype == F32 else None
    if dims is None:
        return jnp.dot(a, b, preferred_element_type=F32, precision=precision)
    return lax.dot_general(a, b, dims, preferred_element_type=F32, precision=precision)


def _const_spec(shape):
    zeros = (0,) * len(shape)
    return pl.BlockSpec(shape, lambda *_: zeros, pipeline_mode=pl.Buffered(1))


def _rope_rows(xt, c, s, n_heads):
    parts = []
    for h in range(n_heads):
        base = h * HEAD_DIM
        x1 = xt[base:base + ROT_HALF]
        x2 = xt[base + ROT_HALF:base + ROT_DIM]
        parts += [x1 * c - x2 * s, x2 * c + x1 * s, xt[base + ROT_DIM:base + HEAD_DIM]]
    return jnp.concatenate(parts, axis=0)


def _ffn_kernel(*refs, with_kv, cast_self, side_fns):
    refs = list(refs)
    act_ref = refs.pop()
    n_side = len(side_fns)
    side_dst = refs[len(refs) - n_side:]
    del refs[len(refs) - n_side:]
    if cast_self:
        self_dst = refs[-2:]
        del refs[-2:]
    if with_kv:
        k_ref, vt_ref = refs[-2:]
        del refs[-2:]
    o_ref = refs.pop()
    side_src = refs[len(refs) - n_side:]
    del refs[len(refs) - n_side:]
    if with_kv:
        gkv_ref, wkv_ref, rc_ref, ra_ref, rb_ref = refs[-5:]
        del refs[-5:]
    h_ref, gpre_ref, gpost_ref, win_ref, wout_ref = refs
    for fn, src, dst in zip(side_fns, side_src, side_dst):
        dst[...] = fn(src[...]).astype(dst.dtype)
    if cast_self:
        for src, dst in zip((win_ref, wout_ref), self_dst):
            dst[...] = src[...].astype(dst.dtype)
    mxu = win_ref.dtype
    tm = h_ref.shape[0]
    n_sub = 2 if tm >= 2 * _FFN_SUB_ROWS else 1
    sub = tm // n_sub

    def shared_kv(rows, h_new):
        kn = _rmsnorm(h_new, gkv_ref[...]).astype(mxu)
        kv = _mm(kn, wkv_ref[...])
        rc = rc_ref[rows, :]
        ra = ra_ref[rows, :]
        rb = rb_ref[rows, :]
        low = lax.broadcasted_iota(jnp.int32, (sub, LANES), 1) < HEAD_DIM
        cols = []
        for grp in range(KV_DIM // LANES):
            kg = kv[:, grp * LANES:(grp + 1) * LANES]
            kg = (kg * rc + pltpu.roll(kg, LANES - ROT_HALF, axis=1) * ra + pltpu.roll(kg, ROT_HALF, axis=1) * rb)
            cols += [jnp.where(low, kg, 0.0), jnp.where(low, pltpu.roll(kg, HEAD_DIM, axis=1), 0.0)]
        k_ref[rows, :] = jnp.concatenate(cols, axis=1).astype(k_ref.dtype)
        vt_ref[0, :, rows] = kv[:, KV_DIM:].T.astype(vt_ref.dtype)

    pending_kv = None
    for blk in range(n_sub):
        rows = slice(blk * sub, (blk + 1) * sub)
        x = h_ref[rows, :]
        if with_kv == "in":
            pending_kv = (rows, x)
        xn = _rmsnorm(x, gpre_ref[...]).astype(mxu)
        for ci, (lo, hi) in enumerate(_FF_CHUNKS):
            gate = _mm(xn, win_ref[:, lo:hi])
            up = _mm(xn, win_ref[:, D_FF + lo:D_FF + hi])
            act_ref[rows, lo:hi] = (gate * _sigmoid(gate) * up).astype(mxu)
            if pending_kv is not None and ci == _KV_LAG_CHUNKS - 1:
                shared_kv(*pending_kv)
                pending_kv = None
        y = _mm(act_ref[rows, :], wout_ref[...])
        h_new = x + _rmsnorm(y, gpost_ref[...])
        o_ref[rows, :] = h_new
        if with_kv == "out":
            pending_kv = (rows, h_new)
    if pending_kv is not None:
        shared_kv(*pending_kv)


def _weight_arg(w, shape):
    if isinstance(w, tuple):
        stack, sel = w
        return stack, pl.BlockSpec((None, None) + shape, lambda *_: (sel[0], sel[1], 0, 0),
                                   pipeline_mode=pl.Buffered(1))
    return w, _const_spec(shape)


def _cast_job(src, lead, n):
    shape = src.shape[len(lead):]
    per = 1
    while (shape[0] * per // n) % _BF16_SUBLANES:
        per *= 2
    blk = (shape[0] * per // n, shape[1])
    in_spec = pl.BlockSpec((None,) * len(lead) + blk, lambda i: tuple(lead) + (i // per, 0))
    return src, in_spec, jax.ShapeDtypeStruct(shape, BF16), pl.BlockSpec(blk, lambda i: (i // per, 0)), lambda x: x


def _transpose_job(src, lead, n, cols, scale=None):
    rows, width = src.shape[len(lead):]
    per = n * LANES // rows
    n_out = cols.stop - cols.start
    in_spec = pl.BlockSpec((None,) * len(lead) + (LANES, width), lambda i: tuple(lead) + (i // per, 0))
    out_spec = pl.BlockSpec((n_out, LANES), lambda i: (0, i // per))
    fn = lambda x: (x[:, cols] if scale is None else x[:, cols] * scale).T
    return src, in_spec, jax.ShapeDtypeStruct((n_out, rows), BF16), out_spec, fn


def _ffn(h, g_pre, g_post, w_in, w_out, *, tm, kv=None, kv_of="out", tiles_per_seq=1, cast_next=None,
         cast_self=False, side=()):
    t = h.shape[0]
    n = t // tm
    row_spec = pl.BlockSpec((tm, D_MODEL), lambda i: (i, 0))
    w_in, w_in_spec = _weight_arg(w_in, (D_MODEL, 2 * D_FF))
    w_out, w_out_spec = _weight_arg(w_out, (D_FF, D_MODEL))
    in_specs = [row_spec, _const_spec((1, D_MODEL)), _const_spec((1, D_MODEL)), w_in_spec, w_out_spec]
    args = [h, g_pre, g_post, w_in, w_out]
    out_shape = [jax.ShapeDtypeStruct((t, D_MODEL), F32)]
    out_specs = [row_spec]
    if kv is not None:
        g_kv, wkv, rc, ra, rb = kv
        rope_spec = pl.BlockSpec((tm, LANES), lambda i: (i % tiles_per_seq, 0))
        in_specs += [_const_spec((1, D_MODEL)), _const_spec((D_MODEL, 2 * KV_DIM)),
                     rope_spec, rope_spec, rope_spec]
        args += [g_kv, wkv, rc, ra, rb]
        n_seq = n // tiles_per_seq
        out_shape += [jax.ShapeDtypeStruct((t, K_PAD), BF16),
                      jax.ShapeDtypeStruct((n_seq, KV_DIM, tiles_per_seq * tm), BF16)]
        out_specs += [pl.BlockSpec((tm, K_PAD), lambda i: (i, 0)),
                      pl.BlockSpec((1, KV_DIM, tm), lambda i: (i // tiles_per_seq, 0, i % tiles_per_seq))]
    if cast_self:
        assert n == 1
        for shape in ((D_MODEL, 2 * D_FF), (D_FF, D_MODEL)):
            out_shape.append(jax.ShapeDtypeStruct(shape, BF16))
            out_specs.append(pl.BlockSpec(shape, lambda i: (0, 0)))
    side = list(side)
    if cast_next is not None:
        nxt_in, nxt_out, sel = cast_next
        side += [_cast_job(nxt_in, sel, n), _cast_job(nxt_out, sel, n)]
    for src, in_spec, out_sds, out_spec, _ in side:
        args.append(src)
        in_specs.append(in_spec)
        out_shape.append(out_sds)
        out_specs.append(out_spec)
    out = pl.pallas_call(
        functools.partial(_ffn_kernel, with_kv=kv_of if kv is not None else None, cast_self=cast_self,
                          side_fns=tuple(job[-1] for job in side)),
        grid=(n,),
        in_specs=in_specs,
        out_specs=out_specs,
        out_shape=out_shape,
        scratch_shapes=[pltpu.VMEM((tm, D_FF), w_in.dtype)],
        compiler_params=pltpu.CompilerParams(dimension_semantics=("arbitrary",), vmem_limit_bytes=_VMEM_LIMIT),
        name="ffn_kv" if kv is not None else "ffn",
    )(*args)
    return out if len(out) > 1 else out[0]


def _gla_kernel(*refs, rows, valid_rows, emit_state):
    if emit_state:
        (h_ref, s0_ref, gpre_ref, gpost_ref, win_ref, wlr_ref, wgate_ref, bgate_ref, ghead_ref, wout_ref,
         o_ref, sfin_ref, st_ref) = refs
    else:
        (h_ref, s0_ref, gpre_ref, gpost_ref, win_ref, wlr_ref, wgate_ref, bgate_ref, ghead_ref, wout_ref,
         o_ref, st_ref) = refs
    step = pl.program_id(1)
    n_chunks = rows // GLA_CHUNK
    blk_rows = min(rows, GLA_ATT_ROWS)
    bsl = [slice(b * blk_rows, (b + 1) * blk_rows) for b in range(rows // blk_rows)]

    @pl.when(step == 0)
    def _():
        st_ref[...] = s0_ref[...]

    mxu = win_ref.dtype
    x = h_ref[0]
    hn = _rmsnorm(x, gpre_ref[...]).astype(mxu)
    lr = _mm(hn, wlr_ref[...])
    gp = _mm(lr.astype(mxu), wgate_ref[...]) + bgate_ref[...]
    gk = (jnp.minimum(gp, 0.0) - jnp.log1p(jnp.exp(-jnp.abs(gp)))) * (_LOG2_E / GLA_TAU)
    q = _mm(hn, win_ref[:, 0:GLA_QK])
    k = _mm(hn, win_ref[:, GLA_QK:2 * GLA_QK])
    if valid_rows is not None:
        live = lax.broadcasted_iota(jnp.int32, (rows, 1), 0) < valid_rows
        gk = jnp.where(live, gk, 0.0)
        k = jnp.where(live, k, 0.0)

    ri = lax.broadcasted_iota(jnp.int32, (blk_rows, blk_rows), 0)
    ci = lax.broadcasted_iota(jnp.int32, (blk_rows, blk_rows), 1)
    intra = ((ri // GLA_CHUNK) == (ci // GLA_CHUNK)) & (ci <= ri)
    tri = jnp.where(intra, 1.0, 0.0).astype(mxu)
    g_hi = gk.astype(mxu)
    rem = gk - g_hi.astype(F32)
    g_mid = rem.astype(mxu)
    g_lo = (rem - g_mid.astype(F32)).astype(mxu)
    g_split = jnp.concatenate([g_hi, g_mid, g_lo], axis=1)
    parts = jnp.concatenate([_mm(tri, g_split[rs]) for rs in bsl], axis=0)
    bcum = parts[:, 0:GLA_QK] + parts[:, GLA_QK:2 * GLA_QK] + parts[:, 2 * GLA_QK:3 * GLA_QK]
    btot = jnp.concatenate(
        [jnp.broadcast_to(bcum[(c + 1) * GLA_CHUNK - 1:(c + 1) * GLA_CHUNK], (GLA_CHUNK, GLA_QK))
         for c in range(n_chunks)], axis=0)

    v = _mm(hn, win_ref[:, 2 * GLA_QK:2 * GLA_QK + GLA_V])
    r = _mm(hn, win_ref[:, 2 * GLA_QK + GLA_V:2 * GLA_QK + 2 * GLA_V])
    q_dec = (q * (GLA_DK ** -0.5) * jnp.exp2(bcum)).astype(mxu)
    k_dec = (k * jnp.exp2(-bcum)).astype(mxu)
    k_rem = (k * jnp.exp2(btot - bcum)).astype(mxu)
    vb = v.astype(mxu)
    decay = [jnp.exp2(jnp.broadcast_to(bcum[(c + 1) * GLA_CHUNK - 1:(c + 1) * GLA_CHUNK], (LANES, GLA_QK)).T)
             for c in range(n_chunks)]

    heads = range(GLA_HEADS)
    ksl = [slice(hd * GLA_DK, (hd + 1) * GLA_DK) for hd in heads]
    vsl = [slice(hd * GLA_DV, (hd + 1) * GLA_DV) for hd in heads]
    csl = [slice(c * GLA_CHUNK, (c + 1) * GLA_CHUNK) for c in range(n_chunks)]
    att = [[_mm(q_dec[rs, ksl[hd]], k_dec[rs, ksl[hd]], _NT) for rs in bsl] for hd in heads]
    upd = [[_mm(k_rem[csl[c], ksl[hd]], vb[csl[c], vsl[hd]], _TN) for c in range(n_chunks)] for hd in heads]
    o_intra = [jnp.concatenate([_mm(jnp.where(intra, att[hd][b], 0.0).astype(mxu), vb[rs, vsl[hd]])
                                for b, rs in enumerate(bsl)], axis=0) for hd in heads]

    outs = []
    for hd in heads:
        st = st_ref[hd]
        o_head = []
        for c in range(n_chunks):
            o_head.append(o_intra[hd][csl[c]] + _mm(q_dec[csl[c], ksl[hd]], st.astype(mxu)))
            d = decay[c][ksl[hd], :]
            st = st * jnp.concatenate([d] * (GLA_DV // LANES), axis=1) + upd[hd][c]
        st_ref[hd] = st
        outs.append(jnp.concatenate(o_head, axis=0))

    g_head = ghead_ref[...]
    gated = []
    for hd in heads:
        rg = r[:, vsl[hd]]
        gated.append((_rmsnorm(outs[hd], g_head) * (rg * _sigmoid(rg))).astype(mxu))
    for rs in bsl:
        mix = _mm(gated[0][rs], wout_ref[vsl[0], :])
        for hd in heads[1:]:
            mix = mix + _mm(gated[hd][rs], wout_ref[vsl[hd], :])
        o_ref[0, rs, :] = h_ref[0, rs, :] + _rmsnorm(mix, gpost_ref[...])

    if emit_state:
        @pl.when(step == pl.num_programs(1) - 1)
        def _():
            sfin_ref[0] = st_ref[...]


def _gla(h, s0, g_pre, g_post, w_in, layer, w_lr, w_gate, b_gate, g_head, w_out, *, rows, valid_rows=None,
         emit_state=False):
    b, length, _ = h.shape
    steps = length // rows
    seq_spec = pl.BlockSpec((1, rows, D_MODEL), lambda i, j: (i, j, 0))
    state_shape = (GLA_HEADS, GLA_DK, GLA_DV)
    if w_in.ndim == 3:
        w_in_spec = pl.BlockSpec((None, D_MODEL, GLA_IN_MAIN), lambda i, j: (layer, 0, 0),
                                 pipeline_mode=pl.Buffered(1))
    else:
        w_in_spec = _const_spec((D_MODEL, GLA_IN_MAIN))
    in_specs = [seq_spec, _const_spec(state_shape), _const_spec((1, D_MODEL)), _const_spec((1, D_MODEL)),
                w_in_spec,
                _const_spec((D_MODEL, LANES)), _const_spec((LANES, GLA_QK)), _const_spec((1, GLA_QK)),
                _const_spec((1, GLA_DV)), _const_spec((GLA_V, D_MODEL))]
    out_shape = [jax.ShapeDtypeStruct(h.shape, F32)]
    out_specs = [seq_spec]
    if emit_state:
        out_shape.append(jax.ShapeDtypeStruct((b,) + state_shape, F32))
        out_specs.append(pl.BlockSpec((1,) + state_shape, lambda i, j: (i, 0, 0, 0)))
    return pl.pallas_call(
        functools.partial(_gla_kernel, rows=rows, valid_rows=valid_rows, emit_state=emit_state),
        grid=(b, steps),
        in_specs=in_specs,
        out_specs=out_specs,
        out_shape=out_shape,
        scratch_shapes=[pltpu.VMEM(state_shape, F32)],
        compiler_params=pltpu.CompilerParams(dimension_semantics=("parallel", "arbitrary"),
                                             vmem_limit_bytes=_VMEM_LIMIT),
        name="gla_meta" if emit_state else "gla",
    )(h, s0, g_pre, g_post, w_in, w_lr, w_gate, b_gate, g_head, w_out)


def _swa_kernel(sinks_ref, h_ref, gpre_ref, gpost_ref, wqt_ref, wo_ref, cos_ref, sin_ref,
                kc_ref, kp_ref, vtc_ref, vtp_ref, km_ref, vtm_ref, o_ref, ot_ref, st_ref, pt_ref, *, rows):
    step = pl.program_id(1)
    x = h_ref[0]
    hn = _rmsnorm(x, gpre_ref[...]).astype(BF16)
    qt = _mm(wqt_ref[...], hn, _NT)
    qtb = _rope_rows(qt, cos_ref[...], sin_ref[...], N_Q_HEADS).astype(BF16)

    key_j = lax.broadcasted_iota(jnp.int32, (WINDOW, WINDOW), 0)
    qry_i = lax.broadcasted_iota(jnp.int32, (WINDOW, WINDOW), 1)
    from_prev = key_j > qry_i
    first_prev_ok = key_j > qry_i + jnp.where(step > 0, 0, WINDOW)
    per_pass = min(rows, _SWA_PASS_ROWS) // WINDOW
    for first_blk in range(0, rows // WINDOW, per_pass):
        _swa_pass(first_blk, per_pass, sinks_ref, qtb, from_prev, first_prev_ok, kc_ref, kp_ref, vtc_ref, vtp_ref,
                  km_ref, vtm_ref, ot_ref, st_ref, pt_ref)
        rs = slice(first_blk * WINDOW, (first_blk + per_pass) * WINDOW)
        mix = _mm(ot_ref[:, rs].astype(BF16), wo_ref[...], _TN)
        o_ref[0, rs, :] = h_ref[0, rs, :] + _rmsnorm(mix, gpost_ref[...])


def _swa_pass(first_blk, n_blk, sinks_ref, qtb, from_prev, first_prev_ok, kc_ref, kp_ref, vtc_ref, vtp_ref,
              km_ref, vtm_ref, ot_ref, st_ref, pt_ref):
    units = [(blk, g) for blk in range(first_blk, first_blk + n_blk) for g in range(N_KV_HEADS)]

    for u, (blk, g) in enumerate(units):
        c0 = blk * WINDOW
        l0 = g * LANES
        k_prev = kp_ref[0, :, l0:l0 + HEAD_DIM] if blk == 0 else kc_ref[0, c0 - WINDOW:c0, l0:l0 + HEAD_DIM]
        kcat = jnp.concatenate([km_ref[:, l0:l0 + HEAD_DIM], k_prev,
                                kc_ref[0, c0:c0 + WINDOW, l0:l0 + HEAD_DIM]], axis=0)
        qg = jnp.concatenate([qtb[(GROUP * g + i) * HEAD_DIM:(GROUP * g + i + 1) * HEAD_DIM, c0:c0 + WINDOW]
                              for i in range(GROUP)], axis=1)
        st_ref[u] = _mm(kcat, qg)

    for u, (blk, g) in enumerate(units):
        c0 = blk * WINDOW
        d0 = g * HEAD_DIM
        inv = []
        for i in range(GROUP):
            sink = sinks_ref[GROUP * g + i] * _LOG2_E
            q0 = i * WINDOW
            s_meta = st_ref[u, 0:N_META, q0:q0 + WINDOW]
            s_prev = st_ref[u, N_META:N_META + WINDOW, q0:q0 + WINDOW]
            s_cur = st_ref[u, N_META + WINDOW:N_META + 2 * WINDOW, q0:q0 + WINDOW]
            if blk == 0:
                s_prev = jnp.where(first_prev_ok, s_prev, NEG_INF)
            s_all = jnp.concatenate([s_meta, jnp.where(from_prev, s_prev, s_cur)], axis=0)
            m = jnp.maximum(jnp.max(s_all, axis=0, keepdims=True), sink)
            p_all = jnp.exp2(s_all - m)
            inv.append(1.0 / (jnp.sum(p_all, axis=0, keepdims=True) + jnp.exp2(sink - m)))
            p_meta = p_all[0:N_META]
            p_band = p_all[N_META:]
            pt_ref[u, :, q0:q0 + WINDOW] = jnp.concatenate(
                [p_meta, jnp.where(from_prev, p_band, 0.0), jnp.where(from_prev, 0.0, p_band)],
                axis=0).astype(BF16)
        vt_prev = vtp_ref[0, d0:d0 + HEAD_DIM, :] if blk == 0 else vtc_ref[0, d0:d0 + HEAD_DIM, c0 - WINDOW:c0]
        vt_band = jnp.concatenate([vt_prev, vtc_ref[0, d0:d0 + HEAD_DIM, c0:c0 + WINDOW]], axis=1)
        og = (_mm(vt_band, pt_ref[u, N_META:, :])
              + _mm(vtm_ref[d0:d0 + HEAD_DIM, :], pt_ref[u, 0:N_META, :]))
        for i in range(GROUP):
            h0 = (GROUP * g + i) * HEAD_DIM
            ot_ref[h0:h0 + HEAD_DIM, c0:c0 + WINDOW] = og[:, i * WINDOW:(i + 1) * WINDOW] * inv[i]


def _swa(h, sinks, g_pre, g_post, wqt, w_o, cos_t, sin_t, k_pad, vt, k_meta, vt_meta, *, rows):
    b, length, _ = h.shape
    steps = length // rows
    per = rows // WINDOW
    n_units = min(rows, _SWA_PASS_ROWS) // WINDOW * N_KV_HEADS
    seq_spec = pl.BlockSpec((1, rows, D_MODEL), lambda i, j: (i, j, 0))
    rope_spec = pl.BlockSpec((ROT_HALF, rows), lambda i, j: (0, j))
    prev_blk = lambda j: jnp.maximum(j * per - 1, 0)
    in_specs = [
        pl.BlockSpec(memory_space=pltpu.SMEM),
        seq_spec, _const_spec((1, D_MODEL)), _const_spec((1, D_MODEL)),
        _const_spec((D_MODEL, D_MODEL)), _const_spec((D_MODEL, D_MODEL)),
        rope_spec, rope_spec,
        pl.BlockSpec((1, rows, K_PAD), lambda i, j: (i, j, 0)),
        pl.BlockSpec((1, WINDOW, K_PAD), lambda i, j: (i, prev_blk(j), 0)),
        pl.BlockSpec((1, KV_DIM, rows), lambda i, j: (i, 0, j)),
        pl.BlockSpec((1, KV_DIM, WINDOW), lambda i, j: (i, 0, prev_blk(j))),
        _const_spec((N_META, K_PAD)), _const_spec((KV_DIM, N_META)),
    ]
    return pl.pallas_call(
        functools.partial(_swa_kernel, rows=rows),
        grid=(b, steps),
        in_specs=in_specs,
        out_specs=seq_spec,
        out_shape=jax.ShapeDtypeStruct(h.shape, F32),
        scratch_shapes=[pltpu.VMEM((D_MODEL, rows), F32),
                        pltpu.VMEM((n_units, N_META + 2 * WINDOW, GROUP * WINDOW), F32),
                        pltpu.VMEM((n_units, N_META + 2 * WINDOW, GROUP * WINDOW), BF16)],
        compiler_params=pltpu.CompilerParams(dimension_semantics=("parallel", "parallel"),
                                             vmem_limit_bytes=_VMEM_LIMIT),
        name="swa",
    )(sinks, h, g_pre, g_post, wqt, w_o, cos_t, sin_t, k_pad, k_pad, vt, vt, k_meta, vt_meta)


def _rope_angles(positions):
    inv_freq = ROPE_THETA ** (-jnp.arange(0, ROT_DIM, 2, dtype=F32) / ROT_DIM)
    ang = positions.astype(F32)[:, None] * inv_freq[None, :]
    return jnp.cos(ang), jnp.sin(ang)


def _k_rope_tables(cos, sin):
    n = cos.shape[0]
    rest = HEAD_DIM - ROT_DIM
    rc = jnp.concatenate([cos, cos, jnp.ones((n, rest), F32)], axis=1)
    ra = jnp.concatenate([-sin, jnp.zeros((n, ROT_HALF + rest), F32)], axis=1)
    rb = jnp.concatenate([jnp.zeros((n, ROT_HALF), F32), sin, jnp.zeros((n, rest), F32)], axis=1)
    return tuple(jnp.tile(t, (1, LANES // HEAD_DIM)) for t in (rc, ra, rb))


def kernel(x, meta_tokens, norm_gains, w_ffn_in, w_ffn_out, gla_w_in, gla_w_gate, gla_b_gate, gla_norm, gla_w_out,
           kv_norm, w_kv, swa_w_q, swa_sinks, swa_w_out):
    batch, seq, _ = x.shape
    depth = norm_gains.shape[0]
    n_a = gla_w_in.shape[0]
    tm = 1024
    gla_rows = 1024
    swa_rows = 1024

    half_step = jnp.array([1.0, 0.5, 1.0, 1.0, 1.0, 0.5], F32)[None, :, None]
    gains = (norm_gains * half_step).reshape(depth, 6, 1, D_MODEL)
    n_ffn = batch * seq // tm
    ffn_order = [(layer, which) for layer in range(depth) for which in range(2)]
    ffn_w = {}

    def ffn_main(h, layer, which, g_pre, g_post, **kw):
        pos = ffn_order.index((layer, which))
        nxt = ffn_order[pos + 1] if pos + 1 < len(ffn_order) else None
        cast = (w_ffn_in, w_ffn_out, nxt) if nxt is not None else None
        out = _ffn(h, g_pre, g_post, *ffn_w[(layer, which)], tm=tm, cast_next=cast, **kw)
        if nxt is None:
            return out
        *out, nxt_in, nxt_out = out
        ffn_w[nxt] = (nxt_in, nxt_out)
        return out if len(out) > 1 else out[0]

    h = x.reshape(batch * seq, D_MODEL)
    hm = jnp.pad(meta_tokens.astype(x.dtype), ((0, META_ROWS - N_META), (0, 0)))

    cos_m, sin_m = _rope_angles(jnp.arange(META_ROWS))
    cos_r, sin_r = _rope_angles(jnp.arange(N_META, N_META + seq))
    g_kv = kv_norm.reshape(1, D_MODEL)

    k_pad = vt = k_meta = vt_meta = w_kv_b = kv_w = None
    for layer in range(depth):
        g = gains[layer]
        last = layer == depth - 1
        if layer < n_a:
            hm = _ffn(hm, g[0], g[1], (w_ffn_in, (layer, 0)), (w_ffn_out, (layer, 0)), tm=META_ROWS,
                      cast_self=layer == 0)
            if layer == 0:
                hm, *ffn_w[(0, 0)] = hm
        if (layer, 0) not in ffn_w:
            ffn_w[(layer, 0)] = (w_ffn_in[layer, 0].astype(BF16), w_ffn_out[layer, 0].astype(BF16))
        if layer < n_a:
            a = layer
            w_in_a = gla_w_in[a]
            side = [_cast_job(w_in_a, (), n_ffn), _cast_job(gla_w_out, (a,), n_ffn)]
            if layer == 0:
                side.append(_cast_job(w_kv, (), n_ffn))
                h, gla_w_in_b, gla_w_out_b, w_kv_b = ffn_main(h, layer, 0, g[0], g[1], side=side)
            else:
                h, gla_w_in_b, gla_w_out_b = ffn_main(h, layer, 0, g[0], g[1], side=side)
            w_lr = jnp.pad(w_in_a[:, GLA_IN_MAIN:], ((0, 0), (0, LANES - GLA_RANK)))
            w_gate_a = jnp.pad(gla_w_gate[a], ((0, LANES - GLA_RANK), (0, 0)))
            gla_vecs = (gla_b_gate[a].reshape(1, GLA_QK), gla_norm[a].reshape(1, GLA_DV))
            s_zero = jnp.zeros((GLA_HEADS, GLA_DK, GLA_DV), F32)
            hm3, s_meta = _gla(hm[None], s_zero, g[2], g[3], w_in_a, a, w_lr, w_gate_a, *gla_vecs, gla_w_out[a],
                               rows=META_ROWS, valid_rows=N_META, emit_state=True)
            hm = hm3[0]
            h = _gla(h.reshape(batch, seq, D_MODEL), s_meta[0], g[2], g[3], gla_w_in_b, a, w_lr.astype(BF16),
                     w_gate_a.astype(BF16), *gla_vecs, gla_w_out_b, rows=gla_rows)[0]
            h = h.reshape(batch * seq, D_MODEL)
        else:
            b = layer - n_a
            side = [_transpose_job(swa_w_q, (b,), n_ffn, slice(0, D_MODEL), HEAD_DIM ** -0.5 * _LOG2_E),
                    _cast_job(swa_w_out, (b,), n_ffn)]
            if b == 0:
                h, k_pad, vt, wqt, w_o = ffn_main(h, layer, 0, g[0], g[1], side=side, kv_of="in",
                                                  kv=kv_w + _k_rope_tables(cos_r, sin_r), tiles_per_seq=seq // tm)
            else:
                h, wqt, w_o = ffn_main(h, layer, 0, g[0], g[1], side=side)
            h = _swa(h.reshape(batch, seq, D_MODEL), swa_sinks[b], g[2], g[3], wqt, w_o,
                     cos_r.T, sin_r.T, k_pad.reshape(batch, seq, K_PAD), vt, k_meta, vt_meta, rows=swa_rows)
            h = h.reshape(batch * seq, D_MODEL)
        if layer == n_a - 1:
            kv_w = (g_kv, w_kv_b)
            h = ffn_main(h, layer, 1, g[4], g[5])
            _, k_m, vt_m = _ffn(hm, g[4], g[5], *ffn_w[(layer, 1)], tm=META_ROWS,
                                kv=kv_w + _k_rope_tables(cos_m, sin_m))
            k_meta = k_m[:N_META]
            vt_meta = vt_m[0, :, :N_META]
        else:
            h = ffn_main(h, layer, 1, g[4], g[5])
            if not last and layer < n_a:
                hm = _ffn(hm, g[4], g[5], (w_ffn_in, (layer, 1)), (w_ffn_out, (layer, 1)), tm=META_ROWS)
    return h.reshape(batch, seq, D_MODEL)
```

```python
import functools

import jax
import jax.numpy as jnp
from jax import lax
from jax.experimental import pallas as pl
from jax.experimental.pallas import tpu as pltpu

F32 = jnp.float32
BF16 = jnp.bfloat16

D_MODEL = 1024
D_FF = 2816
N_META = 16
EPS = 1e-6
NEG_INF = -1e30

GLA_HEADS = 4
GLA_DK = 128
GLA_DV = 256
GLA_QK = GLA_HEADS * GLA_DK
GLA_V = GLA_HEADS * GLA_DV
GLA_RANK = 16
GLA_TAU = 16.0
GLA_CHUNK = 64
GLA_IN_MAIN = 2 * GLA_QK + 2 * GLA_V
GLA_ATT_ROWS = 256

N_Q_HEADS = 16
N_KV_HEADS = 4
GROUP = N_Q_HEADS // N_KV_HEADS
HEAD_DIM = 64
WINDOW = 128
ROT_DIM = HEAD_DIM // 4
ROT_HALF = ROT_DIM // 2
ROPE_THETA = 500000.0
KV_DIM = N_KV_HEADS * HEAD_DIM
LANES = 128
K_PAD = N_KV_HEADS * LANES

META_ROWS = 64

_VMEM_LIMIT = 56 * 1024 * 1024

_FFN_ROWS = 1024
_GLA_ROWS = 1024
_SWA_ROWS = 1024

_FF_CHUNKS = ((0, 768), (768, 1536), (1536, 2304), (2304, 2816))
_FFN_SUB_ROWS = 512
_KV_LAG_CHUNKS = 2
_BF16_SUBLANES = 16
_SWA_PASS_ROWS = 512

_LOG2_E = 1.4426950408889634

_NT = (((1,), (1,)), ((), ()))
_TN = (((0,), (0,)), ((), ()))


def _rmsnorm(x, g):
    return x * lax.rsqrt(jnp.mean(x * x, axis=-1, keepdims=True) + EPS) * g


def _sigmoid(x):
    return 1.0 / (1.0 + jnp.exp(-x))


def _mm(a, b, dims=None):
    precision = lax.Precision.HIGHEST if a.dtype == F32 else None
    if dims is None:
        return jnp.dot(a, b, preferred_element_type=F32, precision=precision)
    return lax.dot_general(a, b, dims, preferred_element_type=F32, precision=precision)


def _const_spec(shape):
    zeros = (0,) * len(shape)
    return pl.BlockSpec(shape, lambda *_: zeros, pipeline_mode=pl.Buffered(1))


def _rope_rows(xt, c, s, n_heads):
    parts = []
    for h in range(n_heads):
        base = h * HEAD_DIM
        x1 = xt[base:base + ROT_HALF]
        x2 = xt[base + ROT_HALF:base + ROT_DIM]
        parts += [x1 * c - x2 * s, x2 * c + x1 * s, xt[base + ROT_DIM:base + HEAD_DIM]]
    return jnp.concatenate(parts, axis=0)


def _ffn_kernel(*refs, with_kv, cast_self, side_fns):
    refs = list(refs)
    act_ref = refs.pop()
    n_side = len(side_fns)
    side_dst = refs[len(refs) - n_side:]
    del refs[len(refs) - n_side:]
    if cast_self:
        self_dst = refs[-2:]
        del refs[-2:]
    if with_kv:
        k_ref, vt_ref = refs[-2:]
        del refs[-2:]
    o_ref = refs.pop()
    side_src = refs[len(refs) - n_side:]
    del refs[len(refs) - n_side:]
    if with_kv:
        gkv_ref, wkv_ref, rc_ref, ra_ref, rb_ref = refs[-5:]
        del refs[-5:]
    h_ref, gpre_ref, gpost_ref, win_ref, wout_ref = refs
    for fn, src, dst in zip(side_fns, side_src, side_dst):
        dst[...] = fn(src[...]).astype(dst.dtype)
    if cast_self:
        for src, dst in zip((win_ref, wout_ref), self_dst):
            dst[...] = src[...].astype(dst.dtype)
    mxu = win_ref.dtype
    tm = h_ref.shape[0]
    n_sub = 2 if tm >= 2 * _FFN_SUB_ROWS else 1
    sub = tm // n_sub

    def shared_kv(rows, h_new):
        kn = _rmsnorm(h_new, gkv_ref[...]).astype(mxu)
        kv = _mm(kn, wkv_ref[...])
        rc = rc_ref[rows, :]
        ra = ra_ref[rows, :]
        rb = rb_ref[rows, :]
        low = lax.broadcasted_iota(jnp.int32, (sub, LANES), 1) < HEAD_DIM
        cols = []
        for grp in range(KV_DIM // LANES):
            kg = kv[:, grp * LANES:(grp + 1) * LANES]
            kg = (kg * rc + pltpu.roll(kg, LANES - ROT_HALF, axis=1) * ra + pltpu.roll(kg, ROT_HALF, axis=1) * rb)
            cols += [jnp.where(low, kg, 0.0), jnp.where(low, pltpu.roll(kg, HEAD_DIM, axis=1), 0.0)]
        k_ref[rows, :] = jnp.concatenate(cols, axis=1).astype(k_ref.dtype)
        vt_ref[0, :, rows] = kv[:, KV_DIM:].T.astype(vt_ref.dtype)

    pending_kv = None
    for blk in range(n_sub):
        rows = slice(blk * sub, (blk + 1) * sub)
        x = h_ref[rows, :]
        if with_kv == "in":
            pending_kv = (rows, x)
        xn = _rmsnorm(x, gpre_ref[...]).astype(mxu)
        for ci, (lo, hi) in enumerate(_FF_CHUNKS):
            gate = _mm(xn, win_ref[:, lo:hi])
            up = _mm(xn, win_ref[:, D_FF + lo:D_FF + hi])
            act_ref[rows, lo:hi] = (gate * _sigmoid(gate) * up).astype(mxu)
            if pending_kv is not None and ci == _KV_LAG_CHUNKS - 1:
                shared_kv(*pending_kv)
                pending_kv = None
        y = _mm(act_ref[rows, :], wout_ref[...])
        h_new = x + _rmsnorm(y, gpost_ref[...])
        o_ref[rows, :] = h_new
        if with_kv == "out":
            pending_kv = (rows, h_new)
    if pending_kv is not None:
        shared_kv(*pending_kv)


def _weight_arg(w, shape):
    if isinstance(w, tuple):
        stack, sel = w
        return stack, pl.BlockSpec((None, None) + shape, lambda *_: (sel[0], sel[1], 0, 0),
                                   pipeline_mode=pl.Buffered(1))
    return w, _const_spec(shape)


def _cast_job(src, lead, n):
    shape = src.shape[len(lead):]
    per = 1
    while (shape[0] * per // n) % _BF16_SUBLANES:
        per *= 2
    blk = (shape[0] * per // n, shape[1])
    in_spec = pl.BlockSpec((None,) * len(lead) + blk, lambda i: tuple(lead) + (i // per, 0))
    return src, in_spec, jax.ShapeDtypeStruct(shape, BF16), pl.BlockSpec(blk, lambda i: (i // per, 0)), lambda x: x


def _transpose_job(src, lead, n, cols, scale=None):
    rows, width = src.shape[len(lead):]
    per = n * LANES // rows
    n_out = cols.stop - cols.start
    in_spec = pl.BlockSpec((None,) * len(lead) + (LANES, width), lambda i: tuple(lead) + (i // per, 0))
    out_spec = pl.BlockSpec((n_out, LANES), lambda i: (0, i // per))
    fn = lambda x: (x[:, cols] if scale is None else x[:, cols] * scale).T
    return src, in_spec, jax.ShapeDtypeStruct((n_out, rows), BF16), out_spec, fn


def _ffn(h, g_pre, g_post, w_in, w_out, *, tm, kv=None, kv_of="out", tiles_per_seq=1, cast_next=None,
         cast_self=False, side=()):
    t = h.shape[0]
    n = t // tm
    row_spec = pl.BlockSpec((tm, D_MODEL), lambda i: (i, 0))
    w_in, w_in_spec = _weight_arg(w_in, (D_MODEL, 2 * D_FF))
    w_out, w_out_spec = _weight_arg(w_out, (D_FF, D_MODEL))
    in_specs = [row_spec, _const_spec((1, D_MODEL)), _const_spec((1, D_MODEL)), w_in_spec, w_out_spec]
    args = [h, g_pre, g_post, w_in, w_out]
    out_shape = [jax.ShapeDtypeStruct((t, D_MODEL), F32)]
    out_specs = [row_spec]
    if kv is not None:
        g_kv, wkv, rc, ra, rb = kv
        rope_spec = pl.BlockSpec((tm, LANES), lambda i: (i % tiles_per_seq, 0))
        in_specs += [_const_spec((1, D_MODEL)), _const_spec((D_MODEL, 2 * KV_DIM)),
                     rope_spec, rope_spec, rope_spec]
        args += [g_kv, wkv, rc, ra, rb]
        n_seq = n // tiles_per_seq
        out_shape += [jax.ShapeDtypeStruct((t, K_PAD), BF16),
                      jax.ShapeDtypeStruct((n_seq, KV_DIM, tiles_per_seq * tm), BF16)]
        out_specs += [pl.BlockSpec((tm, K_PAD), lambda i: (i, 0)),
                      pl.BlockSpec((1, KV_DIM, tm), lambda i: (i // tiles_per_seq, 0, i % tiles_per_seq))]
    if cast_self:
        assert n == 1
        for shape in ((D_MODEL, 2 * D_FF), (D_FF, D_MODEL)):
            out_shape.append(jax.ShapeDtypeStruct(shape, BF16))
            out_specs.append(pl.BlockSpec(shape, lambda i: (0, 0)))
    side = list(side)
    if cast_next is not None:
        nxt_in, nxt_out, sel = cast_next
        side += [_cast_job(nxt_in, sel, n), _cast_job(nxt_out, sel, n)]
    for src, in_spec, out_sds, out_spec, _ in side:
        args.append(src)
        in_specs.append(in_spec)
        out_shape.append(out_sds)
        out_specs.append(out_spec)
    out = pl.pallas_call(
        functools.partial(_ffn_kernel, with_kv=kv_of if kv is not None else None, cast_self=cast_self,
                          side_fns=tuple(job[-1] for job in side)),
        grid=(n,),
        in_specs=in_specs,
        out_specs=out_specs,
        out_shape=out_shape,
        scratch_shapes=[pltpu.VMEM((tm, D_FF), w_in.dtype)],
        compiler_params=pltpu.CompilerParams(dimension_semantics=("arbitrary",), vmem_limit_bytes=_VMEM_LIMIT),
        name="ffn_kv" if kv is not None else "ffn",
    )(*args)
    return out if len(out) > 1 else out[0]


def _gla_kernel(*refs, rows, valid_rows, emit_state):
    if emit_state:
        (h_ref, s0_ref, gpre_ref, gpost_ref, win_ref, wlr_ref, wgate_ref, bgate_ref, ghead_ref, wout_ref,
         o_ref, sfin_ref, st_ref) = refs
    else:
        (h_ref, s0_ref, gpre_ref, gpost_ref, win_ref, wlr_ref, wgate_ref, bgate_ref, ghead_ref, wout_ref,
         o_ref, st_ref) = refs
    step = pl.program_id(1)
    n_chunks = rows // GLA_CHUNK
    blk_rows = min(rows, GLA_ATT_ROWS)
    bsl = [slice(b * blk_rows, (b + 1) * blk_rows) for b in range(rows // blk_rows)]

    @pl.when(step == 0)
    def _():
        st_ref[...] = s0_ref[...]

    mxu = win_ref.dtype
    x = h_ref[0]
    hn = _rmsnorm(x, gpre_ref[...]).astype(mxu)
    lr = _mm(hn, wlr_ref[...])
    gp = _mm(lr.astype(mxu), wgate_ref[...]) + bgate_ref[...]
    gk = (jnp.minimum(gp, 0.0) - jnp.log1p(jnp.exp(-jnp.abs(gp)))) * (_LOG2_E / GLA_TAU)
    q = _mm(hn, win_ref[:, 0:GLA_QK])
    k = _mm(hn, win_ref[:, GLA_QK:2 * GLA_QK])
    if valid_rows is not None:
        live = lax.broadcasted_iota(jnp.int32, (rows, 1), 0) < valid_rows
        gk = jnp.where(live, gk, 0.0)
        k = jnp.where(live, k, 0.0)

    ri = lax.broadcasted_iota(jnp.int32, (blk_rows, blk_rows), 0)
    ci = lax.broadcasted_iota(jnp.int32, (blk_rows, blk_rows), 1)
    intra = ((ri // GLA_CHUNK) == (ci // GLA_CHUNK)) & (ci <= ri)
    tri = jnp.where(intra, 1.0, 0.0).astype(mxu)
    g_hi = gk.astype(mxu)
    rem = gk - g_hi.astype(F32)
    g_mid = rem.astype(mxu)
    g_lo = (rem - g_mid.astype(F32)).astype(mxu)
    g_split = jnp.concatenate([g_hi, g_mid, g_lo], axis=1)
    parts = jnp.concatenate([_mm(tri, g_split[rs]) for rs in bsl], axis=0)
    bcum = parts[:, 0:GLA_QK] + parts[:, GLA_QK:2 * GLA_QK] + parts[:, 2 * GLA_QK:3 * GLA_QK]
    btot = jnp.concatenate(
        [jnp.broadcast_to(bcum[(c + 1) * GLA_CHUNK - 1:(c + 1) * GLA_CHUNK], (GLA_CHUNK, GLA_QK))
         for c in range(n_chunks)], axis=0)

    v = _mm(hn, win_ref[:, 2 * GLA_QK:2 * GLA_QK + GLA_V])
    r = _mm(hn, win_ref[:, 2 * GLA_QK + GLA_V:2 * GLA_QK + 2 * GLA_V])
    q_dec = (q * (GLA_DK ** -0.5) * jnp.exp2(bcum)).astype(mxu)
    k_dec = (k * jnp.exp2(-bcum)).astype(mxu)
    k_rem = (k * jnp.exp2(btot - bcum)).astype(mxu)
    vb = v.astype(mxu)
    decay = [jnp.exp2(jnp.broadcast_to(bcum[(c + 1) * GLA_CHUNK - 1:(c + 1) * GLA_CHUNK], (LANES, GLA_QK)).T)
             for c in range(n_chunks)]

    heads = range(GLA_HEADS)
    ksl = [slice(hd * GLA_DK, (hd + 1) * GLA_DK) for hd in heads]
    vsl = [slice(hd * GLA_DV, (hd + 1) * GLA_DV) for hd in heads]
    csl = [slice(c * GLA_CHUNK, (c + 1) * GLA_CHUNK) for c in range(n_chunks)]
    att = [[_mm(q_dec[rs, ksl[hd]], k_dec[rs, ksl[hd]], _NT) for rs in bsl] for hd in heads]
    upd = [[_mm(k_rem[csl[c], ksl[hd]], vb[csl[c], vsl[hd]], _TN) for c in range(n_chunks)] for hd in heads]
    o_intra = [jnp.concatenate([_mm(jnp.where(intra, att[hd][b], 0.0).astype(mxu), vb[rs, vsl[hd]])
                                for b, rs in enumerate(bsl)], axis=0) for hd in heads]

    outs = []
    for hd in heads:
        st = st_ref[hd]
        o_head = []
        for c in range(n_chunks):
            o_head.append(o_intra[hd][csl[c]] + _mm(q_dec[csl[c], ksl[hd]], st.astype(mxu)))
            d = decay[c][ksl[hd], :]
            st = st * jnp.concatenate([d] * (GLA_DV // LANES), axis=1) + upd[hd][c]
        st_ref[hd] = st
        outs.append(jnp.concatenate(o_head, axis=0))

    g_head = ghead_ref[...]
    gated = []
    for hd in heads:
        rg = r[:, vsl[hd]]
        gated.append((_rmsnorm(outs[hd], g_head) * (rg * _sigmoid(rg))).astype(mxu))
    for rs in bsl:
        mix = _mm(gated[0][rs], wout_ref[vsl[0], :])
        for hd in heads[1:]:
            mix = mix + _mm(gated[hd][rs], wout_ref[vsl[hd], :])
        o_ref[0, rs, :] = h_ref[0, rs, :] + _rmsnorm(mix, gpost_ref[...])

    if emit_state:
        @pl.when(step == pl.num_programs(1) - 1)
        def _():
            sfin_ref[0] = st_ref[...]


def _gla(h, s0, g_pre, g_post, w_in, layer, w_lr, w_gate, b_gate, g_head, w_out, *, rows, valid_rows=None,
         emit_state=False):
    b, length, _ = h.shape
    steps = length // rows
    seq_spec = pl.BlockSpec((1, rows, D_MODEL), lambda i, j: (i, j, 0))
    state_shape = (GLA_HEADS, GLA_DK, GLA_DV)
    if w_in.ndim == 3:
        w_in_spec = pl.BlockSpec((None, D_MODEL, GLA_IN_MAIN), lambda i, j: (layer, 0, 0),
                                 pipeline_mode=pl.Buffered(1))
    else:
        w_in_spec = _const_spec((D_MODEL, GLA_IN_MAIN))
    in_specs = [seq_spec, _const_spec(state_shape), _const_spec((1, D_MODEL)), _const_spec((1, D_MODEL)),
                w_in_spec,
                _const_spec((D_MODEL, LANES)), _const_spec((LANES, GLA_QK)), _const_spec((1, GLA_QK)),
                _const_spec((1, GLA_DV)), _const_spec((GLA_V, D_MODEL))]
    out_shape = [jax.ShapeDtypeStruct(h.shape, F32)]
    out_specs = [seq_spec]
    if emit_state:
        out_shape.append(jax.ShapeDtypeStruct((b,) + state_shape, F32))
        out_specs.append(pl.BlockSpec((1,) + state_shape, lambda i, j: (i, 0, 0, 0)))
    return pl.pallas_call(
        functools.partial(_gla_kernel, rows=rows, valid_rows=valid_rows, emit_state=emit_state),
        grid=(b, steps),
        in_specs=in_specs,
        out_specs=out_specs,
        out_shape=out_shape,
        scratch_shapes=[pltpu.VMEM(state_shape, F32)],
        compiler_params=pltpu.CompilerParams(dimension_semantics=("parallel", "arbitrary"),
                                             vmem_limit_bytes=_VMEM_LIMIT),
        name="gla_meta" if emit_state else "gla",
    )(h, s0, g_pre, g_post, w_in, w_lr, w_gate, b_gate, g_head, w_out)


def _swa_kernel(sinks_ref, h_ref, gpre_ref, gpost_ref, wqt_ref, wo_ref, cos_ref, sin_ref,
                kc_ref, kp_ref, vtc_ref, vtp_ref, km_ref, vtm_ref, o_ref, ot_ref, st_ref, pt_ref, *, rows):
    step = pl.program_id(1)
    x = h_ref[0]
    hn = _rmsnorm(x, gpre_ref[...]).astype(BF16)
    qt = _mm(wqt_ref[...], hn, _NT)
    qtb = _rope_rows(qt, cos_ref[...], sin_ref[...], N_Q_HEADS).astype(BF16)

    key_j = lax.broadcasted_iota(jnp.int32, (WINDOW, WINDOW), 0)
    qry_i = lax.broadcasted_iota(jnp.int32, (WINDOW, WINDOW), 1)
    from_prev = key_j > qry_i
    first_prev_ok = key_j > qry_i + jnp.where(step > 0, 0, WINDOW)
    per_pass = min(rows, _SWA_PASS_ROWS) // WINDOW
    for first_blk in range(0, rows // WINDOW, per_pass):
        _swa_pass(first_blk, per_pass, sinks_ref, qtb, from_prev, first_prev_ok, kc_ref, kp_ref, vtc_ref, vtp_ref,
                  km_ref, vtm_ref, ot_ref, st_ref, pt_ref)
        rs = slice(first_blk * WINDOW, (first_blk + per_pass) * WINDOW)
        mix = _mm(ot_ref[:, rs].astype(BF16), wo_ref[...], _TN)
        o_ref[0, rs, :] = h_ref[0, rs, :] + _rmsnorm(mix, gpost_ref[...])


def _swa_pass(first_blk, n_blk, sinks_ref, qtb, from_prev, first_prev_ok, kc_ref, kp_ref, vtc_ref, vtp_ref,
              km_ref, vtm_ref, ot_ref, st_ref, pt_ref):
    units = [(blk, g) for blk in range(first_blk, first_blk + n_blk) for g in range(N_KV_HEADS)]

    for u, (blk, g) in enumerate(units):
        c0 = blk * WINDOW
        l0 = g * LANES
        k_prev = kp_ref[0, :, l0:l0 + HEAD_DIM] if blk == 0 else kc_ref[0, c0 - WINDOW:c0, l0:l0 + HEAD_DIM]
        kcat = jnp.concatenate([km_ref[:, l0:l0 + HEAD_DIM], k_prev,
                                kc_ref[0, c0:c0 + WINDOW, l0:l0 + HEAD_DIM]], axis=0)
        qg = jnp.concatenate([qtb[(GROUP * g + i) * HEAD_DIM:(GROUP * g + i + 1) * HEAD_DIM, c0:c0 + WINDOW]
                              for i in range(GROUP)], axis=1)
        st_ref[u] = _mm(kcat, qg)

    for u, (blk, g) in enumerate(units):
        c0 = blk * WINDOW
        d0 = g * HEAD_DIM
        inv = []
        for i in range(GROUP):
            sink = sinks_ref[GROUP * g + i] * _LOG2_E
            q0 = i * WINDOW
            s_meta = st_ref[u, 0:N_META, q0:q0 + WINDOW]
            s_prev = st_ref[u, N_META:N_META + WINDOW, q0:q0 + WINDOW]
            s_cur = st_ref[u, N_META + WINDOW:N_META + 2 * WINDOW, q0:q0 + WINDOW]
            if blk == 0:
                s_prev = jnp.where(first_prev_ok, s_prev, NEG_INF)
            s_all = jnp.concatenate([s_meta, jnp.where(from_prev, s_prev, s_cur)], axis=0)
            m = jnp.maximum(jnp.max(s_all, axis=0, keepdims=True), sink)
            p_all = jnp.exp2(s_all - m)
            inv.append(1.0 / (jnp.sum(p_all, axis=0, keepdims=True) + jnp.exp2(sink - m)))
            p_meta = p_all[0:N_META]
            p_band = p_all[N_META:]
            pt_ref[u, :, q0:q0 + WINDOW] = jnp.concatenate(
                [p_meta, jnp.where(from_prev, p_band, 0.0), jnp.where(from_prev, 0.0, p_band)],
                axis=0).astype(BF16)
        vt_prev = vtp_ref[0, d0:d0 + HEAD_DIM, :] if blk == 0 else vtc_ref[0, d0:d0 + HEAD_DIM, c0 - WINDOW:c0]
        vt_band = jnp.concatenate([vt_prev, vtc_ref[0, d0:d0 + HEAD_DIM, c0:c0 + WINDOW]], axis=1)
        og = (_mm(vt_band, pt_ref[u, N_META:, :])
              + _mm(vtm_ref[d0:d0 + HEAD_DIM, :], pt_ref[u, 0:N_META, :]))
        for i in range(GROUP):
            h0 = (GROUP * g + i) * HEAD_DIM
            ot_ref[h0:h0 + HEAD_DIM, c0:c0 + WINDOW] = og[:, i * WINDOW:(i + 1) * WINDOW] * inv[i]


def _swa(h, sinks, g_pre, g_post, wqt, w_o, cos_t, sin_t, k_pad, vt, k_meta, vt_meta, *, rows):
    b, length, _ = h.shape
    steps = length // rows
    per = rows // WINDOW
    n_units = min(rows, _SWA_PASS_ROWS) // WINDOW * N_KV_HEADS
    seq_spec = pl.BlockSpec((1, rows, D_MODEL), lambda i, j: (i, j, 0))
    rope_spec = pl.BlockSpec((ROT_HALF, rows), lambda i, j: (0, j))
    prev_blk = lambda j: jnp.maximum(j * per - 1, 0)
    in_specs = [
        pl.BlockSpec(memory_space=pltpu.SMEM),
        seq_spec, _const_spec((1, D_MODEL)), _const_spec((1, D_MODEL)),
        _const_spec((D_MODEL, D_MODEL)), _const_spec((D_MODEL, D_MODEL)),
        rope_spec, rope_spec,
        pl.BlockSpec((1, rows, K_PAD), lambda i, j: (i, j, 0)),
        pl.BlockSpec((1, WINDOW, K_PAD), lambda i, j: (i, prev_blk(j), 0)),
        pl.BlockSpec((1, KV_DIM, rows), lambda i, j: (i, 0, j)),
        pl.BlockSpec((1, KV_DIM, WINDOW), lambda i, j: (i, 0, prev_blk(j))),
        _const_spec((N_META, K_PAD)), _const_spec((KV_DIM, N_META)),
    ]
    return pl.pallas_call(
        functools.partial(_swa_kernel, rows=rows),
        grid=(b, steps),
        in_specs=in_specs,
        out_specs=seq_spec,
        out_shape=jax.ShapeDtypeStruct(h.shape, F32),
        scratch_shapes=[pltpu.VMEM((D_MODEL, rows), F32),
                        pltpu.VMEM((n_units, N_META + 2 * WINDOW, GROUP * WINDOW), F32),
                        pltpu.VMEM((n_units, N_META + 2 * WINDOW, GROUP * WINDOW), BF16)],
        compiler_params=pltpu.CompilerParams(dimension_semantics=("parallel", "parallel"),
                                             vmem_limit_bytes=_VMEM_LIMIT),
        name="swa",
    )(sinks, h, g_pre, g_post, wqt, w_o, cos_t, sin_t, k_pad, k_pad, vt, vt, k_meta, vt_meta)


def _rope_angles(positions):
    inv_freq = ROPE_THETA ** (-jnp.arange(0, ROT_DIM, 2, dtype=F32) / ROT_DIM)
    ang = positions.astype(F32)[:, None] * inv_freq[None, :]
    return jnp.cos(ang), jnp.sin(ang)


def _k_rope_tables(cos, sin):
    n = cos.shape[0]
    rest = HEAD_DIM - ROT_DIM
    rc = jnp.concatenate([cos, cos, jnp.ones((n, rest), F32)], axis=1)
    ra = jnp.concatenate([-sin, jnp.zeros((n, ROT_HALF + rest), F32)], axis=1)
    rb = jnp.concatenate([jnp.zeros((n, ROT_HALF), F32), sin, jnp.zeros((n, rest), F32)], axis=1)
    return tuple(jnp.tile(t, (1, LANES // HEAD_DIM)) for t in (rc, ra, rb))


def kernel(x, meta_tokens, norm_gains, w_ffn_in, w_ffn_out, gla_w_in, gla_w_gate, gla_b_gate, gla_norm, gla_w_out,
           kv_norm, w_kv, swa_w_q, swa_sinks, swa_w_out):
    batch, seq, _ = x.shape
    depth = norm_gains.shape[0]
    n_a = gla_w_in.shape[0]
    tm, gla_rows, swa_rows = _FFN_ROWS, _GLA_ROWS, _SWA_ROWS
    assert (batch * seq) % tm == 0 and seq % tm == 0 and seq % gla_rows == 0 and seq % swa_rows == 0

    half_step = jnp.array([1.0, 0.5, 1.0, 1.0, 1.0, 0.5], F32)[None, :, None]
    gains = (norm_gains * half_step).reshape(depth, 6, 1, D_MODEL)
    n_ffn = batch * seq // tm
    ffn_order = [(layer, which) for layer in range(depth) for which in range(2)]
    ffn_w = {}

    def ffn_main(h, layer, which, g_pre, g_post, **kw):
        pos = ffn_order.index((layer, which))
        nxt = ffn_order[pos + 1] if pos + 1 < len(ffn_order) else None
        cast = (w_ffn_in, w_ffn_out, nxt) if nxt is not None else None
        out = _ffn(h, g_pre, g_post, *ffn_w[(layer, which)], tm=tm, cast_next=cast, **kw)
        if nxt is None:
            return out
        *out, nxt_in, nxt_out = out
        ffn_w[nxt] = (nxt_in, nxt_out)
        return out if len(out) > 1 else out[0]

    h = x.reshape(batch * seq, D_MODEL)
    hm = jnp.pad(meta_tokens.astype(x.dtype), ((0, META_ROWS - N_META), (0, 0)))

    cos_m, sin_m = _rope_angles(jnp.arange(META_ROWS))
    cos_r, sin_r = _rope_angles(jnp.arange(N_META, N_META + seq))
    g_kv = kv_norm.reshape(1, D_MODEL)

    k_pad = vt = k_meta = vt_meta = w_kv_b = kv_w = None
    for layer in range(depth):
        g = gains[layer]
        last = layer == depth - 1
        if layer < n_a:
            hm = _ffn(hm, g[0], g[1], (w_ffn_in, (layer, 0)), (w_ffn_out, (layer, 0)), tm=META_ROWS,
                      cast_self=layer == 0)
            if layer == 0:
                hm, *ffn_w[(0, 0)] = hm
        if (layer, 0) not in ffn_w:
            ffn_w[(layer, 0)] = (w_ffn_in[layer, 0].astype(BF16), w_ffn_out[layer, 0].astype(BF16))
        if layer < n_a:
            a = layer
            w_in_a = gla_w_in[a]
            side = [_cast_job(w_in_a, (), n_ffn), _cast_job(gla_w_out, (a,), n_ffn)]
            if layer == 0:
                side.append(_cast_job(w_kv, (), n_ffn))
                h, gla_w_in_b, gla_w_out_b, w_kv_b = ffn_main(h, layer, 0, g[0], g[1], side=side)
            else:
                h, gla_w_in_b, gla_w_out_b = ffn_main(h, layer, 0, g[0], g[1], side=side)
            w_lr = jnp.pad(w_in_a[:, GLA_IN_MAIN:], ((0, 0), (0, LANES - GLA_RANK)))
            w_gate_a = jnp.pad(gla_w_gate[a], ((0, LANES - GLA_RANK), (0, 0)))
            gla_vecs = (gla_b_gate[a].reshape(1, GLA_QK), gla_norm[a].reshape(1, GLA_DV))
            s_zero = jnp.zeros((GLA_HEADS, GLA_DK, GLA_DV), F32)
            hm3, s_meta = _gla(hm[None], s_zero, g[2], g[3], w_in_a, a, w_lr, w_gate_a, *gla_vecs, gla_w_out[a],
                               rows=META_ROWS, valid_rows=N_META, emit_state=True)
            hm = hm3[0]
            h = _gla(h.reshape(batch, seq, D_MODEL), s_meta[0], g[2], g[3], gla_w_in_b, a, w_lr.astype(BF16),
                     w_gate_a.astype(BF16), *gla_vecs, gla_w_out_b, rows=gla_rows)[0]
            h = h.reshape(batch * seq, D_MODEL)
        else:
            b = layer - n_a
            side = [_transpose_job(swa_w_q, (b,), n_ffn, slice(0, D_MODEL), HEAD_DIM ** -0.5 * _LOG2_E),
                    _cast_job(swa_w_out, (b,), n_ffn)]
            if b == 0:
                h, k_pad, vt, wqt, w_o = ffn_main(h, layer, 0, g[0], g[1], side=side, kv_of="in",
                                                  kv=kv_w + _k_rope_tables(cos_r, sin_r), tiles_per_seq=seq // tm)
            else:
                h, wqt, w_o = ffn_main(h, layer, 0, g[0], g[1], side=side)
            h = _swa(h.reshape(batch, seq, D_MODEL), swa_sinks[b], g[2], g[3], wqt, w_o,
                     cos_r.T, sin_r.T, k_pad.reshape(batch, seq, K_PAD), vt, k_meta, vt_meta, rows=swa_rows)
            h = h.reshape(batch * seq, D_MODEL)
        if layer == n_a - 1:
            kv_w = (g_kv, w_kv_b)
            h = ffn_main(h, layer, 1, g[4], g[5])
            _, k_m, vt_m = _ffn(hm, g[4], g[5], *ffn_w[(layer, 1)], tm=META_ROWS,
                                kv=kv_w + _k_rope_tables(cos_m, sin_m))
            k_meta = k_m[:N_META]
            vt_meta = vt_m[0, :, :N_META]
        else:
            h = ffn_main(h, layer, 1, g[4], g[5])
            if not last and layer < n_a:
                hm = _ffn(hm, g[4], g[5], (w_ffn_in, (layer, 1)), (w_ffn_out, (layer, 1)), tm=META_ROWS)
    return h.reshape(batch, seq, D_MODEL)
```
